```python
import math
import jax, jax.numpy as jnp
from jax import lax
import numpy as np

D_MODEL = 2048
BATCH = 16
SEQ = 256
DEPTH = 2
DEC_BATCH = 2
DEC_SEQ = 4096
PAST_LEN = 512

GRID_W = 64
D_MIX = D_MODEL
HG_WIDTH = D_MIX // 4
HG_HEADS = 4
HG_DK = HG_WIDTH // HG_HEADS
HG_DV = HG_WIDTH // HG_HEADS
HG_CHUNK = 64
RG_WIDTH = D_MIX // 4
RG_HEADS = 4
RG_BLOCK = RG_WIDTH // RG_HEADS
RG_CONV = 4
RG_C = 8.0
MLA_WIDTH = D_MIX - HG_WIDTH - RG_WIDTH
MLA_HEADS = 8
MLA_V_DIM = MLA_WIDTH // MLA_HEADS
MLA_NOPE_DIM = 128
MLA_ROPE_DIM = 64
MLA_QK_DIM = MLA_NOPE_DIM + MLA_ROPE_DIM
Q_LORA = 512
KV_LORA = 512
ROPE_THETA = 10000.0
Q_BLOCK = 128
IN_COLS = 5 * HG_WIDTH + 2 * RG_WIDTH + Q_LORA + KV_LORA + MLA_ROPE_DIM
N_EXPERTS = 16
CAP_FACTOR = 2
D_FF_EXPERT = 1024
DN_ALPHA = (2.0 * DEPTH) ** 0.25
DN_BETA = (8.0 * DEPTH) ** -0.25
LN_EPS = 1e-5
RMS_EPS = 1e-6

kernel_name = "hybrid_hgrn2_rglru_mla_ec_moe_diffusion_step"


def split_points():
    sizes = [HG_WIDTH] * 5 + [RG_WIDTH] * 2 + [Q_LORA, KV_LORA, MLA_ROPE_DIM]
    return [int(s) for s in np.cumsum(sizes)[:-1]]


def layer_norm(x, g, b):
    xf = x.astype(jnp.float32)
    mu = jnp.mean(xf, -1, keepdims=True)
    var = jnp.mean(jnp.square(xf - mu), -1, keepdims=True)
    return ((xf - mu) * lax.rsqrt(var + LN_EPS) * g + b).astype(x.dtype)


def rms_norm(x, g):
    xf = x.astype(jnp.float32)
    return (xf * lax.rsqrt(jnp.mean(xf * xf, -1, keepdims=True) + RMS_EPS) * g).astype(x.dtype)


def axial_rope_tables(n):
    rows = n // GRID_W
    t = jnp.arange(n)
    row = jnp.repeat(jnp.arange(rows), GRID_W).astype(jnp.float32)
    col = (t % GRID_W).astype(jnp.float32)
    half = MLA_ROPE_DIM // 2
    inv = ROPE_THETA ** (-jnp.arange(0, half, 2, dtype=jnp.float32) / half)
    ar, ac = row[:, None] * inv, col[:, None] * inv
    return jnp.cos(ar), jnp.sin(ar), jnp.cos(ac), jnp.sin(ac)


def rope_1d(x, cos, sin):
    x1, x2 = jnp.split(x, 2, axis=-1)
    return jnp.concatenate([x1 * cos - x2 * sin, x2 * cos + x1 * sin], -1)


def apply_axial_rope(x, tables):
    cr, sr, cc, sc = (a.reshape(a.shape[:1] + (1,) * (x.ndim - 3) + a.shape[1:]) for a in tables)
    xr, xc = jnp.split(x, 2, axis=-1)
    return jnp.concatenate([rope_1d(xr, cr, sr), rope_1d(xc, cc, sc)], -1).astype(x.dtype)


def block_attention(q, k, v):
    b, tq, h, dq = q.shape
    nb = tq // Q_BLOCK
    scale = dq ** -0.5
    qb = q.reshape(b, nb, Q_BLOCK, h, dq).transpose(1, 0, 2, 3, 4)

    def one(qblk):
        s = jnp.einsum('bqhd,bkhd->bhqk', qblk, k, preferred_element_type=jnp.float32) * scale
        p = jax.nn.softmax(s, axis=-1)
        return jnp.einsum('bhqk,bkhd->bqhd', p.astype(v.dtype), v)

    o = lax.map(one, qb)
    return o.transpose(1, 0, 2, 3, 4).reshape(b, tq, h, v.shape[-1])


def hgrn2_chunk_scan(q, k, v, log_f, s0):
    b, t, h, _ = q.shape
    n = t // HG_CHUNK

    def to_chunks(a):
        return a.reshape(b, n, HG_CHUNK, h, a.shape[-1]).transpose(1, 0, 3, 2, 4)

    causal = jnp.tril(jnp.ones((HG_CHUNK, HG_CHUNK), bool))[:, :, None]

    def step(s, inp):
        qc, kc, vc, lf = inp
        cum = jnp.cumsum(lf, axis=-2)
        inter = jnp.einsum('bhcd,bhde->bhce', qc * jnp.exp(cum), s)
        diff = cum[:, :, :, None, :] - cum[:, :, None, :, :]
        decay = jnp.exp(jnp.where(causal, diff, -jnp.inf))
        att = jnp.einsum('bhtd,bhsd,bhtsd->bhts', qc, kc, decay)
        o = inter + jnp.einsum('bhts,bhse->bhte', att, vc)
        last = cum[:, :, -1:, :]
        s_new = jnp.exp(last[:, :, 0, :])[..., None] * s + jnp.einsum('bhsd,bhse->bhde', kc * jnp.exp(last - cum), vc)
        return s_new, o

    s_fin, o = lax.scan(step, s0, (to_chunks(q), to_chunks(k), to_chunks(v), to_chunks(log_f)))
    return o.transpose(1, 0, 3, 2, 4).reshape(b, t, h, v.shape[-1]), s_fin


def hgrn2_direction(q, k, v, log_f, s0, reverse):
    if reverse:
        q, k, v, log_f = (jnp.flip(a, axis=1) for a in (q, k, v, log_f))
    o, s = hgrn2_chunk_scan(q, k, v, log_f, s0)
    if reverse:
        o = jnp.flip(o, axis=1)
    return o, s


def linear_scan(a, bx, h0):
    def comb(l, r):
        return l[0] * r[0], r[0] * l[1] + r[1]
    a_cum, b_cum = lax.associative_scan(comb, (a, bx), axis=1)
    h = a_cum * h0[:, None, :] + b_cum
    return h, h[:, -1]


def rglru_direction(xc, w_r, b_r, w_i, b_i, lam, h0, reverse):
    if reverse:
        xc = jnp.flip(xc, axis=1)
    b, t, w = xc.shape
    xb = xc.reshape(b, t, RG_HEADS, RG_BLOCK)
    r = jax.nn.sigmoid(jnp.einsum('bthi,hij->bthj', xb, w_r.astype(jnp.float32)).reshape(b, t, w) + b_r)
    ig = jax.nn.sigmoid(jnp.einsum('bthi,hij->bthj', xb, w_i.astype(jnp.float32)).reshape(b, t, w) + b_i)
    log_a = -RG_C * r * jax.nn.softplus(-lam.astype(jnp.float32))
    a = jnp.exp(log_a)
    gated = jnp.sqrt(-jnp.expm1(2.0 * log_a)) * (ig * xc)
    h, h_last = linear_scan(a, gated, h0)
    if reverse:
        h = jnp.flip(h, axis=1)
    return h, h_last


def centred_dwconv(x, w, bias):
    y = lax.conv_general_dilated(
        x, w[:, None, :], window_strides=(1,),
        padding=[(RG_CONV // 2, RG_CONV - 1 - RG_CONV // 2)],
        dimension_numbers=('NWC', 'WIO', 'NWC'), feature_group_count=x.shape[-1])
    return y + bias


def mixer(h, p, ctx=None):
    f32 = jnp.float32
    b, t, _ = h.shape
    z = jnp.einsum('btd,de->bte', h, p['w_in'])
    (q_hg, ff_hg, fb_hg, i_hg, g_hg, x_rg, y_rg, cq, ckv_raw, kpe) = jnp.split(z, split_points(), axis=-1)

    qh = jax.nn.silu(q_hg.astype(f32)).reshape(b, t, HG_HEADS, HG_DK)
    vh = i_hg.astype(f32).reshape(b, t, HG_HEADS, HG_DV)
    hg_outs, hg_states = [], []
    for d, f_raw in enumerate((ff_hg, fb_hg)):
        lb = p['hg_lb'][d]
        f = (lb + (1.0 - lb) * jax.nn.sigmoid(f_raw.astype(f32))).reshape(b, t, HG_HEADS, HG_DK)
        s0 = jnp.zeros((b, HG_HEADS, HG_DK, HG_DV), f32) if ctx is None else ctx['hg'][:, d].astype(f32)
        o, s = hgrn2_direction(qh, 1.0 - f, vh, jnp.log(f), s0, reverse=(d == 1))
        hg_outs.append(o)
        hg_states.append(s)
    o_hg = rms_norm(hg_outs[0] + hg_outs[1], p['hg_norm_g'].reshape(HG_HEADS, HG_DV))
    o_hg = o_hg * jax.nn.silu(g_hg.astype(f32)).reshape(b, t, HG_HEADS, HG_DV)
    o_hg = o_hg.reshape(b, t, HG_WIDTH).astype(h.dtype)

    xc = centred_dwconv(x_rg, p['rg_conv_w'], p['rg_conv_b']).astype(f32)
    rg_outs, rg_states = [], []
    for d in range(2):
        h0 = jnp.zeros((b, RG_WIDTH), f32) if ctx is None else ctx['rg'][:, d].astype(f32)
        hd, hl = rglru_direction(xc, p['rg_w_r'][d], p['rg_b_r'][d], p['rg_w_i'][d], p['rg_b_i'][d],
                                 p['rg_lambda'][d], h0, reverse=(d == 1))
        rg_outs.append(hd)
        rg_states.append(hl)
    o_rg = ((rg_outs[0] + rg_outs[1]) * jax.nn.gelu(y_rg.astype(f32))).astype(h.dtype)

    qm = jnp.einsum('btr,re->bte', rms_norm(cq, p['q_norm_g']), p['w_uq']).reshape(b, t, MLA_HEADS, MLA_QK_DIM)
    q_nope, q_pe = qm[..., :MLA_NOPE_DIM], qm[..., MLA_NOPE_DIM:]
    ckv = rms_norm(ckv_raw, p['kv_norm_g'])
    if ctx is None:
        ckv_all, kpe_all = ckv, kpe
    else:
        tabs = axial_rope_tables(t)
        q_pe = apply_axial_rope(q_pe, tabs)
        ckv_all = jnp.concatenate([ckv, ctx['ckv'].astype(ckv.dtype)], axis=1)
        kpe_all = jnp.concatenate([apply_axial_rope(kpe, tabs), ctx['kpe'].astype(kpe.dtype)], axis=1)
    tk = ckv_all.shape[1]
    k_nope = jnp.einsum('btr,re->bte', ckv_all, p['w_uk']).reshape(b, tk, MLA_HEADS, MLA_NOPE_DIM)
    v = jnp.einsum('btr,re->bte', ckv_all, p['w_uv']).reshape(b, tk, MLA_HEADS, MLA_V_DIM)
    k = jnp.concatenate([k_nope, jnp.broadcast_to(kpe_all[:, :, None, :], (b, tk, MLA_HEADS, MLA_ROPE_DIM))], -1)
    o_mla = block_attention(jnp.concatenate([q_nope, q_pe], -1), k, v).reshape(b, t, MLA_WIDTH)

    out = jnp.einsum('bte,ed->btd', jnp.concatenate([o_hg, o_rg, o_mla.astype(h.dtype)], -1), p['w_out'])
    if ctx is None:
        return out, (ckv, kpe, jnp.stack(hg_states, 1), jnp.stack(rg_states, 1))
    return out


def expert_choice_ffn(h, w_router, w_gate, w_up, w_down):
    b, t, d = h.shape
    n = b * t
    cap = CAP_FACTOR * n // N_EXPERTS
    xf = h.reshape(n, d)
    probs = jax.nn.softmax(jnp.einsum('nd,de->ne', xf, w_router).astype(jnp.float32), axis=-1)
    gates, idx = lax.top_k(probs.T, cap)
    xs = jnp.take(xf, idx, axis=0)
    hid = jax.nn.silu(jnp.einsum('ecd,edf->ecf', xs, w_gate)) * jnp.einsum('ecd,edf->ecf', xs, w_up)
    ys = jnp.einsum('ecf,efd->ecd', hid, w_down) * gates[..., None].astype(h.dtype)
    out = jnp.zeros_like(xf).at[idx.reshape(-1)].add(ys.reshape(-1, d))
    return out.reshape(b, t, d)


def trunk_layer(x, mod, p, ctx=None):
    sh1, sc1, g1, sh2, sc2, g2 = jnp.split(mod, 6, axis=-1)
    hm = x * (1.0 + sc1) + sh1
    if ctx is None:
        m, st = mixer(hm, p)
    else:
        m, st = mixer(hm, p, ctx), None
    x = layer_norm(DN_ALPHA * x + g1 * m, p['ln1_g'], p['ln1_b'])
    hf = x * (1.0 + sc2) + sh2
    f = expert_choice_ffn(hf, p['moe_router'], p['moe_w_gate'], p['moe_w_up'], p['moe_w_down'])
    x = layer_norm(DN_ALPHA * x + g2 * f, p['ln2_g'], p['ln2_b'])
    return x, st


def setup_inputs(seed: int = 0) -> dict:
    key = jax.random.key(seed)
    ks = jax.random.split(key, 40)
    f32 = jnp.float32

    def nrm(k, shape, scale=1.0):
        return jax.random.normal(k, shape, f32) * scale

    a8 = jax.random.uniform(ks[15], (DEPTH, 2, RG_WIDTH), f32, minval=0.9, maxval=0.999)
    a0 = a8 ** (1.0 / RG_C)
    return {
        'x_prompt': nrm(ks[0], (BATCH, SEQ, D_MODEL)),
        'x_sample': nrm(ks[1], (DEC_BATCH, DEC_SEQ, D_MODEL)),
        'cache_mla_ckv': nrm(ks[2], (DEC_BATCH, DEPTH, PAST_LEN, KV_LORA)),
        'cache_mla_kpe': nrm(ks[3], (DEC_BATCH, DEPTH, PAST_LEN, MLA_ROPE_DIM)),
        'state_hgrn': nrm(ks[4], (DEC_BATCH, DEPTH, 2, HG_HEADS, HG_DK, HG_DV), 0.5),
        'state_rglru': nrm(ks[5], (DEC_BATCH, DEPTH, 2, RG_WIDTH), 0.5),
        'c': nrm(ks[6], (DEC_BATCH, D_MODEL)),
        'c_ctx': nrm(ks[7], (D_MODEL,)),
        'w_in': nrm(ks[8], (DEPTH, D_MODEL, IN_COLS), D_MODEL ** -0.5),
        'w_out': nrm(ks[9], (DEPTH, D_MIX, D_MODEL), DN_BETA * D_MIX ** -0.5),
        'hg_lb_logits': nrm(ks[10], (DEPTH, 2, HG_WIDTH)),
        'hg_norm_g': 1.0 + nrm(ks[11], (DEPTH, HG_WIDTH), 0.02),
        'rg_conv_w': nrm(ks[12], (DEPTH, RG_CONV, RG_WIDTH), RG_CONV ** -0.5),
        'rg_conv_b': nrm(ks[13], (DEPTH, RG_WIDTH), 0.02),
        'rg_w_r': nrm(ks[14], (DEPTH, 2, RG_HEADS, RG_BLOCK, RG_BLOCK), RG_BLOCK ** -0.5),
        'rg_b_r': nrm(ks[16], (DEPTH, 2, RG_WIDTH), 0.02),
        'rg_w_i': nrm(ks[17], (DEPTH, 2, RG_HEADS, RG_BLOCK, RG_BLOCK), RG_BLOCK ** -0.5),
        'rg_b_i': nrm(ks[18], (DEPTH, 2, RG_WIDTH), 0.02),
        'rg_lambda': jnp.log(a0) - jnp.log1p(-a0),
        'mla_q_norm_g': 1.0 + nrm(ks[19], (DEPTH, Q_LORA), 0.02),
        'mla_kv_norm_g': 1.0 + nrm(ks[20], (DEPTH, KV_LORA), 0.02),
        'mla_w_uq': nrm(ks[21], (DEPTH, Q_LORA, MLA_HEADS * MLA_QK_DIM), Q_LORA ** -0.5),
        'mla_w_uk': nrm(ks[22], (DEPTH, KV_LORA, MLA_HEADS * MLA_NOPE_DIM), KV_LORA ** -0.5),
        'mla_w_uv': nrm(ks[23], (DEPTH, KV_LORA, MLA_HEADS * MLA_V_DIM), DN_BETA * KV_LORA ** -0.5),
        'ada_w': nrm(ks[24], (DEPTH, D_MODEL, 6 * D_MODEL), 0.5 * D_MODEL ** -0.5),
        'ada_b': nrm(ks[25], (DEPTH, 6 * D_MODEL), 0.02),
        'ln1_g': 1.0 + nrm(ks[26], (DEPTH, D_MODEL), 0.02),
        'ln1_b': nrm(ks[27], (DEPTH, D_MODEL), 0.02),
        'ln2_g': 1.0 + nrm(ks[28], (DEPTH, D_MODEL), 0.02),
        'ln2_b': nrm(ks[29], (DEPTH, D_MODEL), 0.02),
        'moe_router': nrm(ks[30], (DEPTH, D_MODEL, N_EXPERTS), D_MODEL ** -0.5),
        'moe_w_gate': nrm(ks[31], (DEPTH, N_EXPERTS, D_MODEL, D_FF_EXPERT), D_MODEL ** -0.5),
        'moe_w_up': nrm(ks[32], (DEPTH, N_EXPERTS, D_MODEL, D_FF_EXPERT), DN_BETA * D_MODEL ** -0.5),
        'moe_w_down': nrm(ks[33], (DEPTH, N_EXPERTS, D_FF_EXPERT, D_MODEL), DN_BETA * D_FF_EXPERT ** -0.5),
    }


def reference(x_prompt, x_sample, cache_mla_ckv, cache_mla_kpe, state_hgrn, state_rglru, c, c_ctx,
              w_in, w_out, hg_lb_logits, hg_norm_g, rg_conv_w, rg_conv_b, rg_w_r, rg_b_r, rg_w_i, rg_b_i,
              rg_lambda, mla_q_norm_g, mla_kv_norm_g, mla_w_uq, mla_w_uk, mla_w_uv, ada_w, ada_b,
              ln1_g, ln1_b, ln2_g, ln2_b, moe_router, moe_w_gate, moe_w_up, moe_w_down):
    lb_cum = jnp.cumsum(jax.nn.softmax(hg_lb_logits.astype(jnp.float32), axis=0), axis=0)
    lb_all = lb_cum - lb_cum[0:1]

    def layer_params(l):
        return {
            'w_in': w_in[l], 'w_out': w_out[l], 'hg_lb': lb_all[l], 'hg_norm_g': hg_norm_g[l],
            'rg_conv_w': rg_conv_w[l], 'rg_conv_b': rg_conv_b[l], 'rg_w_r': rg_w_r[l], 'rg_b_r': rg_b_r[l],
            'rg_w_i': rg_w_i[l], 'rg_b_i': rg_b_i[l], 'rg_lambda': rg_lambda[l],
            'q_norm_g': mla_q_norm_g[l], 'kv_norm_g': mla_kv_norm_g[l],
            'w_uq': mla_w_uq[l], 'w_uk': mla_w_uk[l], 'w_uv': mla_w_uv[l],
            'ln1_g': ln1_g[l], 'ln1_b': ln1_b[l], 'ln2_g': ln2_g[l], 'ln2_b': ln2_b[l],
            'moe_router': moe_router[l], 'moe_w_gate': moe_w_gate[l], 'moe_w_up': moe_w_up[l],
            'moe_w_down': moe_w_down[l],
        }

    y_prompt = x_prompt
    ckvs, kpes, hgs, rgs = [], [], [], []
    for l in range(DEPTH):
        p = layer_params(l)
        mod = (jnp.einsum('d,de->e', jax.nn.silu(c_ctx), ada_w[l]) + ada_b[l])[None, None, :]
        y_prompt, (ckv_l, kpe_l, hg_l, rg_l) = trunk_layer(y_prompt, mod, p)
        ckvs.append(ckv_l)
        kpes.append(kpe_l)
        hgs.append(hg_l)
        rgs.append(rg_l)
    new_mla_ckv = jnp.stack(ckvs, axis=1)
    new_mla_kpe = jnp.stack(kpes, axis=1)
    new_state_hgrn = jnp.stack(hgs, axis=1)
    new_state_rglru = jnp.stack(rgs, axis=1)

    y_sample = x_sample
    for l in range(DEPTH):
        p = layer_params(l)
        mod = (jnp.einsum('bd,de->be', jax.nn.silu(c), ada_w[l]) + ada_b[l])[:, None, :]
        ctx = {'ckv': cache_mla_ckv[:, l], 'kpe': cache_mla_kpe[:, l],
               'hg': state_hgrn[:, l], 'rg': state_rglru[:, l]}
        y_sample, _ = trunk_layer(y_sample, mod, p, ctx)

    return (y_prompt, y_sample, new_mla_ckv, new_mla_kpe, new_state_hgrn, new_state_rglru)
```

```python
import functools
import math

import jax
import jax.numpy as jnp
import numpy as np
from jax import lax
from jax.experimental import pallas as pl
from jax.experimental.pallas import tpu as pltpu

F32 = jnp.float32
BF16 = jnp.bfloat16

D_MODEL = 2048
BATCH = 16
SEQ = 256
DEPTH = 2
DEC_BATCH = 2
DEC_SEQ = 4096
PAST_LEN = 512
GRID_W = 64
HG_WIDTH = 512
HG_HEADS = 4
HG_DK = 128
RG_WIDTH = 512
RG_HEADS = 4
RG_BLOCK = 128
RG_CONV = 4
RG_C = 8.0
MLA_WIDTH = 1024
MLA_HEADS = 8
MLA_V_DIM = 128
MLA_NOPE_DIM = 128
MLA_ROPE_DIM = 64
MLA_QK_DIM = 192
Q_LORA = 512
KV_LORA = 512
ROPE_THETA = 10000.0
Q_BLOCK = 128
N_EXPERTS = 16
CAP_FACTOR = 2
D_FF_EXPERT = 1024
DN_ALPHA = (2.0 * DEPTH) ** 0.25
LN_EPS = 1e-5
RMS_EPS = 1e-6

N_CTX = BATCH * SEQ
N_LAT = DEC_BATCH * DEC_SEQ
N_TOK = N_CTX + N_LAT
N_MODROWS = 1 + DEC_BATCH

LANES = 128
SUBLANES = 8
VMEM_LIMIT_BYTES = 56 * 1024 * 1024

W_HG = 5 * HG_WIDTH
W_RG = 2 * RG_WIDTH
W_KV = KV_LORA + 2 * MLA_ROPE_DIM
IN_GROUPS = (W_HG, W_RG, Q_LORA, W_KV)
IN_COLS_OWN = sum(IN_GROUPS)


def _params(sem, vmem=VMEM_LIMIT_BYTES):
    return pltpu.CompilerParams(dimension_semantics=sem, vmem_limit_bytes=vmem)


def _mod_row(i, tm):
    n_ctx_tiles = N_CTX // tm
    per_batch = DEC_SEQ // tm
    return jnp.where(i < n_ctx_tiles, 0, 1 + (i - n_ctx_tiles) // per_batch)


ADA_TN = 1024


def _ada_kernel(c_ref, w_ref, b_ref, o_ref):
    c = c_ref[...]
    s = (c * jax.nn.sigmoid(c)).astype(BF16)
    o_ref[...] = jnp.dot(s, w_ref[...].astype(BF16), preferred_element_type=F32) + b_ref[...]


def ada_mod(cvec, ada_w, ada_b):
    ncol = 6 * D_MODEL
    return pl.pallas_call(
        _ada_kernel,
        grid=(DEPTH, ncol // ADA_TN),
        in_specs=[
            pl.BlockSpec((SUBLANES, D_MODEL), lambda l, j: (0, 0)),
            pl.BlockSpec((None, D_MODEL, ADA_TN), lambda l, j: (l, 0, j)),
            pl.BlockSpec((None, 1, ADA_TN), lambda l, j: (l, 0, j)),
        ],
        out_specs=pl.BlockSpec((None, SUBLANES, ADA_TN), lambda l, j: (l, 0, j)),
        out_shape=jax.ShapeDtypeStruct((DEPTH, SUBLANES, ncol), F32),
        compiler_params=_params(("arbitrary", "arbitrary")),
        name="ada_mod",
    )(cvec, ada_w, ada_b.reshape(DEPTH, 1, ncol))


INPROJ_TM = 256


def _inproj_kernel(x_ref, mod_ref, w_ref, ohg_ref, org_ref, ocq_ref, okv_ref):
    m = mod_ref[...]
    hm = (x_ref[...] * (1.0 + m[1:2]) + m[0:1]).astype(BF16)
    a = 0
    for o_ref, width in zip((ohg_ref, org_ref, ocq_ref, okv_ref), IN_GROUPS):
        o_ref[...] = jnp.dot(hm, w_ref[:, a:a + width], preferred_element_type=F32)
        a += width


def in_proj(x, mod6, w_in_l, layer):
    tm = INPROJ_TM
    return pl.pallas_call(
        _inproj_kernel,
        grid=(N_TOK // tm,),
        in_specs=[
            pl.BlockSpec((tm, D_MODEL), lambda i: (i, 0)),
            pl.BlockSpec((None, 6, D_MODEL), lambda i: (layer * N_MODROWS + _mod_row(i, tm), 0, 0)),
            pl.BlockSpec((D_MODEL, IN_COLS_OWN), lambda i: (0, 0), pipeline_mode=pl.Buffered(1)),
        ],
        out_specs=[pl.BlockSpec((tm, w), lambda i: (i, 0)) for w in IN_GROUPS],
        out_shape=[jax.ShapeDtypeStruct((N_TOK, w), F32) for w in IN_GROUPS],
        compiler_params=_params(("arbitrary",)),
        name="in_proj",
    )(x, mod6, w_in_l)


def _rot_cols(w):
    quarter = MLA_ROPE_DIM // 4
    j = np.arange(MLA_ROPE_DIM)
    first = (j % (2 * quarter)) < quarter
    src = np.where(first, j + quarter, j - quarter)
    sign = np.where(first, -1.0, 1.0).astype(np.float32)
    return w[..., src] * sign


def prep_w_in(w_in_l):
    kpe = w_in_l[:, -MLA_ROPE_DIM:]
    return jnp.concatenate([w_in_l, _rot_cols(kpe)], axis=1).astype(BF16)


HG_CHUNK = 64
HG_SAFE_EXP = 80.0


def _silu(x):
    return x * jax.nn.sigmoid(x)


def _hg_chunk(q, k, v, lf, st, tri, mask, rows):
    first, mid, last = rows
    c = q.shape[0]
    cum = jnp.dot(tri, lf, precision=lax.Precision.HIGHEST, preferred_element_type=F32)
    c_first, c_mid, c_last = cum[first:first + 1], cum[mid:mid + 1], cum[last:last + 1]
    guard = jnp.max(jnp.maximum(c_first - c_mid, c_mid - c_last))
    nt = (((1,), (1,)), ((), ()))
    q_in = (q * jnp.exp(cum)).astype(BF16)
    inter = lax.dot_general(q_in, st.astype(BF16), nt, preferred_element_type=F32)

    def fast():
        qt = (q * jnp.exp(cum - c_mid)).astype(BF16)
        kt = (k * jnp.exp(c_mid - cum)).astype(BF16)
        return lax.dot_general(qt, kt, nt, preferred_element_type=F32)

    def exact():
        lane = lax.broadcasted_iota(jnp.int32, (c, c), 1)
        row = lax.broadcasted_iota(jnp.int32, (c, 1), 0)

        def body(s_idx, att):
            sel = row == s_idx
            cum_s = jnp.sum(jnp.where(sel, cum, 0.0), axis=0, keepdims=True)
            k_s = jnp.sum(jnp.where(sel, k, 0.0), axis=0, keepdims=True)
            dec = jnp.exp(jnp.minimum(cum - cum_s, 0.0))
            col = jnp.sum(q * k_s * dec, axis=1, keepdims=True)
            return jnp.where(lane == s_idx, col, att)

        return lax.fori_loop(0, c, body, jnp.zeros((c, c), F32))

    att = jnp.where(mask, lax.cond(guard > HG_SAFE_EXP, exact, fast), 0.0)
    o = inter + jnp.dot(att.astype(BF16), v.astype(BF16), preferred_element_type=F32)
    k_end = (k * jnp.exp(c_last - cum)).astype(BF16)
    tn = (((0,), (0,)), ((), ()))
    st_new = st * jnp.exp(c_last) + lax.dot_general(v.astype(BF16), k_end, tn, preferred_element_type=F32)
    return o, st_new


def _hgrn_kernel(*refs, layer, seq_len, has_state):
    if has_state:
        (q_ref, ff_ref, fb_ref, i_ref, g_ref, lbl_ref, ng_ref, s0_ref, o_ref, of_scr, ob_scr) = refs
    else:
        (q_ref, ff_ref, fb_ref, i_ref, g_ref, lbl_ref, ng_ref, o_ref, sfin_ref, of_scr, ob_scr) = refs
    c = HG_CHUNK
    n_chunks = seq_len // c

    lg = lbl_ref[...]
    e = jnp.exp(lg - jnp.max(lg, axis=0, keepdims=True))
    sm = e / jnp.sum(e, axis=0, keepdims=True)
    lb = jnp.zeros_like(sm[0])
    for j in range(1, layer + 1):
        lb = lb + sm[j]
    lb_f, lb_b = lb[0:1], lb[1:2]

    r = lax.broadcasted_iota(jnp.int32, (c, c), 0)
    s = lax.broadcasted_iota(jnp.int32, (c, c), 1)
    causal, anti = r >= s, r <= s
    tri_f, tri_b = causal.astype(F32), anti.astype(F32)

    if has_state:
        st_f0, st_b0 = s0_ref[0].T, s0_ref[1].T
    else:
        st_f0 = st_b0 = jnp.zeros((HG_DK, HG_DK), F32)

    def load(rows, fr_ref, lbd):
        q = _silu(q_ref[rows, :])
        f = lbd + (1.0 - lbd) * jax.nn.sigmoid(fr_ref[rows, :])
        return q, 1.0 - f, i_ref[rows, :], jnp.log(f)

    def body(ci, carry):
        st_f, st_b = carry
        rows_f = pl.ds(pl.multiple_of(ci * c, c), c)
        rows_b = pl.ds(pl.multiple_of((n_chunks - 1 - ci) * c, c), c)
        q, k, v, lf = load(rows_f, ff_ref, lb_f)
        o, st_f = _hg_chunk(q, k, v, lf, st_f, tri_f, causal, (0, c // 2 - 1, c - 1))
        of_scr[rows_f, :] = o
        q, k, v, lf = load(rows_b, fb_ref, lb_b)
        o, st_b = _hg_chunk(q, k, v, lf, st_b, tri_b, anti, (c - 1, c // 2, 0))
        ob_scr[rows_b, :] = o
        return st_f, st_b

    st_f, st_b = lax.fori_loop(0, n_chunks, body, (st_f0, st_b0))
    if not has_state:
        sfin_ref[0] = st_f.T
        sfin_ref[1] = st_b.T

    o = of_scr[...] + ob_scr[...]
    o = o * lax.rsqrt(jnp.mean(o * o, axis=-1, keepdims=True) + RMS_EPS) * ng_ref[...]
    o_ref[...] = o * _silu(g_ref[...])


def hgrn_mixer(z_hg, lb_logits, norm_g, state, layer, latent):
    seq_len = DEC_SEQ if latent else SEQ
    n_seq = DEC_BATCH if latent else BATCH
    blk0 = N_CTX // seq_len if latent else 0
    h = HG_HEADS

    def col(group):
        return pl.BlockSpec((seq_len, LANES), lambda b, hh: (blk0 + b, group * h + hh))

    in_specs = [col(0), col(1), col(2), col(3), col(4),
                pl.BlockSpec((DEPTH, 2, LANES), lambda b, hh: (0, 0, hh)),
                pl.BlockSpec((None, 1, LANES), lambda b, hh: (layer, 0, hh))]
    args = [z_hg] * 5 + [lb_logits, norm_g]
    o_spec = pl.BlockSpec((seq_len, LANES), lambda b, hh: (b, hh))
    o_shape = jax.ShapeDtypeStruct((n_seq * seq_len, HG_WIDTH), F32)
    if latent:
        in_specs.append(pl.BlockSpec((None, None, 2, None, HG_DK, HG_DK), lambda b, hh: (b, layer, 0, hh, 0, 0)))
        args.append(state)
        out_specs, out_shape = [o_spec], [o_shape]
    else:
        out_specs = [o_spec, pl.BlockSpec((None, 2, None, HG_DK, HG_DK), lambda b, hh: (b, 0, hh, 0, 0))]
        out_shape = [o_shape, jax.ShapeDtypeStruct((BATCH, 2, h, HG_DK, HG_DK), F32)]
    return pl.pallas_call(
        functools.partial(_hgrn_kernel, layer=layer, seq_len=seq_len, has_state=latent),
        grid=(n_seq, h),
        in_specs=in_specs,
        out_specs=out_specs,
        out_shape=out_shape,
        scratch_shapes=[pltpu.VMEM((seq_len, LANES), F32), pltpu.VMEM((seq_len, LANES), F32)],
        compiler_params=_params(("arbitrary", "arbitrary")),
        name="hgrn_lat" if latent else "hgrn_ctx",
    )(*args)


RG_ROWS = 256
RG_PAD = SUBLANES


def _tile_scan(a, b, reverse):
    row = lax.broadcasted_iota(jnp.int32, a.shape, 0)
    for sh in (1, 2, 4):
        if reverse:
            a_s, b_s = pltpu.roll(a, SUBLANES - sh, 0), pltpu.roll(b, SUBLANES - sh, 0)
            valid = row < SUBLANES - sh
        else:
            a_s, b_s = pltpu.roll(a, sh, 0), pltpu.roll(b, sh, 0)
            valid = row >= sh
        b = jnp.where(valid, a * b_s + b, b)
        a = jnp.where(valid, a * a_s, a)
    return a, b


def _rglru_kernel(*refs, seq_len, has_state):
    if has_state:
        (x_ref, y_ref, cw_ref, cb_ref, wr_ref, br_ref, wi_ref, bi_ref, lam_ref, h0_ref,
         o_ref, xp_scr, a_scr, b_scr) = refs
    else:
        (x_ref, y_ref, cw_ref, cb_ref, wr_ref, br_ref, wi_ref, bi_ref, lam_ref,
         o_ref, hfin_ref, xp_scr, a_scr, b_scr) = refs
    t = seq_len
    zeros = jnp.zeros((RG_PAD, LANES), F32)
    xp_scr[0:RG_PAD, :] = zeros
    xp_scr[RG_PAD + t:, :] = zeros
    xp_scr[RG_PAD:RG_PAD + t, :] = x_ref[...]

    cw = cw_ref[...]
    cb = cb_ref[...]
    lam = lam_ref[...]
    sp = jnp.maximum(-lam, 0.0) + jnp.log1p(jnp.exp(-jnp.abs(lam)))

    def gates(ci, _):
        r0 = pl.multiple_of(ci * RG_ROWS, RG_ROWS)
        xc = cb
        for j in range(RG_CONV):
            xc = xc + cw[j:j + 1] * xp_scr[pl.ds(r0 + RG_PAD - RG_CONV // 2 + j, RG_ROWS), :]
        rows = pl.ds(r0, RG_ROWS)
        xcb = xc.astype(BF16)
        for d in range(2):
            r = jax.nn.sigmoid(jnp.dot(xcb, wr_ref[d], preferred_element_type=F32) + br_ref[d])
            ig = jax.nn.sigmoid(jnp.dot(xcb, wi_ref[d], preferred_element_type=F32) + bi_ref[d])
            log_a = -RG_C * r * sp[d:d + 1]
            a_scr[d, rows, :] = jnp.exp(log_a)
            b_scr[d, rows, :] = jnp.sqrt(1.0 - jnp.exp(2.0 * log_a)) * (ig * xc)
        return 0

    lax.fori_loop(0, t // RG_ROWS, gates, 0)

    n_tiles = t // SUBLANES

    def scan(j, carry):
        h_f, h_b = carry
        rows_f = pl.ds(pl.multiple_of(j * SUBLANES, SUBLANES), SUBLANES)
        rows_b = pl.ds(pl.multiple_of((n_tiles - 1 - j) * SUBLANES, SUBLANES), SUBLANES)
        aa, bb = _tile_scan(a_scr[0, rows_f, :], b_scr[0, rows_f, :], False)
        hf = aa * h_f + bb
        b_scr[0, rows_f, :] = hf
        aa, bb = _tile_scan(a_scr[1, rows_b, :], b_scr[1, rows_b, :], True)
        hb = aa * h_b + bb
        b_scr[1, rows_b, :] = hb
        return hf[SUBLANES - 1:SUBLANES], hb[0:1]

    if has_state:
        h0 = h0_ref[...]
        init = (h0[0:1], h0[1:2])
    else:
        init = (jnp.zeros((1, LANES), F32), jnp.zeros((1, LANES), F32))
    h_f, h_b = lax.fori_loop(0, n_tiles, scan, init)
    if not has_state:
        hfin_ref[0:1, :] = h_f
        hfin_ref[1:2, :] = h_b
    o_ref[...] = (b_scr[0] + b_scr[1]) * jax.nn.gelu(y_ref[...], approximate=True)


def rglru_mixer(z_rg, conv_w, conv_b, w_r, b_r, w_i, b_i, lam, state, layer, latent):
    seq_len = DEC_SEQ if latent else SEQ
    n_seq = DEC_BATCH if latent else BATCH
    blk0 = N_CTX // seq_len if latent else 0
    h = RG_HEADS
    vec = lambda rows: pl.BlockSpec((None, rows, LANES), lambda b, hh: (layer, 0, hh))
    wspec = pl.BlockSpec((None, 2, None, RG_BLOCK, RG_BLOCK), lambda b, hh: (layer, 0, hh, 0, 0))
    bspec = pl.BlockSpec((None, 2, 1, LANES), lambda b, hh: (layer, 0, 0, hh))
    in_specs = [
        pl.BlockSpec((seq_len, LANES), lambda b, hh: (blk0 + b, hh)),
        pl.BlockSpec((seq_len, LANES), lambda b, hh: (blk0 + b, h + hh)),
        vec(RG_CONV), vec(1), wspec, bspec, wspec, bspec, vec(2),
    ]
    args = [z_rg, z_rg, conv_w, conv_b, w_r, b_r.reshape(DEPTH, 2, 1, RG_WIDTH), w_i,
            b_i.reshape(DEPTH, 2, 1, RG_WIDTH), lam]
    o_spec = pl.BlockSpec((seq_len, LANES), lambda b, hh: (b, hh))
    o_shape = jax.ShapeDtypeStruct((n_seq * seq_len, RG_WIDTH), F32)
    if latent:
        in_specs.append(pl.BlockSpec((None, None, 2, LANES), lambda b, hh: (b, layer, 0, hh)))
        args.append(state)
        out_specs, out_shape = [o_spec], [o_shape]
    else:
        out_specs = [o_spec, pl.BlockSpec((None, 2, LANES), lambda b, hh: (b, 0, hh))]
        out_shape = [o_shape, jax.ShapeDtypeStruct((BATCH, 2, RG_WIDTH), F32)]
    return pl.pallas_call(
        functools.partial(_rglru_kernel, seq_len=seq_len, has_state=latent),
        grid=(n_seq, h),
        in_specs=in_specs,
        out_specs=out_specs,
        out_shape=out_shape,
        scratch_shapes=[pltpu.VMEM((seq_len + 2 * RG_PAD, LANES), F32),
                        pltpu.VMEM((2, seq_len, LANES), F32), pltpu.VMEM((2, seq_len, LANES), F32)],
        compiler_params=_params(("arbitrary", "arbitrary")),
        name="rglru_lat" if latent else "rglru_ctx",
    )(*args)


MLA_HEAD_PAD = 2 * LANES
MLA_TM = 512
ATT_TQ = 256


def rope_tables():
    half = MLA_ROPE_DIM // 2
    t = np.arange(DEC_SEQ)
    row = (t // GRID_W).astype(np.float32)
    col = (t % GRID_W).astype(np.float32)
    inv = (ROPE_THETA ** (-np.arange(0, half, 2, dtype=np.float32) / half)).astype(np.float32)
    ar, ac = row[:, None] * inv, col[:, None] * inv
    cos = np.concatenate([np.cos(ar), np.cos(ar), np.cos(ac), np.cos(ac)], -1)
    sin = np.concatenate([np.sin(ar), np.sin(ar), np.sin(ac), np.sin(ac)], -1)
    pad = np.zeros((DEC_SEQ, LANES - MLA_ROPE_DIM), np.float32)
    cos_lat = np.tile(np.concatenate([cos, pad], -1), (DEC_BATCH, 1))
    sin_lat = np.tile(np.concatenate([sin, pad], -1), (DEC_BATCH, 1))
    cos_ctx = np.concatenate([np.ones((N_CTX, MLA_ROPE_DIM), np.float32), np.zeros((N_CTX, LANES - MLA_ROPE_DIM), np.float32)], -1)
    sin_ctx = np.zeros((N_CTX, LANES), np.float32)
    return (jnp.asarray(np.concatenate([cos_ctx, cos_lat], 0).astype(np.float32)),
            jnp.asarray(np.concatenate([sin_ctx, sin_lat], 0).astype(np.float32)))


def _rope_group(x, cos, sin):
    return x * cos + pltpu.roll(x, MLA_ROPE_DIM, 1) * sin


def _rms(x, g):
    return x * lax.rsqrt(jnp.mean(x * x, axis=-1, keepdims=True) + RMS_EPS) * g


def _qproj_kernel(cq_ref, g_ref, w_ref, cos_ref, sin_ref, q_ref):
    xn = _rms(cq_ref[...], g_ref[...]).astype(BF16)
    qm = jnp.dot(xn, w_ref[...], preferred_element_type=F32)
    cos, sin = cos_ref[...], sin_ref[...]
    for h in range(MLA_HEADS):
        a = h * MLA_HEAD_PAD
        q_ref[:, a:a + LANES] = qm[:, a:a + LANES].astype(BF16)
        q_ref[:, a + LANES:a + 2 * LANES] = _rope_group(qm[:, a + LANES:a + 2 * LANES], cos, sin).astype(BF16)


def q_proj(z_cq, q_norm_g, w_uq_l, cos_t, sin_t, layer):
    tm = MLA_TM
    width = MLA_HEADS * MLA_HEAD_PAD
    return pl.pallas_call(
        _qproj_kernel,
        grid=(N_TOK // tm,),
        in_specs=[
            pl.BlockSpec((tm, Q_LORA), lambda i: (i, 0)),
            pl.BlockSpec((None, 1, Q_LORA), lambda i: (layer, 0, 0)),
            pl.BlockSpec((Q_LORA, width), lambda i: (0, 0)),
            pl.BlockSpec((tm, LANES), lambda i: (i, 0)),
            pl.BlockSpec((tm, LANES), lambda i: (i, 0)),
        ],
        out_specs=pl.BlockSpec((tm, width), lambda i: (i, 0)),
        out_shape=jax.ShapeDtypeStruct((N_TOK, width), BF16),
        compiler_params=_params(("arbitrary",)),
        name="q_proj",
    )(z_cq, q_norm_g, w_uq_l, cos_t, sin_t)


def prep_w_uq(w_uq_l):
    w = w_uq_l.reshape(Q_LORA, MLA_HEADS, MLA_QK_DIM)
    pe = w[..., MLA_NOPE_DIM:]
    w = jnp.concatenate([w, _rot_cols(pe)], axis=-1)
    return w.reshape(Q_LORA, MLA_HEADS * MLA_HEAD_PAD).astype(BF16)


def _kvproj_kernel(zkv_ref, g_ref, wk_ref, wv_ref, cos_ref, sin_ref, *outs, normalize):
    if normalize:
        k_ref, v_ref, ckv_ref, kpe_ref = outs
    else:
        k_ref, v_ref = outs
    ckv = zkv_ref[:, 0:KV_LORA]
    if normalize:
        ckv = _rms(ckv, g_ref[...])
        ckv_ref[...] = ckv
    pe_group = zkv_ref[:, KV_LORA:KV_LORA + LANES]
    if normalize:
        kpe_ref[...] = pe_group[:, 0:MLA_ROPE_DIM]
    pe = _rope_group(pe_group, cos_ref[...], sin_ref[...]).astype(BF16)
    cb = ckv.astype(BF16)
    kn = jnp.dot(cb, wk_ref[...], preferred_element_type=F32)
    v_ref[...] = jnp.dot(cb, wv_ref[...], preferred_element_type=F32).astype(BF16)
    for h in range(MLA_HEADS):
        a = h * MLA_HEAD_PAD
        k_ref[:, a:a + LANES] = kn[:, h * LANES:(h + 1) * LANES].astype(BF16)
        k_ref[:, a + LANES:a + 2 * LANES] = pe


def kv_proj(z_kv, kv_norm_g, w_uk_l, w_uv_l, cos_t, sin_t, layer):
    tm = MLA_TM
    kw, vw = MLA_HEADS * MLA_HEAD_PAD, MLA_HEADS * MLA_V_DIM
    return pl.pallas_call(
        functools.partial(_kvproj_kernel, normalize=True),
        grid=(N_TOK // tm,),
        in_specs=[
            pl.BlockSpec((tm, W_KV), lambda i: (i, 0)),
            pl.BlockSpec((None, 1, KV_LORA), lambda i: (layer, 0, 0)),
            pl.BlockSpec((KV_LORA, MLA_HEADS * MLA_NOPE_DIM), lambda i: (0, 0)),
            pl.BlockSpec((KV_LORA, vw), lambda i: (0, 0)),
            pl.BlockSpec((tm, LANES), lambda i: (i, 0)),
            pl.BlockSpec((tm, LANES), lambda i: (i, 0)),
        ],
        out_specs=[
            pl.BlockSpec((tm, kw), lambda i: (i, 0)),
            pl.BlockSpec((tm, vw), lambda i: (i, 0)),
            pl.BlockSpec((tm, KV_LORA), lambda i: (i, 0)),
            pl.BlockSpec((tm, MLA_ROPE_DIM), lambda i: (i, 0)),
        ],
        out_shape=[
            jax.ShapeDtypeStruct((N_TOK, kw), BF16),
            jax.ShapeDtypeStruct((N_TOK, vw), BF16),
            jax.ShapeDtypeStruct((N_TOK, KV_LORA), F32),
            jax.ShapeDtypeStruct((N_TOK, MLA_ROPE_DIM), F32),
        ],
        compiler_params=_params(("arbitrary",)),
        name="kv_proj",
    )(z_kv, kv_norm_g, w_uk_l, w_uv_l, cos_t, sin_t)


def kv_proj_cache(cache_kv, kv_norm_g, w_uk_l, w_uv_l, cos_c, sin_c, layer):
    tm = PAST_LEN
    kw, vw = MLA_HEADS * MLA_HEAD_PAD, MLA_HEADS * MLA_V_DIM
    n = DEC_BATCH * PAST_LEN
    return pl.pallas_call(
        functools.partial(_kvproj_kernel, normalize=False),
        grid=(DEC_BATCH,),
        in_specs=[
            pl.BlockSpec((tm, W_KV), lambda b: (b, 0)),
            pl.BlockSpec((None, 1, KV_LORA), lambda b: (layer, 0, 0)),
            pl.BlockSpec((KV_LORA, MLA_HEADS * MLA_NOPE_DIM), lambda b: (0, 0)),
            pl.BlockSpec((KV_LORA, vw), lambda b: (0, 0)),
            pl.BlockSpec((tm, LANES), lambda b: (0, 0)),
            pl.BlockSpec((tm, LANES), lambda b: (0, 0)),
        ],
        out_specs=[pl.BlockSpec((tm, kw), lambda b: (b, 0)), pl.BlockSpec((tm, vw), lambda b: (b, 0))],
        out_shape=[jax.ShapeDtypeStruct((n, kw), BF16), jax.ShapeDtypeStruct((n, vw), BF16)],
        compiler_params=_params(("arbitrary",)),
        name="kv_proj_cache",
    )(cache_kv, kv_norm_g, w_uk_l, w_uv_l, cos_c, sin_c)


def _attn_kernel(q_ref, *refs):
    o_ref = refs[-1]
    segs = [(refs[i], refs[i + 1]) for i in range(0, len(refs) - 1, 2)]
    nt = (((1,), (1,)), ((), ()))
    q = q_ref[...]
    scores = [lax.dot_general(q, k_ref[...], nt, preferred_element_type=F32) for k_ref, _ in segs]
    m = scores[0].max(axis=-1, keepdims=True)
    for s in scores[1:]:
        m = jnp.maximum(m, s.max(axis=-1, keepdims=True))
    l, o = 0.0, 0.0
    for s, (_, v_ref) in zip(scores, segs):
        p = jnp.exp((s - m) * (MLA_QK_DIM ** -0.5))
        l = l + jnp.sum(p, axis=-1, keepdims=True)
        o = o + jnp.dot(p.astype(BF16), v_ref[...], preferred_element_type=F32)
    o_ref[...] = o / l


def mla_attention(q, k_tok, v_tok, k_cache, v_cache, latent):
    h = MLA_HEADS
    if latent:
        tq = ATT_TQ
        n_q = DEC_SEQ // tq
        grid = (DEC_BATCH, h, n_q)
        q_map = lambda b, hh, i: (N_CTX // tq + b * n_q + i, hh)
        kv_map = lambda b, hh, i: (N_CTX // DEC_SEQ + b, hh)
        o_map = lambda b, hh, i: (b * n_q + i, hh)
        in_specs = [
            pl.BlockSpec((tq, MLA_HEAD_PAD), q_map),
            pl.BlockSpec((DEC_SEQ, MLA_HEAD_PAD), kv_map),
            pl.BlockSpec((DEC_SEQ, MLA_V_DIM), kv_map),
            pl.BlockSpec((PAST_LEN, MLA_HEAD_PAD), lambda b, hh, i: (b, hh)),
            pl.BlockSpec((PAST_LEN, MLA_V_DIM), lambda b, hh, i: (b, hh)),
        ]
        args = [q, k_tok, v_tok, k_cache, v_cache]
        n_out = N_LAT
    else:
        tq = SEQ
        grid = (BATCH, h, 1)
        q_map = o_map = lambda b, hh, i: (b, hh)
        in_specs = [
            pl.BlockSpec((tq, MLA_HEAD_PAD), q_map),
            pl.BlockSpec((SEQ, MLA_HEAD_PAD), q_map),
            pl.BlockSpec((SEQ, MLA_V_DIM), q_map),
        ]
        args = [q, k_tok, v_tok]
        n_out = N_CTX
    return pl.pallas_call(
        _attn_kernel,
        grid=grid,
        in_specs=in_specs,
        out_specs=pl.BlockSpec((tq, MLA_V_DIM), o_map),
        out_shape=jax.ShapeDtypeStruct((n_out, MLA_WIDTH), F32),
        compiler_params=_params(("arbitrary",) * 3),
        name="mla_attn_lat" if latent else "mla_attn_ctx",
    )(*args)


OUT_TM = 256


def _layer_norm(y, g, b):
    mu = jnp.mean(y, axis=-1, keepdims=True)
    yc = y - mu
    var = jnp.mean(yc * yc, axis=-1, keepdims=True)
    return yc * lax.rsqrt(var + LN_EPS) * g + b


def _outproj_kernel(ohg_ref, org_ref, omla_ref, w_ref, x_ref, mod_ref, g_ref, b_ref, wr_ref,
                    x1_ref, hf_ref, pt_ref):
    m = jnp.dot(ohg_ref[...].astype(BF16), w_ref[0:HG_WIDTH, :], preferred_element_type=F32)
    m += jnp.dot(org_ref[...].astype(BF16), w_ref[HG_WIDTH:HG_WIDTH + RG_WIDTH, :], preferred_element_type=F32)
    m += jnp.dot(omla_ref[...].astype(BF16), w_ref[HG_WIDTH + RG_WIDTH:, :], preferred_element_type=F32)
    md = mod_ref[...]
    x1 = _layer_norm(DN_ALPHA * x_ref[...] + md[2:3] * m, g_ref[...], b_ref[...])
    x1_ref[...] = x1
    hf = x1 * (1.0 + md[4:5]) + md[3:4]
    hf_ref[...] = hf
    logits = jnp.dot(hf.astype(BF16), wr_ref[...], preferred_element_type=F32)
    lane = lax.broadcasted_iota(jnp.int32, logits.shape, 1)
    logits = jnp.where(lane < N_EXPERTS, logits, -jnp.inf)
    e = jnp.exp(logits - jnp.max(logits, axis=-1, keepdims=True))
    p = e / jnp.sum(e, axis=-1, keepdims=True)
    pt_ref[...] = p.T[0:N_EXPERTS, :]


def out_proj(o_hg, o_rg, o_mla, w_out_l, x, mod6, ln_g, ln_b, w_router_l, layer):
    tm = OUT_TM
    row = lambda w: pl.BlockSpec((tm, w), lambda i: (i, 0))
    const = lambda shape: pl.BlockSpec(shape, lambda i: (0,) * len(shape), pipeline_mode=pl.Buffered(1))
    return pl.pallas_call(
        _outproj_kernel,
        grid=(N_TOK // tm,),
        in_specs=[
            row(HG_WIDTH), row(RG_WIDTH), row(MLA_WIDTH),
            const((D_MODEL, D_MODEL)),
            row(D_MODEL),
            pl.BlockSpec((None, 6, D_MODEL), lambda i: (layer * N_MODROWS + _mod_row(i, tm), 0, 0)),
            pl.BlockSpec((None, 1, D_MODEL), lambda i: (layer, 0, 0)),
            pl.BlockSpec((None, 1, D_MODEL), lambda i: (layer, 0, 0)),
            const((D_MODEL, LANES)),
        ],
        out_specs=[row(D_MODEL), row(D_MODEL), pl.BlockSpec((N_EXPERTS, tm), lambda i: (0, i))],
        out_shape=[jax.ShapeDtypeStruct((N_TOK, D_MODEL), F32),
                   jax.ShapeDtypeStruct((N_TOK, D_MODEL), F32),
                   jax.ShapeDtypeStruct((N_EXPERTS, N_TOK), F32)],
        compiler_params=_params(("arbitrary",)),
        name="out_proj",
    )(o_hg, o_rg, o_mla, w_out_l, x, mod6, ln_g, ln_b, w_router_l)


CAP_CTX = CAP_FACTOR * N_CTX // N_EXPERTS
CAP_LAT = CAP_FACTOR * N_LAT // N_EXPERTS
ROUTE_SETS = ((0, N_CTX, CAP_CTX), (N_CTX, N_LAT, CAP_LAT))
SLOTS = CAP_CTX + CAP_LAT


def _route_thr_kernel(pt_ref, thr_ref, need_ref):
    for si, (start, n, cap) in enumerate(ROUTE_SETS):
        p = pt_ref[:, start:start + n]

        def count(mask):
            return jnp.sum(mask.astype(F32), axis=1, keepdims=True)

        def body(i, t):
            cand = t | jnp.left_shift(jnp.int32(1), 30 - i)
            return jnp.where(count(p >= pltpu.bitcast(cand, F32)) >= cap, cand, t)

        t = pltpu.bitcast(lax.fori_loop(0, 31, body, jnp.zeros((N_EXPERTS, 1), jnp.int32)), F32)
        need = cap - count(p > t)
        thr_ref[si] = jnp.broadcast_to(t, (N_EXPERTS, LANES))
        need_ref[si] = jnp.broadcast_to(need, (N_EXPERTS, LANES))


def route_threshold(p_t):
    n_sets = len(ROUTE_SETS)
    return pl.pallas_call(
        _route_thr_kernel,
        out_shape=[jax.ShapeDtypeStruct((n_sets, N_EXPERTS, LANES), F32),
                   jax.ShapeDtypeStruct((n_sets, N_EXPERTS, LANES), F32)],
        compiler_params=_params(None),
        name="route_threshold",
    )(p_t)


def _route_lists_kernel(p_ref, thr_ref, need_ref, lists_ref, cnt_ref, first_ref, sel_ref, *, nb, cap, tok0, row0):
    nt = (((1,), (1,)), ((), ()))
    r128 = lax.broadcasted_iota(jnp.int32, (LANES, LANES), 0)
    c128 = lax.broadcasted_iota(jnp.int32, (LANES, LANES), 1)
    incl = (r128 <= c128).astype(BF16)
    eye = r128 == c128
    rb = lax.broadcasted_iota(jnp.int32, (nb, nb), 0)
    cb = lax.broadcasted_iota(jnp.int32, (nb, nb), 1)
    below = (cb < rb).astype(BF16)
    incl_b = (rb <= cb).astype(BF16)
    ones8 = jnp.ones((SUBLANES, LANES), BF16)
    s_col = lax.broadcasted_iota(jnp.int32, (cap, 1), 0).astype(F32)
    lane = lax.broadcasted_iota(jnp.int32, (cap, LANES), 1)
    lane_f = lane.astype(F32)
    j_row = lax.broadcasted_iota(jnp.int32, (1, nb), 1).astype(F32)

    def dot(a, b):
        return jnp.dot(a, b, preferred_element_type=F32)

    def block_base(totals, unit):
        hi = jnp.floor(totals * (1.0 / unit))
        lo = totals - unit * hi
        bc = lambda a: jnp.broadcast_to(a, (nb, LANES)).astype(BF16)
        return unit * dot(below, bc(hi)) + dot(below, bc(lo))

    def choose(e, acc):
        p = p_ref[e]
        t = thr_ref[pl.ds(e, 1), :]
        need = need_ref[pl.ds(e, 1), :]
        eq = p == t
        eq_f = eq.astype(F32)
        eq_lp = dot(eq_f.astype(BF16), incl)
        tie_rank = eq_lp - eq_f + block_base(eq_lp[:, LANES - 1:LANES], 16.0)
        sel_f = jnp.where((p > t) | (eq & (tie_rank < need)), 1.0, 0.0)
        sel_ref[e] = sel_f
        return acc + sel_f

    cnt = lax.fori_loop(0, N_EXPERTS, choose, jnp.zeros((nb, LANES), F32))
    cnt_lp = dot(cnt.astype(BF16), incl)
    first = cnt_lp - cnt + block_base(cnt_lp[:, LANES - 1:LANES], 64.0) + row0
    first_hi = jnp.floor(first * (1.0 / LANES))
    first_lo = first - LANES * first_hi

    def expert(e, acc):
        p = p_ref[e]
        sel_f = sel_ref[e]
        sel_b = sel_f.astype(BF16)
        lp = dot(sel_b, incl)
        c_row = lax.dot_general(ones8, sel_b, nt, preferred_element_type=F32)
        incl_row = dot(c_row.astype(BF16), incl_b)[0:1]
        excl_row = incl_row - c_row[0:1]
        oh_j = jnp.where((s_col >= excl_row) & (s_col < incl_row), 1.0, 0.0)
        base_s = jnp.sum(oh_j * excl_row, axis=1, keepdims=True)
        j_s = jnp.sum(oh_j * j_row, axis=1, keepdims=True)
        oh_jb = oh_j.astype(BF16)
        lp_rows = dot(oh_jb, lp.astype(BF16))
        pos = jnp.sum(jnp.where(lp_rows <= s_col - base_s, 1.0, 0.0), axis=1, keepdims=True)
        oh_c = lane_f == pos
        p1 = p.astype(BF16)
        r1 = p - p1.astype(F32)
        p2 = r1.astype(BF16)
        p3 = (r1 - p2.astype(F32)).astype(BF16)
        p_rows = dot(oh_jb, p1) + dot(oh_jb, p2) + dot(oh_jb, p3)
        gate = jnp.sum(jnp.where(oh_c, p_rows, 0.0), axis=1, keepdims=True)
        rank_rows = dot(oh_jb, acc.astype(BF16))
        first_rows = LANES * dot(oh_jb, first_hi.astype(BF16)) + dot(oh_jb, first_lo.astype(BF16))
        dst = jnp.sum(jnp.where(oh_c, rank_rows + first_rows, 0.0), axis=1, keepdims=True)
        idx = tok0 + LANES * j_s + pos
        lists_ref[e] = jnp.where(lane == 0, idx, jnp.where(lane == 1, dst, jnp.where(lane == 2, gate, 0.0)))
        return acc + sel_f

    lax.fori_loop(0, N_EXPERTS, expert, jnp.zeros((nb, LANES), F32))

    ones_b = jnp.ones((LANES, LANES), BF16)

    def column(a, j):
        diag = jnp.where(eye, jnp.broadcast_to(a[j:j + 1, :], (LANES, LANES)), 0.0)
        return dot(diag.astype(BF16), ones_b)

    for j in range(nb):
        rows = slice(j * LANES, (j + 1) * LANES)
        cnt_ref[rows, :] = column(cnt, j)
        first_ref[rows, :] = LANES * column(first_hi, j) + column(first_lo, j)


def route_lists(p_blk, thr, need, set_index):
    tok0, n, cap = ROUTE_SETS[set_index]
    nb = n // LANES
    row0 = float(CAP_FACTOR * tok0)
    kern = functools.partial(_route_lists_kernel, nb=nb, cap=cap, tok0=tok0, row0=row0)
    return pl.pallas_call(
        kern,
        grid=(1,),
        in_specs=[
            pl.BlockSpec((N_EXPERTS, nb, LANES), lambda i: (0, 0, 0)),
            pl.BlockSpec((None, N_EXPERTS, LANES), lambda i: (set_index, 0, 0)),
            pl.BlockSpec((None, N_EXPERTS, LANES), lambda i: (set_index, 0, 0)),
        ],
        out_specs=[pl.BlockSpec((N_EXPERTS, cap, LANES), lambda i: (0, 0, 0)),
                   pl.BlockSpec((n, LANES), lambda i: (0, 0)),
                   pl.BlockSpec((n, LANES), lambda i: (0, 0))],
        out_shape=[jax.ShapeDtypeStruct((N_EXPERTS, cap, LANES), F32),
                   jax.ShapeDtypeStruct((n, LANES), F32),
                   jax.ShapeDtypeStruct((n, LANES), F32)],
        scratch_shapes=[pltpu.VMEM((N_EXPERTS, nb, LANES), F32)],
        compiler_params=_params(("arbitrary",)),
        name="route_lists_lat" if set_index else "route_lists_ctx",
    )(p_blk, thr, need)


def _row_copy_kernel(src_idx_ref, dst_idx_ref, src_ref, dst_ref, sem, *, n_rows):
    def issue(i, _):
        pltpu.make_async_copy(src_ref.at[pl.ds(src_idx_ref[0, 0, i], 1), :],
                              dst_ref.at[pl.ds(dst_idx_ref[0, 0, i], 1), :], sem).start()
        return 0

    lax.fori_loop(0, n_rows, issue, 0)
    pltpu.make_async_copy(dst_ref.at[pl.ds(0, n_rows), :], dst_ref.at[pl.ds(0, n_rows), :], sem).wait()


def row_copy(src_idx, dst_idx, src, dst_rows):
    g, n_rows = src_idx.shape
    assert g * n_rows == dst_rows
    width = src.shape[1]
    idx_spec = pl.BlockSpec((1, 1, n_rows), lambda i: (i, 0, 0), memory_space=pltpu.SMEM)
    return pl.pallas_call(
        functools.partial(_row_copy_kernel, n_rows=n_rows),
        grid=(g,),
        in_specs=[idx_spec, idx_spec, pl.BlockSpec(memory_space=pl.ANY)],
        out_specs=pl.BlockSpec(memory_space=pl.ANY),
        out_shape=jax.ShapeDtypeStruct((dst_rows, width), src.dtype),
        scratch_shapes=[pltpu.SemaphoreType.DMA(())],
        compiler_params=_params(("arbitrary",)),
        name="row_copy",
    )(src_idx.reshape(g, 1, n_rows), dst_idx.reshape(g, 1, n_rows), src)


FFN_TM = 512


def _ffn_kernel(xs_ref, lists_ref, wg_ref, wu_ref, wd_ref, ys_ref):
    x = xs_ref[...].astype(BF16)
    gate = jnp.dot(x, wg_ref[...], preferred_element_type=F32)
    up = jnp.dot(x, wu_ref[...], preferred_element_type=F32)
    hid = (_silu(gate) * up).astype(BF16)
    ys_ref[...] = jnp.dot(hid, wd_ref[...], preferred_element_type=F32) * lists_ref[:, 2:3]


def expert_ffn(xs, lists, w_gate_l, w_up_l, w_down_l):
    tm = FFN_TM
    per_e = SLOTS // tm
    return pl.pallas_call(
        _ffn_kernel,
        grid=(N_EXPERTS, per_e),
        in_specs=[
            pl.BlockSpec((tm, D_MODEL), lambda e, i: (e * per_e + i, 0)),
            pl.BlockSpec((None, tm, LANES), lambda e, i: (e, i, 0)),
            pl.BlockSpec((None, D_MODEL, D_FF_EXPERT), lambda e, i: (e, 0, 0)),
            pl.BlockSpec((None, D_MODEL, D_FF_EXPERT), lambda e, i: (e, 0, 0)),
            pl.BlockSpec((None, D_FF_EXPERT, D_MODEL), lambda e, i: (e, 0, 0)),
        ],
        out_specs=pl.BlockSpec((tm, D_MODEL), lambda e, i: (e * per_e + i, 0)),
        out_shape=jax.ShapeDtypeStruct((N_EXPERTS * SLOTS, D_MODEL), F32),
        compiler_params=_params(("arbitrary", "arbitrary")),
        name="expert_ffn",
    )(xs, lists, w_gate_l, w_up_l, w_down_l)


COMB_TM = 512
COMB_ROWS = 256
N_CHOICES = N_EXPERTS * SLOTS


def _combine_kernel(rows_ref, y_ref, cnt_ref, first_ref, x1_ref, mod_ref, g_ref, b_ref, o_ref, buf, sem, acc_ref):
    i = pl.program_id(0)
    lc = COMB_ROWS
    r0, r1 = rows_ref[i], rows_ref[i + 1]
    base = lax.div(r0, lc) * lc
    n_chunks = lax.div(r1 - base + lc - 1, lc)

    def chunk_start(k):
        return pl.multiple_of(jnp.minimum(base + k * lc, N_CHOICES - lc), lc)

    def copy(k, slot):
        return pltpu.make_async_copy(y_ref.at[pl.ds(chunk_start(k), lc), :], buf.at[slot], sem.at[slot])

    acc_ref[...] = jnp.zeros_like(acc_ref)

    @pl.when(n_chunks > 0)
    def _():
        copy(0, 0).start()

    first = first_ref[:, 0:1]
    last = first + cnt_ref[:, 0:1]
    col = lax.broadcasted_iota(jnp.int32, (1, lc), 1)

    def body(k, _):
        slot = lax.rem(k, 2)
        copy(k, slot).wait()

        @pl.when(k + 1 < n_chunks)
        def _():
            copy(k + 1, 1 - slot).start()

        row_id = (chunk_start(k) + col).astype(F32)
        mine = (row_id >= first) & (row_id < last) & (row_id >= (base + k * lc).astype(F32))
        onehot = jnp.where(mine, 1.0, 0.0).astype(BF16)
        y = buf[slot]
        hi = y.astype(BF16)
        lo = (y - hi.astype(F32)).astype(BF16)
        acc_ref[...] += (jnp.dot(onehot, hi, preferred_element_type=F32)
                         + jnp.dot(onehot, lo, preferred_element_type=F32))
        return 0

    lax.fori_loop(0, n_chunks, body, 0)
    g2 = mod_ref[...][5:6]
    o_ref[...] = _layer_norm(DN_ALPHA * x1_ref[...] + g2 * acc_ref[...], g_ref[...], b_ref[...])


def moe_combine(tile_rows, y_choices, cnt, first, x1, mod6, ln_g, ln_b, layer):
    tm = COMB_TM
    row = lambda w: pl.BlockSpec((tm, w), lambda i, t: (i, 0))
    grid_spec = pltpu.PrefetchScalarGridSpec(
        num_scalar_prefetch=1,
        grid=(N_TOK // tm,),
        in_specs=[
            pl.BlockSpec(memory_space=pl.ANY),
            row(LANES), row(LANES), row(D_MODEL),
            pl.BlockSpec((None, 6, D_MODEL), lambda i, t: (layer * N_MODROWS + _mod_row(i, tm), 0, 0)),
            pl.BlockSpec((None, 1, D_MODEL), lambda i, t: (layer, 0, 0)),
            pl.BlockSpec((None, 1, D_MODEL), lambda i, t: (layer, 0, 0)),
        ],
        out_specs=row(D_MODEL),
        scratch_shapes=[pltpu.VMEM((2, COMB_ROWS, D_MODEL), F32), pltpu.SemaphoreType.DMA((2,)),
                        pltpu.VMEM((tm, D_MODEL), F32)],
    )
    return pl.pallas_call(
        _combine_kernel,
        grid_spec=grid_spec,
        out_shape=jax.ShapeDtypeStruct((N_TOK, D_MODEL), F32),
        compiler_params=_params(("arbitrary",)),
        name="moe_combine",
    )(tile_rows, y_choices, cnt, first, x1, mod6, ln_g, ln_b)


def moe_block(x1, hf, p_t, mod6, ln_g, ln_b, w_gate_l, w_up_l, w_down_l, layer):
    thr, need = route_threshold(p_t)
    parts = []
    for si, (tok0, n, cap) in enumerate(ROUTE_SETS):
        p_blk = p_t[:, tok0:tok0 + n].reshape(N_EXPERTS, n // LANES, LANES)
        parts.append(route_lists(p_blk, thr, need, si))
    lists = jnp.concatenate([p[0] for p in parts], axis=1)
    cnt = jnp.concatenate([p[1] for p in parts], axis=0)
    first = jnp.concatenate([p[2] for p in parts], axis=0)
    tok = lists[:, :, 0].astype(jnp.int32)
    dst = lists[:, :, 1].astype(jnp.int32)
    slot = jnp.arange(N_CHOICES, dtype=jnp.int32).reshape(N_EXPERTS, SLOTS)
    xs = row_copy(tok, slot, hf, N_CHOICES)
    ys = expert_ffn(xs, lists, w_gate_l, w_up_l, w_down_l)
    y_choices = row_copy(slot, dst, ys, N_CHOICES)
    tile_rows = jnp.concatenate([first[::COMB_TM, 0], jnp.full((1,), N_CHOICES, F32)]).astype(jnp.int32)
    return moe_combine(tile_rows, y_choices, cnt, first, x1, mod6, ln_g, ln_b, layer)


def kernel(x_prompt, x_sample, cache_mla_ckv, cache_mla_kpe, state_hgrn, state_rglru, c, c_ctx,
           w_in, w_out, hg_lb_logits, hg_norm_g, rg_conv_w, rg_conv_b, rg_w_r, rg_b_r, rg_w_i, rg_b_i,
           rg_lambda, mla_q_norm_g, mla_kv_norm_g, mla_w_uq, mla_w_uk, mla_w_uv, ada_w, ada_b,
           ln1_g, ln1_b, ln2_g, ln2_b, moe_router, moe_w_gate, moe_w_up, moe_w_down):
    x = jnp.concatenate([x_prompt.reshape(N_CTX, D_MODEL), x_sample.reshape(N_LAT, D_MODEL)], axis=0)
    cvec = jnp.concatenate([c_ctx[None, :], c, jnp.zeros((SUBLANES - N_MODROWS, D_MODEL), F32)], axis=0)
    mod = ada_mod(cvec, ada_w, ada_b)
    mod6 = mod[:, :N_MODROWS].reshape(DEPTH * N_MODROWS, 6, D_MODEL)
    cos_t, sin_t = rope_tables()
    vec = lambda a: a.reshape(DEPTH, 1, a.shape[-1])
    hg_ng, cb, qg, kg = vec(hg_norm_g), vec(rg_conv_b), vec(mla_q_norm_g), vec(mla_kv_norm_g)
    g1, b1, g2, b2 = vec(ln1_g), vec(ln1_b), vec(ln2_g), vec(ln2_b)
    w_r, w_i = rg_w_r.astype(BF16), rg_w_i.astype(BF16)
    router = jnp.pad(moe_router, ((0, 0), (0, 0), (0, LANES - N_EXPERTS))).astype(BF16)

    ckvs, kpes, hgs, rgs = [], [], [], []
    for l in range(DEPTH):
        z_hg, z_rg, z_cq, z_kv = in_proj(x, mod6, prep_w_in(w_in[l]), l)

        o_hg_c, hg_fin = hgrn_mixer(z_hg, hg_lb_logits, hg_ng, None, l, False)
        (o_hg_l,) = hgrn_mixer(z_hg, hg_lb_logits, hg_ng, state_hgrn, l, True)
        o_hg = jnp.concatenate([o_hg_c, o_hg_l], axis=0)

        rg_args = (rg_conv_w, cb, w_r, rg_b_r, w_i, rg_b_i, rg_lambda)
        o_rg_c, rg_fin = rglru_mixer(z_rg, *rg_args, None, l, False)
        (o_rg_l,) = rglru_mixer(z_rg, *rg_args, state_rglru, l, True)
        o_rg = jnp.concatenate([o_rg_c, o_rg_l], axis=0)

        w_uk, w_uv = mla_w_uk[l].astype(BF16), mla_w_uv[l].astype(BF16)
        q = q_proj(z_cq, qg, prep_w_uq(mla_w_uq[l]), cos_t, sin_t, l)
        k_tok, v_tok, ckv_n, kpe = kv_proj(z_kv, kg, w_uk, w_uv, cos_t, sin_t, l)
        cache = jnp.concatenate([cache_mla_ckv[:, l].reshape(DEC_BATCH * PAST_LEN, KV_LORA),
                                 cache_mla_kpe[:, l].reshape(DEC_BATCH * PAST_LEN, MLA_ROPE_DIM),
                                 jnp.zeros((DEC_BATCH * PAST_LEN, MLA_ROPE_DIM), F32)], axis=1)
        k_cache, v_cache = kv_proj_cache(cache, kg, w_uk, w_uv, cos_t[:PAST_LEN], sin_t[:PAST_LEN], l)
        o_mla = jnp.concatenate([mla_attention(q, k_tok, v_tok, None, None, False),
                                 mla_attention(q, k_tok, v_tok, k_cache, v_cache, True)], axis=0)

        x1, hf, p_t = out_proj(o_hg, o_rg, o_mla, w_out[l].astype(BF16), x, mod6, g1, b1, router[l], l)
        x = moe_block(x1, hf, p_t, mod6, g2, b2, moe_w_gate[l].astype(BF16), moe_w_up[l].astype(BF16),
                      moe_w_down[l].astype(BF16), l)

        ckvs.append(ckv_n[:N_CTX].reshape(BATCH, SEQ, KV_LORA))
        kpes.append(kpe[:N_CTX].reshape(BATCH, SEQ, MLA_ROPE_DIM))
        hgs.append(hg_fin)
        rgs.append(rg_fin)

    y_prompt = x[:N_CTX].reshape(BATCH, SEQ, D_MODEL)
    y_sample = x[N_CTX:].reshape(DEC_BATCH, DEC_SEQ, D_MODEL)
    return (y_prompt, y_sample, jnp.stack(ckvs, axis=1), jnp.stack(kpes, axis=1),
            jnp.stack(hgs, axis=1), jnp.stack(rgs, axis=1))
```

```python
import functools
import math

import jax
import jax.numpy as jnp
import numpy as np
from jax import lax
from jax.experimental import pallas as pl
from jax.experimental.pallas import tpu as pltpu

F32 = jnp.float32
BF16 = jnp.bfloat16

D_MODEL = 2048
BATCH = 16
SEQ = 256
DEPTH = 2
DEC_BATCH = 2
DEC_SEQ = 4096
PAST_LEN = 512
GRID_W = 64
HG_WIDTH = 512
HG_HEADS = 4
HG_DK = 128
RG_WIDTH = 512
RG_HEADS = 4
RG_BLOCK = 128
RG_CONV = 4
RG_C = 8.0
MLA_WIDTH = 1024
MLA_HEADS = 8
MLA_V_DIM = 128
MLA_NOPE_DIM = 128
MLA_ROPE_DIM = 64
MLA_QK_DIM = 192
Q_LORA = 512
KV_LORA = 512
ROPE_THETA = 10000.0
Q_BLOCK = 128
N_EXPERTS = 16
CAP_FACTOR = 2
D_FF_EXPERT = 1024
DN_ALPHA = (2.0 * DEPTH) ** 0.25
LN_EPS = 1e-5
RMS_EPS = 1e-6

N_CTX = BATCH * SEQ
N_LAT = DEC_BATCH * DEC_SEQ
N_TOK = N_CTX + N_LAT
N_MODROWS = 1 + DEC_BATCH

LANES = 128
SUBLANES = 8
VMEM_LIMIT_BYTES = 56 * 1024 * 1024

W_HG = 5 * HG_WIDTH
W_RG = 2 * RG_WIDTH
W_KV = KV_LORA + 2 * MLA_ROPE_DIM
IN_GROUPS = (W_HG, W_RG, Q_LORA, W_KV)
IN_COLS_OWN = sum(IN_GROUPS)


def _params(sem, vmem=VMEM_LIMIT_BYTES):
    return pltpu.CompilerParams(dimension_semantics=sem, vmem_limit_bytes=vmem)


def _mod_row(i, tm):
    n_ctx_tiles = N_CTX // tm
    per_batch = DEC_SEQ // tm
    return jnp.where(i < n_ctx_tiles, 0, 1 + (i - n_ctx_tiles) // per_batch)


ADA_TN = 1024


def _ada_kernel(c_ref, w_ref, b_ref, o_ref):
    c = c_ref[...]
    s = (c * jax.nn.sigmoid(c)).astype(BF16)
    o_ref[...] = jnp.dot(s, w_ref[...].astype(BF16), preferred_element_type=F32) + b_ref[...]


def ada_mod(cvec, ada_w, ada_b):
    ncol = 6 * D_MODEL
    return pl.pallas_call(
        _ada_kernel,
        grid=(DEPTH, ncol // ADA_TN),
        in_specs=[
            pl.BlockSpec((SUBLANES, D_MODEL), lambda l, j: (0, 0)),
            pl.BlockSpec((None, D_MODEL, ADA_TN), lambda l, j: (l, 0, j)),
            pl.BlockSpec((None, 1, ADA_TN), lambda l, j: (l, 0, j)),
        ],
        out_specs=pl.BlockSpec((None, SUBLANES, ADA_TN), lambda l, j: (l, 0, j)),
        out_shape=jax.ShapeDtypeStruct((DEPTH, SUBLANES, ncol), F32),
        compiler_params=_params(("arbitrary", "arbitrary")),
        name="ada_mod",
    )(cvec, ada_w, ada_b.reshape(DEPTH, 1, ncol))


INPROJ_TM = 256


def _inproj_kernel(x_ref, mod_ref, w_ref, ohg_ref, org_ref, ocq_ref, okv_ref):
    m = mod_ref[...]
    hm = (x_ref[...] * (1.0 + m[1:2]) + m[0:1]).astype(BF16)
    a = 0
    for o_ref, width in zip((ohg_ref, org_ref, ocq_ref, okv_ref), IN_GROUPS):
        o_ref[...] = jnp.dot(hm, w_ref[:, a:a + width], preferred_element_type=F32)
        a += width


def in_proj(x, mod6, w_in_l, layer):
    tm = INPROJ_TM
    return pl.pallas_call(
        _inproj_kernel,
        grid=(N_TOK // tm,),
        in_specs=[
            pl.BlockSpec((tm, D_MODEL), lambda i: (i, 0)),
            pl.BlockSpec((None, 6, D_MODEL), lambda i: (layer * N_MODROWS + _mod_row(i, tm), 0, 0)),
            pl.BlockSpec((D_MODEL, IN_COLS_OWN), lambda i: (0, 0), pipeline_mode=pl.Buffered(1)),
        ],
        out_specs=[pl.BlockSpec((tm, w), lambda i: (i, 0)) for w in IN_GROUPS],
        out_shape=[jax.ShapeDtypeStruct((N_TOK, w), F32) for w in IN_GROUPS],
        compiler_params=_params(("arbitrary",)),
        name="in_proj",
    )(x, mod6, w_in_l)


def _rot_cols(w):
    quarter = MLA_ROPE_DIM // 4
    j = np.arange(MLA_ROPE_DIM)
    first = (j % (2 * quarter)) < quarter
    src = np.where(first, j + quarter, j - quarter)
    sign = np.where(first, -1.0, 1.0).astype(np.float32)
    return w[..., src] * sign


def prep_w_in(w_in_l):
    kpe = w_in_l[:, -MLA_ROPE_DIM:]
    return jnp.concatenate([w_in_l, _rot_cols(kpe)], axis=1).astype(BF16)


HG_CHUNK = 64
HG_SAFE_EXP = 80.0


def _silu(x):
    return x * jax.nn.sigmoid(x)


def _hg_chunk(q, k, v, lf, st, tri, mask, rows):
    first, mid, last = rows
    c = q.shape[0]
    cum = jnp.dot(tri, lf, precision=lax.Precision.HIGHEST, preferred_element_type=F32)
    c_first, c_mid, c_last = cum[first:first + 1], cum[mid:mid + 1], cum[last:last + 1]
    guard = jnp.max(jnp.maximum(c_first - c_mid, c_mid - c_last))
    nt = (((1,), (1,)), ((), ()))
    q_in = (q * jnp.exp(cum)).astype(BF16)
    inter = lax.dot_general(q_in, st.astype(BF16), nt, preferred_element_type=F32)

    def fast():
        qt = (q * jnp.exp(cum - c_mid)).astype(BF16)
        kt = (k * jnp.exp(c_mid - cum)).astype(BF16)
        return lax.dot_general(qt, kt, nt, preferred_element_type=F32)

    def exact():
        lane = lax.broadcasted_iota(jnp.int32, (c, c), 1)
        row = lax.broadcasted_iota(jnp.int32, (c, 1), 0)

        def body(s_idx, att):
            sel = row == s_idx
            cum_s = jnp.sum(jnp.where(sel, cum, 0.0), axis=0, keepdims=True)
            k_s = jnp.sum(jnp.where(sel, k, 0.0), axis=0, keepdims=True)
            dec = jnp.exp(jnp.minimum(cum - cum_s, 0.0))
            col = jnp.sum(q * k_s * dec, axis=1, keepdims=True)
            return jnp.where(lane == s_idx, col, att)

        return lax.fori_loop(0, c, body, jnp.zeros((c, c), F32))

    att = jnp.where(mask, lax.cond(guard > HG_SAFE_EXP, exact, fast), 0.0)
    o = inter + jnp.dot(att.astype(BF16), v.astype(BF16), preferred_element_type=F32)
    k_end = (k * jnp.exp(c_last - cum)).astype(BF16)
    tn = (((0,), (0,)), ((), ()))
    st_new = st * jnp.exp(c_last) + lax.dot_general(v.astype(BF16), k_end, tn, preferred_element_type=F32)
    return o, st_new


def _hgrn_kernel(*refs, layer, seq_len, has_state):
    if has_state:
        (q_ref, ff_ref, fb_ref, i_ref, g_ref, lbl_ref, ng_ref, s0_ref, o_ref, of_scr, ob_scr) = refs
    else:
        (q_ref, ff_ref, fb_ref, i_ref, g_ref, lbl_ref, ng_ref, o_ref, sfin_ref, of_scr, ob_scr) = refs
    c = HG_CHUNK
    n_chunks = seq_len // c

    lg = lbl_ref[...]
    e = jnp.exp(lg - jnp.max(lg, axis=0, keepdims=True))
    sm = e / jnp.sum(e, axis=0, keepdims=True)
    lb = jnp.zeros_like(sm[0])
    for j in range(1, layer + 1):
        lb = lb + sm[j]
    lb_f, lb_b = lb[0:1], lb[1:2]

    r = lax.broadcasted_iota(jnp.int32, (c, c), 0)
    s = lax.broadcasted_iota(jnp.int32, (c, c), 1)
    causal, anti = r >= s, r <= s
    tri_f, tri_b = causal.astype(F32), anti.astype(F32)

    if has_state:
        st_f0, st_b0 = s0_ref[0].T, s0_ref[1].T
    else:
        st_f0 = st_b0 = jnp.zeros((HG_DK, HG_DK), F32)

    def load(rows, fr_ref, lbd):
        q = _silu(q_ref[rows, :])
        f = lbd + (1.0 - lbd) * jax.nn.sigmoid(fr_ref[rows, :])
        return q, 1.0 - f, i_ref[rows, :], jnp.log(f)

    def body(ci, carry):
        st_f, st_b = carry
        rows_f = pl.ds(pl.multiple_of(ci * c, c), c)
        rows_b = pl.ds(pl.multiple_of((n_chunks - 1 - ci) * c, c), c)
        q, k, v, lf = load(rows_f, ff_ref, lb_f)
        o, st_f = _hg_chunk(q, k, v, lf, st_f, tri_f, causal, (0, c // 2 - 1, c - 1))
        of_scr[rows_f, :] = o
        q, k, v, lf = load(rows_b, fb_ref, lb_b)
        o, st_b = _hg_chunk(q, k, v, lf, st_b, tri_b, anti, (c - 1, c // 2, 0))
        ob_scr[rows_b, :] = o
        return st_f, st_b

    st_f, st_b = lax.fori_loop(0, n_chunks, body, (st_f0, st_b0))
    if not has_state:
        sfin_ref[0] = st_f.T
        sfin_ref[1] = st_b.T

    o = of_scr[...] + ob_scr[...]
    o = o * lax.rsqrt(jnp.mean(o * o, axis=-1, keepdims=True) + RMS_EPS) * ng_ref[...]
    o_ref[...] = o * _silu(g_ref[...])


def hgrn_mixer(z_hg, lb_logits, norm_g, state, layer, latent):
    seq_len = DEC_SEQ if latent else SEQ
    n_seq = DEC_BATCH if latent else BATCH
    blk0 = N_CTX // seq_len if latent else 0
    h = HG_HEADS

    def col(group):
        return pl.BlockSpec((seq_len, LANES), lambda b, hh: (blk0 + b, group * h + hh))

    in_specs = [col(0), col(1), col(2), col(3), col(4),
                pl.BlockSpec((DEPTH, 2, LANES), lambda b, hh: (0, 0, hh)),
                pl.BlockSpec((None, 1, LANES), lambda b, hh: (layer, 0, hh))]
    args = [z_hg] * 5 + [lb_logits, norm_g]
    o_spec = pl.BlockSpec((seq_len, LANES), lambda b, hh: (b, hh))
    o_shape = jax.ShapeDtypeStruct((n_seq * seq_len, HG_WIDTH), F32)
    if latent:
        in_specs.append(pl.BlockSpec((None, None, 2, None, HG_DK, HG_DK), lambda b, hh: (b, layer, 0, hh, 0, 0)))
        args.append(state)
        out_specs, out_shape = [o_spec], [o_shape]
    else:
        out_specs = [o_spec, pl.BlockSpec((None, 2, None, HG_DK, HG_DK), lambda b, hh: (b, 0, hh, 0, 0))]
        out_shape = [o_shape, jax.ShapeDtypeStruct((BATCH, 2, h, HG_DK, HG_DK), F32)]
    return pl.pallas_call(
        functools.partial(_hgrn_kernel, layer=layer, seq_len=seq_len, has_state=latent),
        grid=(n_seq, h),
        in_specs=in_specs,
        out_specs=out_specs,
        out_shape=out_shape,
        scratch_shapes=[pltpu.VMEM((seq_len, LANES), F32), pltpu.VMEM((seq_len, LANES), F32)],
        compiler_params=_params(("arbitrary", "arbitrary")),
        name="hgrn_lat" if latent else "hgrn_ctx",
    )(*args)


RG_ROWS = 256
RG_PAD = SUBLANES


def _tile_scan(a, b, reverse):
    row = lax.broadcasted_iota(jnp.int32, a.shape, 0)
    for sh in (1, 2, 4):
        if reverse:
            a_s, b_s = pltpu.roll(a, SUBLANES - sh, 0), pltpu.roll(b, SUBLANES - sh, 0)
            valid = row < SUBLANES - sh
        else:
            a_s, b_s = pltpu.roll(a, sh, 0), pltpu.roll(b, sh, 0)
            valid = row >= sh
        b = jnp.where(valid, a * b_s + b, b)
        a = jnp.where(valid, a * a_s, a)
    return a, b


def _rglru_kernel(*refs, seq_len, has_state):
    if has_state:
        (x_ref, y_ref, cw_ref, cb_ref, wr_ref, br_ref, wi_ref, bi_ref, lam_ref, h0_ref,
         o_ref, xp_scr, a_scr, b_scr) = refs
    else:
        (x_ref, y_ref, cw_ref, cb_ref, wr_ref, br_ref, wi_ref, bi_ref, lam_ref,
         o_ref, hfin_ref, xp_scr, a_scr, b_scr) = refs
    t = seq_len
    zeros = jnp.zeros((RG_PAD, LANES), F32)
    xp_scr[0:RG_PAD, :] = zeros
    xp_scr[RG_PAD + t:, :] = zeros
    xp_scr[RG_PAD:RG_PAD + t, :] = x_ref[...]

    cw = cw_ref[...]
    cb = cb_ref[...]
    lam = lam_ref[...]
    sp = jnp.maximum(-lam, 0.0) + jnp.log1p(jnp.exp(-jnp.abs(lam)))

    def gates(ci, _):
        r0 = pl.multiple_of(ci * RG_ROWS, RG_ROWS)
        xc = cb
        for j in range(RG_CONV):
            xc = xc + cw[j:j + 1] * xp_scr[pl.ds(r0 + RG_PAD - RG_CONV // 2 + j, RG_ROWS), :]
        rows = pl.ds(r0, RG_ROWS)
        xcb = xc.astype(BF16)
        for d in range(2):
            r = jax.nn.sigmoid(jnp.dot(xcb, wr_ref[d], preferred_element_type=F32) + br_ref[d])
            ig = jax.nn.sigmoid(jnp.dot(xcb, wi_ref[d], preferred_element_type=F32) + bi_ref[d])
            log_a = -RG_C * r * sp[d:d + 1]
            a_scr[d, rows, :] = jnp.exp(log_a)
            b_scr[d, rows, :] = jnp.sqrt(1.0 - jnp.exp(2.0 * log_a)) * (ig * xc)
        return 0

    lax.fori_loop(0, t // RG_ROWS, gates, 0)

    n_tiles = t // SUBLANES

    def scan(j, carry):
        h_f, h_b = carry
        rows_f = pl.ds(pl.multiple_of(j * SUBLANES, SUBLANES), SUBLANES)
        rows_b = pl.ds(pl.multiple_of((n_tiles - 1 - j) * SUBLANES, SUBLANES), SUBLANES)
        aa, bb = _tile_scan(a_scr[0, rows_f, :], b_scr[0, rows_f, :], False)
        hf = aa * h_f + bb
        b_scr[0, rows_f, :] = hf
        aa, bb = _tile_scan(a_scr[1, rows_b, :], b_scr[1, rows_b, :], True)
        hb = aa * h_b + bb
        b_scr[1, rows_b, :] = hb
        return hf[SUBLANES - 1:SUBLANES], hb[0:1]

    if has_state:
        h0 = h0_ref[...]
        init = (h0[0:1], h0[1:2])
    else:
        init = (jnp.zeros((1, LANES), F32), jnp.zeros((1, LANES), F32))
    h_f, h_b = lax.fori_loop(0, n_tiles, scan, init)
    if not has_state:
        hfin_ref[0:1, :] = h_f
        hfin_ref[1:2, :] = h_b
    o_ref[...] = (b_scr[0] + b_scr[1]) * jax.nn.gelu(y_ref[...], approximate=True)


def rglru_mixer(z_rg, conv_w, conv_b, w_r, b_r, w_i, b_i, lam, state, layer, latent):
    seq_len = DEC_SEQ if latent else SEQ
    n_seq = DEC_BATCH if latent else BATCH
    blk0 = N_CTX // seq_len if latent else 0
    h = RG_HEADS
    vec = lambda rows: pl.BlockSpec((None, rows, LANES), lambda b, hh: (layer, 0, hh))
    wspec = pl.BlockSpec((None, 2, None, RG_BLOCK, RG_BLOCK), lambda b, hh: (layer, 0, hh, 0, 0))
    bspec = pl.BlockSpec((None, 2, 1, LANES), lambda b, hh: (layer, 0, 0, hh))
    in_specs = [
        pl.BlockSpec((seq_len, LANES), lambda b, hh: (blk0 + b, hh)),
        pl.BlockSpec((seq_len, LANES), lambda b, hh: (blk0 + b, h + hh)),
        vec(RG_CONV), vec(1), wspec, bspec, wspec, bspec, vec(2),
    ]
    args = [z_rg, z_rg, conv_w, conv_b, w_r, b_r.reshape(DEPTH, 2, 1, RG_WIDTH), w_i,
            b_i.reshape(DEPTH, 2, 1, RG_WIDTH), lam]
    o_spec = pl.BlockSpec((seq_len, LANES), lambda b, hh: (b, hh))
    o_shape = jax.ShapeDtypeStruct((n_seq * seq_len, RG_WIDTH), F32)
    if latent:
        in_specs.append(pl.BlockSpec((None, None, 2, LANES), lambda b, hh: (b, layer, 0, hh)))
        args.append(state)
        out_specs, out_shape = [o_spec], [o_shape]
    else:
        out_specs = [o_spec, pl.BlockSpec((None, 2, LANES), lambda b, hh: (b, 0, hh))]
        out_shape = [o_shape, jax.ShapeDtypeStruct((BATCH, 2, RG_WIDTH), F32)]
    return pl.pallas_call(
        functools.partial(_rglru_kernel, seq_len=seq_len, has_state=latent),
        grid=(n_seq, h),
        in_specs=in_specs,
        out_specs=out_specs,
        out_shape=out_shape,
        scratch_shapes=[pltpu.VMEM((seq_len + 2 * RG_PAD, LANES), F32),
                        pltpu.VMEM((2, seq_len, LANES), F32), pltpu.VMEM((2, seq_len, LANES), F32)],
        compiler_params=_params(("arbitrary", "arbitrary")),
        name="rglru_lat" if latent else "rglru_ctx",
    )(*args)


MLA_HEAD_PAD = 2 * LANES
MLA_TM = 512
ATT_TQ = 256


def rope_tables():
    half = MLA_ROPE_DIM // 2
    t = np.arange(DEC_SEQ)
    row = (t // GRID_W).astype(np.float32)
    col = (t % GRID_W).astype(np.float32)
    inv = (ROPE_THETA ** (-np.arange(0, half, 2, dtype=np.float32) / half)).astype(np.float32)
    ar, ac = row[:, None] * inv, col[:, None] * inv
    cos = np.concatenate([np.cos(ar), np.cos(ar), np.cos(ac), np.cos(ac)], -1)
    sin = np.concatenate([np.sin(ar), np.sin(ar), np.sin(ac), np.sin(ac)], -1)
    pad = np.zeros((DEC_SEQ, LANES - MLA_ROPE_DIM), np.float32)
    cos_lat = np.tile(np.concatenate([cos, pad], -1), (DEC_BATCH, 1))
    sin_lat = np.tile(np.concatenate([sin, pad], -1), (DEC_BATCH, 1))
    cos_ctx = np.concatenate([np.ones((N_CTX, MLA_ROPE_DIM), np.float32), np.zeros((N_CTX, LANES - MLA_ROPE_DIM), np.float32)], -1)
    sin_ctx = np.zeros((N_CTX, LANES), np.float32)
    return (jnp.asarray(np.concatenate([cos_ctx, cos_lat], 0).astype(np.float32)),
            jnp.asarray(np.concatenate([sin_ctx, sin_lat], 0).astype(np.float32)))


def _rope_group(x, cos, sin):
    return x * cos + pltpu.roll(x, MLA_ROPE_DIM, 1) * sin


def _rms(x, g):
    return x * lax.rsqrt(jnp.mean(x * x, axis=-1, keepdims=True) + RMS_EPS) * g


def _qproj_kernel(cq_ref, g_ref, w_ref, cos_ref, sin_ref, q_ref):
    xn = _rms(cq_ref[...], g_ref[...]).astype(BF16)
    qm = jnp.dot(xn, w_ref[...], preferred_element_type=F32)
    cos, sin = cos_ref[...], sin_ref[...]
    for h in range(MLA_HEADS):
        a = h * MLA_HEAD_PAD
        q_ref[:, a:a + LANES] = qm[:, a:a + LANES].astype(BF16)
        q_ref[:, a + LANES:a + 2 * LANES] = _rope_group(qm[:, a + LANES:a + 2 * LANES], cos, sin).astype(BF16)


def q_proj(z_cq, q_norm_g, w_uq_l, cos_t, sin_t, layer):
    tm = MLA_TM
    width = MLA_HEADS * MLA_HEAD_PAD
    return pl.pallas_call(
        _qproj_kernel,
        grid=(N_TOK // tm,),
        in_specs=[
            pl.BlockSpec((tm, Q_LORA), lambda i: (i, 0)),
            pl.BlockSpec((None, 1, Q_LORA), lambda i: (layer, 0, 0)),
            pl.BlockSpec((Q_LORA, width), lambda i: (0, 0)),
            pl.BlockSpec((tm, LANES), lambda i: (i, 0)),
            pl.BlockSpec((tm, LANES), lambda i: (i, 0)),
        ],
        out_specs=pl.BlockSpec((tm, width), lambda i: (i, 0)),
        out_shape=jax.ShapeDtypeStruct((N_TOK, width), BF16),
        compiler_params=_params(("arbitrary",)),
        name="q_proj",
    )(z_cq, q_norm_g, w_uq_l, cos_t, sin_t)


def prep_w_uq(w_uq_l):
    w = w_uq_l.reshape(Q_LORA, MLA_HEADS, MLA_QK_DIM)
    pe = w[..., MLA_NOPE_DIM:]
    w = jnp.concatenate([w, _rot_cols(pe)], axis=-1)
    return w.reshape(Q_LORA, MLA_HEADS * MLA_HEAD_PAD).astype(BF16)


def _kvproj_kernel(zkv_ref, g_ref, wk_ref, wv_ref, cos_ref, sin_ref, *outs, normalize):
    if normalize:
        k_ref, v_ref, ckv_ref, kpe_ref = outs
    else:
        k_ref, v_ref = outs
    ckv = zkv_ref[:, 0:KV_LORA]
    if normalize:
        ckv = _rms(ckv, g_ref[...])
        ckv_ref[...] = ckv
    pe_group = zkv_ref[:, KV_LORA:KV_LORA + LANES]
    if normalize:
        kpe_ref[...] = pe_group[:, 0:MLA_ROPE_DIM]
    pe = _rope_group(pe_group, cos_ref[...], sin_ref[...]).astype(BF16)
    cb = ckv.astype(BF16)
    kn = jnp.dot(cb, wk_ref[...], preferred_element_type=F32)
    v_ref[...] = jnp.dot(cb, wv_ref[...], preferred_element_type=F32).astype(BF16)
    for h in range(MLA_HEADS):
        a = h * MLA_HEAD_PAD
        k_ref[:, a:a + LANES] = kn[:, h * LANES:(h + 1) * LANES].astype(BF16)
        k_ref[:, a + LANES:a + 2 * LANES] = pe


def kv_proj(z_kv, kv_norm_g, w_uk_l, w_uv_l, cos_t, sin_t, layer):
    tm = MLA_TM
    kw, vw = MLA_HEADS * MLA_HEAD_PAD, MLA_HEADS * MLA_V_DIM
    return pl.pallas_call(
        functools.partial(_kvproj_kernel, normalize=True),
        grid=(N_TOK // tm,),
        in_specs=[
            pl.BlockSpec((tm, W_KV), lambda i: (i, 0)),
            pl.BlockSpec((None, 1, KV_LORA), lambda i: (layer, 0, 0)),
            pl.BlockSpec((KV_LORA, MLA_HEADS * MLA_NOPE_DIM), lambda i: (0, 0)),
            pl.BlockSpec((KV_LORA, vw), lambda i: (0, 0)),
            pl.BlockSpec((tm, LANES), lambda i: (i, 0)),
            pl.BlockSpec((tm, LANES), lambda i: (i, 0)),
        ],
        out_specs=[
            pl.BlockSpec((tm, kw), lambda i: (i, 0)),
            pl.BlockSpec((tm, vw), lambda i: (i, 0)),
            pl.BlockSpec((tm, KV_LORA), lambda i: (i, 0)),
            pl.BlockSpec((tm, MLA_ROPE_DIM), lambda i: (i, 0)),
        ],
        out_shape=[
            jax.ShapeDtypeStruct((N_TOK, kw), BF16),
            jax.ShapeDtypeStruct((N_TOK, vw), BF16),
            jax.ShapeDtypeStruct((N_TOK, KV_LORA), F32),
            jax.ShapeDtypeStruct((N_TOK, MLA_ROPE_DIM), F32),
        ],
        compiler_params=_params(("arbitrary",)),
        name="kv_proj",
    )(z_kv, kv_norm_g, w_uk_l, w_uv_l, cos_t, sin_t)


def kv_proj_cache(cache_kv, kv_norm_g, w_uk_l, w_uv_l, cos_c, sin_c, layer):
    tm = PAST_LEN
    kw, vw = MLA_HEADS * MLA_HEAD_PAD, MLA_HEADS * MLA_V_DIM
    n = DEC_BATCH * PAST_LEN
    return pl.pallas_call(
        functools.partial(_kvproj_kernel, normalize=False),
        grid=(DEC_BATCH,),
        in_specs=[
            pl.BlockSpec((tm, W_KV), lambda b: (b, 0)),
            pl.BlockSpec((None, 1, KV_LORA), lambda b: (layer, 0, 0)),
            pl.BlockSpec((KV_LORA, MLA_HEADS * MLA_NOPE_DIM), lambda b: (0, 0)),
            pl.BlockSpec((KV_LORA, vw), lambda b: (0, 0)),
            pl.BlockSpec((tm, LANES), lambda b: (0, 0)),
            pl.BlockSpec((tm, LANES), lambda b: (0, 0)),
        ],
        out_specs=[pl.BlockSpec((tm, kw), lambda b: (b, 0)), pl.BlockSpec((tm, vw), lambda b: (b, 0))],
        out_shape=[jax.ShapeDtypeStruct((n, kw), BF16), jax.ShapeDtypeStruct((n, vw), BF16)],
        compiler_params=_params(("arbitrary",)),
        name="kv_proj_cache",
    )(cache_kv, kv_norm_g, w_uk_l, w_uv_l, cos_c, sin_c)


def _attn_kernel(q_ref, *refs):
    o_ref = refs[-1]
    segs = [(refs[i], refs[i + 1]) for i in range(0, len(refs) - 1, 2)]
    nt = (((1,), (1,)), ((), ()))
    q = q_ref[...]
    scores = [lax.dot_general(q, k_ref[...], nt, preferred_element_type=F32) for k_ref, _ in segs]
    m = scores[0].max(axis=-1, keepdims=True)
    for s in scores[1:]:
        m = jnp.maximum(m, s.max(axis=-1, keepdims=True))
    l, o = 0.0, 0.0
    for s, (_, v_ref) in zip(scores, segs):
        p = jnp.exp((s - m) * (MLA_QK_DIM ** -0.5))
        l = l + jnp.sum(p, axis=-1, keepdims=True)
        o = o + jnp.dot(p.astype(BF16), v_ref[...], preferred_element_type=F32)
    o_ref[...] = o / l


def mla_attention(q, k_tok, v_tok, k_cache, v_cache, latent):
    h = MLA_HEADS
    if latent:
        tq = ATT_TQ
        n_q = DEC_SEQ // tq
        grid = (DEC_BATCH, h, n_q)
        q_map = lambda b, hh, i: (N_CTX // tq + b * n_q + i, hh)
        kv_map = lambda b, hh, i: (N_CTX // DEC_SEQ + b, hh)
        o_map = lambda b, hh, i: (b * n_q + i, hh)
        in_specs = [
            pl.BlockSpec((tq, MLA_HEAD_PAD), q_map),
            pl.BlockSpec((DEC_SEQ, MLA_HEAD_PAD), kv_map),
            pl.BlockSpec((DEC_SEQ, MLA_V_DIM), kv_map),
            pl.BlockSpec((PAST_LEN, MLA_HEAD_PAD), lambda b, hh, i: (b, hh)),
            pl.BlockSpec((PAST_LEN, MLA_V_DIM), lambda b, hh, i: (b, hh)),
        ]
        args = [q, k_tok, v_tok, k_cache, v_cache]
        n_out = N_LAT
    else:
        tq = SEQ
        grid = (BATCH, h, 1)
        q_map = o_map = lambda b, hh, i: (b, hh)
        in_specs = [
            pl.BlockSpec((tq, MLA_HEAD_PAD), q_map),
            pl.BlockSpec((SEQ, MLA_HEAD_PAD), q_map),
            pl.BlockSpec((SEQ, MLA_V_DIM), q_map),
        ]
        args = [q, k_tok, v_tok]
        n_out = N_CTX
    return pl.pallas_call(
        _attn_kernel,
        grid=grid,
        in_specs=in_specs,
        out_specs=pl.BlockSpec((tq, MLA_V_DIM), o_map),
        out_shape=jax.ShapeDtypeStruct((n_out, MLA_WIDTH), F32),
        compiler_params=_params(("arbitrary",) * 3),
        name="mla_attn_lat" if latent else "mla_attn_ctx",
    )(*args)


OUT_TM = 256


def _layer_norm(y, g, b):
    mu = jnp.mean(y, axis=-1, keepdims=True)
    yc = y - mu
    var = jnp.mean(yc * yc, axis=-1, keepdims=True)
    return yc * lax.rsqrt(var + LN_EPS) * g + b


def _outproj_kernel(ohg_ref, org_ref, omla_ref, w_ref, x_ref, mod_ref, g_ref, b_ref, wr_ref,
                    x1_ref, hf_ref, pt_ref):
    m = jnp.dot(ohg_ref[...].astype(BF16), w_ref[0:HG_WIDTH, :], preferred_element_type=F32)
    m += jnp.dot(org_ref[...].astype(BF16), w_ref[HG_WIDTH:HG_WIDTH + RG_WIDTH, :], preferred_element_type=F32)
    m += jnp.dot(omla_ref[...].astype(BF16), w_ref[HG_WIDTH + RG_WIDTH:, :], preferred_element_type=F32)
    md = mod_ref[...]
    x1 = _layer_norm(DN_ALPHA * x_ref[...] + md[2:3] * m, g_ref[...], b_ref[...])
    x1_ref[...] = x1
    hf = x1 * (1.0 + md[4:5]) + md[3:4]
    hf_ref[...] = hf
    logits = jnp.dot(hf.astype(BF16), wr_ref[...], preferred_element_type=F32)
    lane = lax.broadcasted_iota(jnp.int32, logits.shape, 1)
    logits = jnp.where(lane < N_EXPERTS, logits, -jnp.inf)
    e = jnp.exp(logits - jnp.max(logits, axis=-1, keepdims=True))
    p = e / jnp.sum(e, axis=-1, keepdims=True)
    pt_ref[...] = p.T[0:N_EXPERTS, :]


def out_proj(o_hg, o_rg, o_mla, w_out_l, x, mod6, ln_g, ln_b, w_router_l, layer):
    tm = OUT_TM
    row = lambda w: pl.BlockSpec((tm, w), lambda i: (i, 0))
    const = lambda shape: pl.BlockSpec(shape, lambda i: (0,) * len(shape), pipeline_mode=pl.Buffered(1))
    return pl.pallas_call(
        _outproj_kernel,
        grid=(N_TOK // tm,),
        in_specs=[
            row(HG_WIDTH), row(RG_WIDTH), row(MLA_WIDTH),
            const((D_MODEL, D_MODEL)),
            row(D_MODEL),
            pl.BlockSpec((None, 6, D_MODEL), lambda i: (layer * N_MODROWS + _mod_row(i, tm), 0, 0)),
            pl.BlockSpec((None, 1, D_MODEL), lambda i: (layer, 0, 0)),
            pl.BlockSpec((None, 1, D_MODEL), lambda i: (layer, 0, 0)),
            const((D_MODEL, LANES)),
        ],
        out_specs=[row(D_MODEL), row(D_MODEL), pl.BlockSpec((N_EXPERTS, tm), lambda i: (0, i))],
        out_shape=[jax.ShapeDtypeStruct((N_TOK, D_MODEL), F32),
                   jax.ShapeDtypeStruct((N_TOK, D_MODEL), F32),
                   jax.ShapeDtypeStruct((N_EXPERTS, N_TOK), F32)],
        compiler_params=_params(("arbitrary",)),
        name="out_proj",
    )(o_hg, o_rg, o_mla, w_out_l, x, mod6, ln_g, ln_b, w_router_l)


CAP_CTX = CAP_FACTOR * N_CTX // N_EXPERTS
CAP_LAT = CAP_FACTOR * N_LAT // N_EXPERTS
ROUTE_SETS = ((0, N_CTX, CAP_CTX), (N_CTX, N_LAT, CAP_LAT))
SLOTS = CAP_CTX + CAP_LAT


def _route_thr_kernel(pt_ref, thr_ref, need_ref):
    for si, (start, n, cap) in enumerate(ROUTE_SETS):
        p = pt_ref[:, start:start + n]

        def count(mask):
            return jnp.sum(mask.astype(F32), axis=1, keepdims=True)

        def body(i, t):
            cand = t | jnp.left_shift(jnp.int32(1), 30 - i)
            return jnp.where(count(p >= pltpu.bitcast(cand, F32)) >= cap, cand, t)

        t = pltpu.bitcast(lax.fori_loop(0, 31, body, jnp.zeros((N_EXPERTS, 1), jnp.int32)), F32)
        need = cap - count(p > t)
        thr_ref[si] = jnp.broadcast_to(t, (N_EXPERTS, LANES))
        need_ref[si] = jnp.broadcast_to(need, (N_EXPERTS, LANES))


def route_threshold(p_t):
    n_sets = len(ROUTE_SETS)
    return pl.pallas_call(
        _route_thr_kernel,
        out_shape=[jax.ShapeDtypeStruct((n_sets, N_EXPERTS, LANES), F32),
                   jax.ShapeDtypeStruct((n_sets, N_EXPERTS, LANES), F32)],
        compiler_params=_params(None),
        name="route_threshold",
    )(p_t)


def _route_lists_kernel(p_ref, thr_ref, need_ref, lists_ref, cnt_ref, first_ref, sel_ref, *, nb, cap, tok0, row0):
    nt = (((1,), (1,)), ((), ()))
    r128 = lax.broadcasted_iota(jnp.int32, (LANES, LANES), 0)
    c128 = lax.broadcasted_iota(jnp.int32, (LANES, LANES), 1)
    incl = (r128 <= c128).astype(BF16)
    eye = r128 == c128
    rb = lax.broadcasted_iota(jnp.int32, (nb, nb), 0)
    cb = lax.broadcasted_iota(jnp.int32, (nb, nb), 1)
    below = (cb < rb).astype(BF16)
    incl_b = (rb <= cb).astype(BF16)
    ones8 = jnp.ones((SUBLANES, LANES), BF16)
    s_col = lax.broadcasted_iota(jnp.int32, (cap, 1), 0).astype(F32)
    lane = lax.broadcasted_iota(jnp.int32, (cap, LANES), 1)
    lane_f = lane.astype(F32)
    j_row = lax.broadcasted_iota(jnp.int32, (1, nb), 1).astype(F32)

    def dot(a, b):
        return jnp.dot(a, b, preferred_element_type=F32)

    def block_base(totals, unit):
        hi = jnp.floor(totals * (1.0 / unit))
        lo = totals - unit * hi
        bc = lambda a: jnp.broadcast_to(a, (nb, LANES)).astype(BF16)
        return unit * dot(below, bc(hi)) + dot(below, bc(lo))

    def choose(e, acc):
        p = p_ref[e]
        t = thr_ref[pl.ds(e, 1), :]
        need = need_ref[pl.ds(e, 1), :]
        eq = p == t
        eq_f = eq.astype(F32)
        eq_lp = dot(eq_f.astype(BF16), incl)
        tie_rank = eq_lp - eq_f + block_base(eq_lp[:, LANES - 1:LANES], 16.0)
        sel_f = jnp.where((p > t) | (eq & (tie_rank < need)), 1.0, 0.0)
        sel_ref[e] = sel_f
        return acc + sel_f

    cnt = lax.fori_loop(0, N_EXPERTS, choose, jnp.zeros((nb, LANES), F32))
    cnt_lp = dot(cnt.astype(BF16), incl)
    first = cnt_lp - cnt + block_base(cnt_lp[:, LANES - 1:LANES], 64.0) + row0
    first_hi = jnp.floor(first * (1.0 / LANES))
    first_lo = first - LANES * first_hi

    def expert(e, acc):
        p = p_ref[e]
        sel_f = sel_ref[e]
        sel_b = sel_f.astype(BF16)
        lp = dot(sel_b, incl)
        c_row = lax.dot_general(ones8, sel_b, nt, preferred_element_type=F32)
        incl_row = dot(c_row.astype(BF16), incl_b)[0:1]
        excl_row = incl_row - c_row[0:1]
        oh_j = jnp.where((s_col >= excl_row) & (s_col < incl_row), 1.0, 0.0)
        base_s = jnp.sum(oh_j * excl_row, axis=1, keepdims=True)
        j_s = jnp.sum(oh_j * j_row, axis=1, keepdims=True)
        oh_jb = oh_j.astype(BF16)
        lp_rows = dot(oh_jb, lp.astype(BF16))
        pos = jnp.sum(jnp.where(lp_rows <= s_col - base_s, 1.0, 0.0), axis=1, keepdims=True)
        oh_c = lane_f == pos
        p1 = p.astype(BF16)
        r1 = p - p1.astype(F32)
        p2 = r1.astype(BF16)
        p3 = (r1 - p2.astype(F32)).astype(BF16)
        p_rows = dot(oh_jb, p1) + dot(oh_jb, p2) + dot(oh_jb, p3)
        gate = jnp.sum(jnp.where(oh_c, p_rows, 0.0), axis=1, keepdims=True)
        rank_rows = dot(oh_jb, acc.astype(BF16))
        first_rows = LANES * dot(oh_jb, first_hi.astype(BF16)) + dot(oh_jb, first_lo.astype(BF16))
        dst = jnp.sum(jnp.where(oh_c, rank_rows + first_rows, 0.0), axis=1, keepdims=True)
        idx = tok0 + LANES * j_s + pos
        lists_ref[e] = jnp.where(lane == 0, idx, jnp.where(lane == 1, dst, jnp.where(lane == 2, gate, 0.0)))
        return acc + sel_f

    lax.fori_loop(0, N_EXPERTS, expert, jnp.zeros((nb, LANES), F32))

    ones_b = jnp.ones((LANES, LANES), BF16)

    def column(a, j):
        diag = jnp.where(eye, jnp.broadcast_to(a[j:j + 1, :], (LANES, LANES)), 0.0)
        return dot(diag.astype(BF16), ones_b)

    for j in range(nb):
        rows = slice(j * LANES, (j + 1) * LANES)
        cnt_ref[rows, :] = column(cnt, j)
        first_ref[rows, :] = LANES * column(first_hi, j) + column(first_lo, j)


def route_lists(p_blk, thr, need, set_index):
    tok0, n, cap = ROUTE_SETS[set_index]
    nb = n // LANES
    row0 = float(CAP_FACTOR * tok0)
    kern = functools.partial(_route_lists_kernel, nb=nb, cap=cap, tok0=tok0, row0=row0)
    return pl.pallas_call(
        kern,
        grid=(1,),
        in_specs=[
            pl.BlockSpec((N_EXPERTS, nb, LANES), lambda i: (0, 0, 0)),
            pl.BlockSpec((None, N_EXPERTS, LANES), lambda i: (set_index, 0, 0)),
            pl.BlockSpec((None, N_EXPERTS, LANES), lambda i: (set_index, 0, 0)),
        ],
        out_specs=[pl.BlockSpec((N_EXPERTS, cap, LANES), lambda i: (0, 0, 0)),
                   pl.BlockSpec((n, LANES), lambda i: (0, 0)),
                   pl.BlockSpec((n, LANES), lambda i: (0, 0))],
        out_shape=[jax.ShapeDtypeStruct((N_EXPERTS, cap, LANES), F32),
                   jax.ShapeDtypeStruct((n, LANES), F32),
                   jax.ShapeDtypeStruct((n, LANES), F32)],
        scratch_shapes=[pltpu.VMEM((N_EXPERTS, nb, LANES), F32)],
        compiler_params=_params(("arbitrary",)),
        name="route_lists_lat" if set_index else "route_lists_ctx",
    )(p_blk, thr, need)


FFN_TM = 512


def _ffn_kernel(tok_ref, dst_ref, hf_ref, lists_ref, wg_ref, wu_ref, wd_ref, yc_ref, x_buf, y_buf, sem_in, sem_out):
    tm = FFN_TM
    step = pl.program_id(0) * pl.num_programs(1) + pl.program_id(1)
    n_steps = pl.num_programs(0) * pl.num_programs(1)

    def gather(i, _):
        pltpu.make_async_copy(hf_ref.at[pl.ds(tok_ref[0, 0, i], 1), :], x_buf.at[pl.ds(i, 1), :], sem_in).start()
        return 0

    def scatter(i, _):
        pltpu.make_async_copy(y_buf.at[pl.ds(i, 1), :], yc_ref.at[pl.ds(dst_ref[0, 0, i], 1), :], sem_out).start()
        return 0

    def wait_scatter():
        pltpu.make_async_copy(y_buf, yc_ref.at[pl.ds(0, tm), :], sem_out).wait()

    lax.fori_loop(0, tm, gather, 0)

    @pl.when(step > 0)
    def _():
        wait_scatter()

    pltpu.make_async_copy(hf_ref.at[pl.ds(0, tm), :], x_buf, sem_in).wait()
    x = x_buf[...].astype(BF16)
    gate = jnp.dot(x, wg_ref[...], preferred_element_type=F32)
    up = jnp.dot(x, wu_ref[...], preferred_element_type=F32)
    hid = (_silu(gate) * up).astype(BF16)
    y_buf[...] = jnp.dot(hid, wd_ref[...], preferred_element_type=F32) * lists_ref[:, 2:3]
    lax.fori_loop(0, tm, scatter, 0)

    @pl.when(step == n_steps - 1)
    def _():
        wait_scatter()


def expert_ffn(tok, dst, hf, lists, w_gate_l, w_up_l, w_down_l):
    tm = FFN_TM
    per_e = SLOTS // tm
    idx_spec = pl.BlockSpec((1, 1, tm), lambda e, i: (e * per_e + i, 0, 0), memory_space=pltpu.SMEM)
    as_tiles = lambda a: a.reshape(N_EXPERTS * per_e, 1, tm)
    return pl.pallas_call(
        _ffn_kernel,
        grid=(N_EXPERTS, per_e),
        in_specs=[
            idx_spec, idx_spec,
            pl.BlockSpec(memory_space=pl.ANY),
            pl.BlockSpec((None, tm, LANES), lambda e, i: (e, i, 0)),
            pl.BlockSpec((None, D_MODEL, D_FF_EXPERT), lambda e, i: (e, 0, 0)),
            pl.BlockSpec((None, D_MODEL, D_FF_EXPERT), lambda e, i: (e, 0, 0)),
            pl.BlockSpec((None, D_FF_EXPERT, D_MODEL), lambda e, i: (e, 0, 0)),
        ],
        out_specs=pl.BlockSpec(memory_space=pl.ANY),
        out_shape=jax.ShapeDtypeStruct((N_CHOICES, D_MODEL), F32),
        scratch_shapes=[pltpu.VMEM((tm, D_MODEL), F32), pltpu.VMEM((tm, D_MODEL), F32),
                        pltpu.SemaphoreType.DMA(()), pltpu.SemaphoreType.DMA(())],
        compiler_params=_params(("arbitrary", "arbitrary")),
        name="expert_ffn",
    )(as_tiles(tok), as_tiles(dst), hf, lists, w_gate_l, w_up_l, w_down_l)


COMB_TM = 512
COMB_ROWS = 256
N_CHOICES = N_EXPERTS * SLOTS


def _combine_kernel(rows_ref, y_ref, cnt_ref, first_ref, x1_ref, mod_ref, g_ref, b_ref, o_ref, buf, sem, acc_ref):
    i = pl.program_id(0)
    lc = COMB_ROWS
    r0, r1 = rows_ref[i], rows_ref[i + 1]
    base = lax.div(r0, lc) * lc
    n_chunks = lax.div(r1 - base + lc - 1, lc)

    def chunk_start(k):
        return pl.multiple_of(jnp.minimum(base + k * lc, N_CHOICES - lc), lc)

    def copy(k, slot):
        return pltpu.make_async_copy(y_ref.at[pl.ds(chunk_start(k), lc), :], buf.at[slot], sem.at[slot])

    acc_ref[...] = jnp.zeros_like(acc_ref)

    @pl.when(n_chunks > 0)
    def _():
        copy(0, 0).start()

    first = first_ref[:, 0:1]
    last = first + cnt_ref[:, 0:1]
    col = lax.broadcasted_iota(jnp.int32, (1, lc), 1)

    def body(k, _):
        slot = lax.rem(k, 2)
        copy(k, slot).wait()

        @pl.when(k + 1 < n_chunks)
        def _():
            copy(k + 1, 1 - slot).start()

        row_id = (chunk_start(k) + col).astype(F32)
        mine = (row_id >= first) & (row_id < last) & (row_id >= (base + k * lc).astype(F32))
        onehot = jnp.where(mine, 1.0, 0.0).astype(BF16)
        y = buf[slot]
        hi = y.astype(BF16)
        lo = (y - hi.astype(F32)).astype(BF16)
        acc_ref[...] += (jnp.dot(onehot, hi, preferred_element_type=F32)
                         + jnp.dot(onehot, lo, preferred_element_type=F32))
        return 0

    lax.fori_loop(0, n_chunks, body, 0)
    g2 = mod_ref[...][5:6]
    o_ref[...] = _layer_norm(DN_ALPHA * x1_ref[...] + g2 * acc_ref[...], g_ref[...], b_ref[...])


def moe_combine(tile_rows, y_choices, cnt, first, x1, mod6, ln_g, ln_b, layer):
    tm = COMB_TM
    row = lambda w: pl.BlockSpec((tm, w), lambda i, t: (i, 0))
    grid_spec = pltpu.PrefetchScalarGridSpec(
        num_scalar_prefetch=1,
        grid=(N_TOK // tm,),
        in_specs=[
            pl.BlockSpec(memory_space=pl.ANY),
            row(LANES), row(LANES), row(D_MODEL),
            pl.BlockSpec((None, 6, D_MODEL), lambda i, t: (layer * N_MODROWS + _mod_row(i, tm), 0, 0)),
            pl.BlockSpec((None, 1, D_MODEL), lambda i, t: (layer, 0, 0)),
            pl.BlockSpec((None, 1, D_MODEL), lambda i, t: (layer, 0, 0)),
        ],
        out_specs=row(D_MODEL),
        scratch_shapes=[pltpu.VMEM((2, COMB_ROWS, D_MODEL), F32), pltpu.SemaphoreType.DMA((2,)),
                        pltpu.VMEM((tm, D_MODEL), F32)],
    )
    return pl.pallas_call(
        _combine_kernel,
        grid_spec=grid_spec,
        out_shape=jax.ShapeDtypeStruct((N_TOK, D_MODEL), F32),
        compiler_params=_params(("arbitrary",)),
        name="moe_combine",
    )(tile_rows, y_choices, cnt, first, x1, mod6, ln_g, ln_b)


def moe_block(x1, hf, p_t, mod6, ln_g, ln_b, w_gate_l, w_up_l, w_down_l, layer):
    thr, need = route_threshold(p_t)
    parts = []
    for si, (tok0, n, cap) in enumerate(ROUTE_SETS):
        p_blk = p_t[:, tok0:tok0 + n].reshape(N_EXPERTS, n // LANES, LANES)
        parts.append(route_lists(p_blk, thr, need, si))
    lists = jnp.concatenate([p[0] for p in parts], axis=1)
    cnt = jnp.concatenate([p[1] for p in parts], axis=0)
    first = jnp.concatenate([p[2] for p in parts], axis=0)
    tok = lists[:, :, 0].astype(jnp.int32)
    dst = lists[:, :, 1].astype(jnp.int32)
    y_choices = expert_ffn(tok, dst, hf, lists, w_gate_l, w_up_l, w_down_l)
    tile_rows = jnp.concatenate([first[::COMB_TM, 0], jnp.full((1,), N_CHOICES, F32)]).astype(jnp.int32)
    return moe_combine(tile_rows, y_choices, cnt, first, x1, mod6, ln_g, ln_b, layer)


def kernel(x_prompt, x_sample, cache_mla_ckv, cache_mla_kpe, state_hgrn, state_rglru, c, c_ctx,
           w_in, w_out, hg_lb_logits, hg_norm_g, rg_conv_w, rg_conv_b, rg_w_r, rg_b_r, rg_w_i, rg_b_i,
           rg_lambda, mla_q_norm_g, mla_kv_norm_g, mla_w_uq, mla_w_uk, mla_w_uv, ada_w, ada_b,
           ln1_g, ln1_b, ln2_g, ln2_b, moe_router, moe_w_gate, moe_w_up, moe_w_down):
    x = jnp.concatenate([x_prompt.reshape(N_CTX, D_MODEL), x_sample.reshape(N_LAT, D_MODEL)], axis=0)
    cvec = jnp.concatenate([c_ctx[None, :], c, jnp.zeros((SUBLANES - N_MODROWS, D_MODEL), F32)], axis=0)
    mod = ada_mod(cvec, ada_w, ada_b)
    mod6 = mod[:, :N_MODROWS].reshape(DEPTH * N_MODROWS, 6, D_MODEL)
    cos_t, sin_t = rope_tables()
    vec = lambda a: a.reshape(DEPTH, 1, a.shape[-1])
    hg_ng, cb, qg, kg = vec(hg_norm_g), vec(rg_conv_b), vec(mla_q_norm_g), vec(mla_kv_norm_g)
    g1, b1, g2, b2 = vec(ln1_g), vec(ln1_b), vec(ln2_g), vec(ln2_b)
    w_r, w_i = rg_w_r.astype(BF16), rg_w_i.astype(BF16)
    router = jnp.pad(moe_router, ((0, 0), (0, 0), (0, LANES - N_EXPERTS))).astype(BF16)

    ckvs, kpes, hgs, rgs = [], [], [], []
    for l in range(DEPTH):
        z_hg, z_rg, z_cq, z_kv = in_proj(x, mod6, prep_w_in(w_in[l]), l)

        o_hg_c, hg_fin = hgrn_mixer(z_hg, hg_lb_logits, hg_ng, None, l, False)
        (o_hg_l,) = hgrn_mixer(z_hg, hg_lb_logits, hg_ng, state_hgrn, l, True)
        o_hg = jnp.concatenate([o_hg_c, o_hg_l], axis=0)

        rg_args = (rg_conv_w, cb, w_r, rg_b_r, w_i, rg_b_i, rg_lambda)
        o_rg_c, rg_fin = rglru_mixer(z_rg, *rg_args, None, l, False)
        (o_rg_l,) = rglru_mixer(z_rg, *rg_args, state_rglru, l, True)
        o_rg = jnp.concatenate([o_rg_c, o_rg_l], axis=0)

        w_uk, w_uv = mla_w_uk[l].astype(BF16), mla_w_uv[l].astype(BF16)
        q = q_proj(z_cq, qg, prep_w_uq(mla_w_uq[l]), cos_t, sin_t, l)
        k_tok, v_tok, ckv_n, kpe = kv_proj(z_kv, kg, w_uk, w_uv, cos_t, sin_t, l)
        cache = jnp.concatenate([cache_mla_ckv[:, l].reshape(DEC_BATCH * PAST_LEN, KV_LORA),
                                 cache_mla_kpe[:, l].reshape(DEC_BATCH * PAST_LEN, MLA_ROPE_DIM),
                                 jnp.zeros((DEC_BATCH * PAST_LEN, MLA_ROPE_DIM), F32)], axis=1)
        k_cache, v_cache = kv_proj_cache(cache, kg, w_uk, w_uv, cos_t[:PAST_LEN], sin_t[:PAST_LEN], l)
        o_mla = jnp.concatenate([mla_attention(q, k_tok, v_tok, None, None, False),
                                 mla_attention(q, k_tok, v_tok, k_cache, v_cache, True)], axis=0)

        x1, hf, p_t = out_proj(o_hg, o_rg, o_mla, w_out[l].astype(BF16), x, mod6, g1, b1, router[l], l)
        x = moe_block(x1, hf, p_t, mod6, g2, b2, moe_w_gate[l].astype(BF16), moe_w_up[l].astype(BF16),
                      moe_w_down[l].astype(BF16), l)

        ckvs.append(ckv_n[:N_CTX].reshape(BATCH, SEQ, KV_LORA))
        kpes.append(kpe[:N_CTX].reshape(BATCH, SEQ, MLA_ROPE_DIM))
        hgs.append(hg_fin)
        rgs.append(rg_fin)

    y_prompt = x[:N_CTX].reshape(BATCH, SEQ, D_MODEL)
    y_sample = x[N_CTX:].reshape(DEC_BATCH, DEC_SEQ, D_MODEL)
    return (y_prompt, y_sample, jnp.stack(ckvs, axis=1), jnp.stack(kpes, axis=1),
            jnp.stack(hgs, axis=1), jnp.stack(rgs, axis=1))
```

```python
import functools
import math

import jax
import jax.numpy as jnp
import numpy as np
from jax import lax
from jax.experimental import pallas as pl
from jax.experimental.pallas import tpu as pltpu

F32 = jnp.float32
BF16 = jnp.bfloat16

D_MODEL = 2048
BATCH = 16
SEQ = 256
DEPTH = 2
DEC_BATCH = 2
DEC_SEQ = 4096
PAST_LEN = 512
GRID_W = 64
HG_WIDTH = 512
HG_HEADS = 4
HG_DK = 128
RG_WIDTH = 512
RG_HEADS = 4
RG_BLOCK = 128
RG_CONV = 4
RG_C = 8.0
MLA_WIDTH = 1024
MLA_HEADS = 8
MLA_V_DIM = 128
MLA_NOPE_DIM = 128
MLA_ROPE_DIM = 64
MLA_QK_DIM = 192
Q_LORA = 512
KV_LORA = 512
ROPE_THETA = 10000.0
Q_BLOCK = 128
N_EXPERTS = 16
CAP_FACTOR = 2
D_FF_EXPERT = 1024
DN_ALPHA = (2.0 * DEPTH) ** 0.25
LN_EPS = 1e-5
RMS_EPS = 1e-6

N_CTX = BATCH * SEQ
N_LAT = DEC_BATCH * DEC_SEQ
N_TOK = N_CTX + N_LAT
N_MODROWS = 1 + DEC_BATCH

LANES = 128
SUBLANES = 8
VMEM_LIMIT_BYTES = 56 * 1024 * 1024

W_HG = 5 * HG_WIDTH
W_RG = 2 * RG_WIDTH
W_KV = KV_LORA + 2 * MLA_ROPE_DIM
IN_GROUPS = (W_HG, W_RG, Q_LORA, W_KV)
IN_COLS_OWN = sum(IN_GROUPS)


def _params(sem, vmem=VMEM_LIMIT_BYTES):
    return pltpu.CompilerParams(dimension_semantics=sem, vmem_limit_bytes=vmem)


def _mod_row(i, tm):
    n_ctx_tiles = N_CTX // tm
    per_batch = DEC_SEQ // tm
    return jnp.where(i < n_ctx_tiles, 0, 1 + (i - n_ctx_tiles) // per_batch)


ADA_TN = 1024


def _ada_kernel(c_ref, w_ref, b_ref, o_ref):
    c = c_ref[...]
    s = (c * jax.nn.sigmoid(c)).astype(BF16)
    o_ref[...] = jnp.dot(s, w_ref[...].astype(BF16), preferred_element_type=F32) + b_ref[...]


def ada_mod(cvec, ada_w, ada_b):
    ncol = 6 * D_MODEL
    return pl.pallas_call(
        _ada_kernel,
        grid=(DEPTH, ncol // ADA_TN),
        in_specs=[
            pl.BlockSpec((SUBLANES, D_MODEL), lambda l, j: (0, 0)),
            pl.BlockSpec((None, D_MODEL, ADA_TN), lambda l, j: (l, 0, j)),
            pl.BlockSpec((None, 1, ADA_TN), lambda l, j: (l, 0, j)),
        ],
        out_specs=pl.BlockSpec((None, SUBLANES, ADA_TN), lambda l, j: (l, 0, j)),
        out_shape=jax.ShapeDtypeStruct((DEPTH, SUBLANES, ncol), F32),
        compiler_params=_params(("arbitrary", "arbitrary")),
        name="ada_mod",
    )(cvec, ada_w, ada_b.reshape(DEPTH, 1, ncol))


INPROJ_TM = 256


def _inproj_kernel(x_ref, mod_ref, w_ref, ohg_ref, org_ref, ocq_ref, okv_ref):
    m = mod_ref[...]
    hm = (x_ref[...] * (1.0 + m[1:2]) + m[0:1]).astype(BF16)
    a = 0
    for o_ref, width in zip((ohg_ref, org_ref, ocq_ref, okv_ref), IN_GROUPS):
        o_ref[...] = jnp.dot(hm, w_ref[:, a:a + width], preferred_element_type=F32)
        a += width


def in_proj(x, mod6, w_in_l, layer):
    tm = INPROJ_TM
    return pl.pallas_call(
        _inproj_kernel,
        grid=(N_TOK // tm,),
        in_specs=[
            pl.BlockSpec((tm, D_MODEL), lambda i: (i, 0)),
            pl.BlockSpec((None, 6, D_MODEL), lambda i: (layer * N_MODROWS + _mod_row(i, tm), 0, 0)),
            pl.BlockSpec((D_MODEL, IN_COLS_OWN), lambda i: (0, 0), pipeline_mode=pl.Buffered(1)),
        ],
        out_specs=[pl.BlockSpec((tm, w), lambda i: (i, 0)) for w in IN_GROUPS],
        out_shape=[jax.ShapeDtypeStruct((N_TOK, w), F32) for w in IN_GROUPS],
        compiler_params=_params(("arbitrary",)),
        name="in_proj",
    )(x, mod6, w_in_l)


def _rot_cols(w):
    quarter = MLA_ROPE_DIM // 4
    j = np.arange(MLA_ROPE_DIM)
    first = (j % (2 * quarter)) < quarter
    src = np.where(first, j + quarter, j - quarter)
    sign = np.where(first, -1.0, 1.0).astype(np.float32)
    return w[..., src] * sign


def prep_w_in(w_in_l):
    kpe = w_in_l[:, -MLA_ROPE_DIM:]
    return jnp.concatenate([w_in_l, _rot_cols(kpe)], axis=1).astype(BF16)


HG_CHUNK = 64
HG_SAFE_EXP = 80.0


def _silu(x):
    return x * jax.nn.sigmoid(x)


HG_TB = SEQ
HG_BLOCKS = N_TOK // HG_TB
HG_CTX_BLOCKS = N_CTX // HG_TB
HG_SEQ_BLOCKS = DEC_SEQ // HG_TB


def _hg_exact_att(q, k, cum):
    c = q.shape[0]
    lane = lax.broadcasted_iota(jnp.int32, (c, c), 1)
    row = lax.broadcasted_iota(jnp.int32, (c, 1), 0)

    def body(s_idx, att):
        sel = row == s_idx
        cum_s = jnp.sum(jnp.where(sel, cum, 0.0), axis=0, keepdims=True)
        k_s = jnp.sum(jnp.where(sel, k, 0.0), axis=0, keepdims=True)
        dec = jnp.exp(jnp.minimum(cum - cum_s, 0.0))
        col = jnp.sum(q * k_s * dec, axis=1, keepdims=True)
        return jnp.where(lane == s_idx, col, att)

    return lax.fori_loop(0, c, body, jnp.zeros((c, c), F32))


def _hgrn_kernel(qf_ref, ff_ref, vf_ref, qb_ref, fb_ref, vb_ref, lbl_ref, s0f_ref, s0b_ref,
                 of_ref, ob_ref, sf_ref, sb_ref, st_scr, *, layer):
    c = HG_CHUNK
    n_c = HG_TB // c
    heads = range(HG_HEADS)
    i = pl.program_id(0)
    blk = (i, HG_BLOCKS - 1 - i)
    is_ctx = tuple(b < HG_CTX_BLOCKS for b in blk)
    pos = tuple(lax.rem(b - HG_CTX_BLOCKS, HG_SEQ_BLOCKS) for b in blk)
    starts = (is_ctx[0] | (pos[0] == 0), is_ctx[1] | (pos[1] == HG_SEQ_BLOCKS - 1))

    for d, s0_ref in enumerate((s0f_ref, s0b_ref)):
        @pl.when(starts[d] & is_ctx[d])
        def _():
            st_scr[d] = jnp.zeros((HG_HEADS, HG_DK, HG_DK), F32)

        @pl.when(starts[d] & jnp.logical_not(is_ctx[d]))
        def _():
            for h in heads:
                st_scr[d, h] = s0_ref[h].T

    lg = lbl_ref[...]
    e = jnp.exp(lg - jnp.max(lg, axis=0, keepdims=True))
    sm = e / jnp.sum(e, axis=0, keepdims=True)
    lb = jnp.zeros_like(sm[0])
    for j in range(1, layer + 1):
        lb = lb + sm[j]

    r = lax.broadcasted_iota(jnp.int32, (c, c), 0)
    s = lax.broadcasted_iota(jnp.int32, (c, c), 1)
    masks = (r >= s, r <= s)
    marks = ((0, c // 2 - 1, c - 1), (c - 1, c // 2, 0))
    in_refs = ((qf_ref, ff_ref, vf_ref), (qb_ref, fb_ref, vb_ref))
    out_refs = (of_ref, ob_ref)

    work = ([], [])
    guard = jnp.float32(0.0)
    for d in range(2):
        q_ref, f_ref, v_ref = in_refs[d]
        lbd = lb[d:d + 1]
        first, mid, last = marks[d]
        for ci in (range(n_c) if d == 0 else reversed(range(n_c))):
            rows = slice(ci * c, (ci + 1) * c)
            f = lbd + (1.0 - lbd) * jax.nn.sigmoid(f_ref[rows, :])
            cum = jnp.dot(masks[d].astype(F32), jnp.log(f), precision=lax.Precision.HIGHEST,
                          preferred_element_type=F32)
            c_mid, c_last = cum[mid:mid + 1], cum[last:last + 1]
            guard = jnp.maximum(guard, jnp.max(jnp.maximum(cum[first:first + 1] - c_mid, c_mid - c_last)))
            work[d].append((rows, _silu(q_ref[rows, :]), 1.0 - f, v_ref[rows, :].astype(BF16), cum))

    nt = (((1,), (1,)), ((), ()))
    tn = (((0,), (0,)), ((), ()))

    def run(exact):
        for d in range(2):
            _, mid, last = marks[d]
            sts = [st_scr[d, h] for h in heads]
            for rows, q, k, v, cum in work[d]:
                c_mid, c_last = cum[mid:mid + 1], cum[last:last + 1]
                q_in = (q * jnp.exp(cum)).astype(BF16)
                k_end = (k * jnp.exp(c_last - cum)).astype(BF16)
                dec = jnp.exp(c_last)
                if not exact:
                    qt = (q * jnp.exp(cum - c_mid)).astype(BF16)
                    kt = (k * jnp.exp(c_mid - cum)).astype(BF16)
                outs = []
                for h in heads:
                    ls = slice(h * LANES, (h + 1) * LANES)
                    inter = lax.dot_general(q_in[:, ls], sts[h].astype(BF16), nt, preferred_element_type=F32)
                    if exact:
                        att = _hg_exact_att(q[:, ls], k[:, ls], cum[:, ls])
                    else:
                        att = lax.dot_general(qt[:, ls], kt[:, ls], nt, preferred_element_type=F32)
                    att = jnp.where(masks[d], att, 0.0).astype(BF16)
                    outs.append(inter + jnp.dot(att, v[:, ls], preferred_element_type=F32))
                    sts[h] = sts[h] * dec[:, ls] + lax.dot_general(v[:, ls], k_end[:, ls], tn,
                                                                   preferred_element_type=F32)
                out_refs[d][rows, :] = jnp.concatenate(outs, axis=1)
            for h in heads:
                st_scr[d, h] = sts[h]

    unsafe = guard > HG_SAFE_EXP
    pl.when(unsafe)(lambda: run(True))
    pl.when(jnp.logical_not(unsafe))(lambda: run(False))

    for d, fin_ref in enumerate((sf_ref, sb_ref)):
        @pl.when(is_ctx[d])
        def _():
            for h in heads:
                fin_ref[h] = st_scr[d, h].T


def hgrn_mixer(z_hg, lb_logits, state, layer):
    last = HG_BLOCKS - 1
    blocks = (lambda i: i, lambda i: last - i)
    lat_batch = lambda b: jnp.clip(lax.div(b - HG_CTX_BLOCKS, HG_SEQ_BLOCKS), 0, DEC_BATCH - 1)
    ctx_seq = lambda b: jnp.minimum(b, BATCH - 1)

    def col(d, group):
        return pl.BlockSpec((HG_TB, HG_WIDTH), lambda i: (blocks[d](i), group))

    def s0(d):
        return pl.BlockSpec((None, None, None, HG_HEADS, HG_DK, HG_DK),
                            lambda i: (lat_batch(blocks[d](i)), layer, d, 0, 0, 0))

    def fin(d):
        return pl.BlockSpec((None, HG_HEADS, HG_DK, HG_DK), lambda i: (ctx_seq(blocks[d](i)), 0, 0, 0))

    return pl.pallas_call(
        functools.partial(_hgrn_kernel, layer=layer),
        grid=(HG_BLOCKS,),
        in_specs=[col(0, 0), col(0, 1), col(0, 3), col(1, 0), col(1, 2), col(1, 3),
                  pl.BlockSpec((DEPTH, 2, HG_WIDTH), lambda i: (0, 0, 0)), s0(0), s0(1)],
        out_specs=[col(0, 0), col(1, 0), fin(0), fin(1)],
        out_shape=[jax.ShapeDtypeStruct((N_TOK, HG_WIDTH), F32), jax.ShapeDtypeStruct((N_TOK, HG_WIDTH), F32),
                   jax.ShapeDtypeStruct((BATCH, HG_HEADS, HG_DK, HG_DK), F32),
                   jax.ShapeDtypeStruct((BATCH, HG_HEADS, HG_DK, HG_DK), F32)],
        scratch_shapes=[pltpu.VMEM((2, HG_HEADS, HG_DK, HG_DK), F32)],
        compiler_params=_params(("arbitrary",)),
        name="hgrn",
    )(z_hg, z_hg, z_hg, z_hg, z_hg, z_hg, lb_logits, state, state)


RG_ROWS = 256
RG_PAD = SUBLANES


def _tile_scan(a, b, reverse):
    row = lax.broadcasted_iota(jnp.int32, a.shape, 0)
    for sh in (1, 2, 4):
        if reverse:
            a_s, b_s = pltpu.roll(a, SUBLANES - sh, 0), pltpu.roll(b, SUBLANES - sh, 0)
            valid = row < SUBLANES - sh
        else:
            a_s, b_s = pltpu.roll(a, sh, 0), pltpu.roll(b, sh, 0)
            valid = row >= sh
        b = jnp.where(valid, a * b_s + b, b)
        a = jnp.where(valid, a * a_s, a)
    return a, b


def _rglru_kernel(*refs, seq_len, has_state):
    if has_state:
        (x_ref, y_ref, cw_ref, cb_ref, wr_ref, br_ref, wi_ref, bi_ref, lam_ref, h0_ref,
         o_ref, xp_scr, a_scr, b_scr) = refs
    else:
        (x_ref, y_ref, cw_ref, cb_ref, wr_ref, br_ref, wi_ref, bi_ref, lam_ref,
         o_ref, hfin_ref, xp_scr, a_scr, b_scr) = refs
    t = seq_len
    zeros = jnp.zeros((RG_PAD, LANES), F32)
    xp_scr[0:RG_PAD, :] = zeros
    xp_scr[RG_PAD + t:, :] = zeros
    xp_scr[RG_PAD:RG_PAD + t, :] = x_ref[...]

    cw = cw_ref[...]
    cb = cb_ref[...]
    lam = lam_ref[...]
    sp = jnp.maximum(-lam, 0.0) + jnp.log1p(jnp.exp(-jnp.abs(lam)))

    def gates(ci, _):
        r0 = pl.multiple_of(ci * RG_ROWS, RG_ROWS)
        xc = cb
        for j in range(RG_CONV):
            xc = xc + cw[j:j + 1] * xp_scr[pl.ds(r0 + RG_PAD - RG_CONV // 2 + j, RG_ROWS), :]
        rows = pl.ds(r0, RG_ROWS)
        xcb = xc.astype(BF16)
        for d in range(2):
            r = jax.nn.sigmoid(jnp.dot(xcb, wr_ref[d], preferred_element_type=F32) + br_ref[d])
            ig = jax.nn.sigmoid(jnp.dot(xcb, wi_ref[d], preferred_element_type=F32) + bi_ref[d])
            log_a = -RG_C * r * sp[d:d + 1]
            a_scr[d, rows, :] = jnp.exp(log_a)
            b_scr[d, rows, :] = jnp.sqrt(1.0 - jnp.exp(2.0 * log_a)) * (ig * xc)
        return 0

    lax.fori_loop(0, t // RG_ROWS, gates, 0)

    n_tiles = t // SUBLANES

    def scan(j, carry):
        h_f, h_b = carry
        rows_f = pl.ds(pl.multiple_of(j * SUBLANES, SUBLANES), SUBLANES)
        rows_b = pl.ds(pl.multiple_of((n_tiles - 1 - j) * SUBLANES, SUBLANES), SUBLANES)
        aa, bb = _tile_scan(a_scr[0, rows_f, :], b_scr[0, rows_f, :], False)
        hf = aa * h_f + bb
        b_scr[0, rows_f, :] = hf
        aa, bb = _tile_scan(a_scr[1, rows_b, :], b_scr[1, rows_b, :], True)
        hb = aa * h_b + bb
        b_scr[1, rows_b, :] = hb
        return hf[SUBLANES - 1:SUBLANES], hb[0:1]

    if has_state:
        h0 = h0_ref[...]
        init = (h0[0:1], h0[1:2])
    else:
        init = (jnp.zeros((1, LANES), F32), jnp.zeros((1, LANES), F32))
    h_f, h_b = lax.fori_loop(0, n_tiles, scan, init)
    if not has_state:
        hfin_ref[0:1, :] = h_f
        hfin_ref[1:2, :] = h_b
    o_ref[...] = (b_scr[0] + b_scr[1]) * jax.nn.gelu(y_ref[...], approximate=True)


def rglru_mixer(z_rg, conv_w, conv_b, w_r, b_r, w_i, b_i, lam, state, layer, latent):
    seq_len = DEC_SEQ if latent else SEQ
    n_seq = DEC_BATCH if latent else BATCH
    blk0 = N_CTX // seq_len if latent else 0
    h = RG_HEADS
    vec = lambda rows: pl.BlockSpec((None, rows, LANES), lambda b, hh: (layer, 0, hh))
    wspec = pl.BlockSpec((None, 2, None, RG_BLOCK, RG_BLOCK), lambda b, hh: (layer, 0, hh, 0, 0))
    bspec = pl.BlockSpec((None, 2, 1, LANES), lambda b, hh: (layer, 0, 0, hh))
    in_specs = [
        pl.BlockSpec((seq_len, LANES), lambda b, hh: (blk0 + b, hh)),
        pl.BlockSpec((seq_len, LANES), lambda b, hh: (blk0 + b, h + hh)),
        vec(RG_CONV), vec(1), wspec, bspec, wspec, bspec, vec(2),
    ]
    args = [z_rg, z_rg, conv_w, conv_b, w_r, b_r.reshape(DEPTH, 2, 1, RG_WIDTH), w_i,
            b_i.reshape(DEPTH, 2, 1, RG_WIDTH), lam]
    o_spec = pl.BlockSpec((seq_len, LANES), lambda b, hh: (b, hh))
    o_shape = jax.ShapeDtypeStruct((n_seq * seq_len, RG_WIDTH), F32)
    if latent:
        in_specs.append(pl.BlockSpec((None, None, 2, LANES), lambda b, hh: (b, layer, 0, hh)))
        args.append(state)
        out_specs, out_shape = [o_spec], [o_shape]
    else:
        out_specs = [o_spec, pl.BlockSpec((None, 2, LANES), lambda b, hh: (b, 0, hh))]
        out_shape = [o_shape, jax.ShapeDtypeStruct((BATCH, 2, RG_WIDTH), F32)]
    return pl.pallas_call(
        functools.partial(_rglru_kernel, seq_len=seq_len, has_state=latent),
        grid=(n_seq, h),
        in_specs=in_specs,
        out_specs=out_specs,
        out_shape=out_shape,
        scratch_shapes=[pltpu.VMEM((seq_len + 2 * RG_PAD, LANES), F32),
                        pltpu.VMEM((2, seq_len, LANES), F32), pltpu.VMEM((2, seq_len, LANES), F32)],
        compiler_params=_params(("arbitrary", "arbitrary")),
        name="rglru_lat" if latent else "rglru_ctx",
    )(*args)


MLA_HEAD_PAD = 2 * LANES
MLA_TM = 512
ATT_TQ = 256


def rope_tables():
    half = MLA_ROPE_DIM // 2
    t = np.arange(DEC_SEQ)
    row = (t // GRID_W).astype(np.float32)
    col = (t % GRID_W).astype(np.float32)
    inv = (ROPE_THETA ** (-np.arange(0, half, 2, dtype=np.float32) / half)).astype(np.float32)
    ar, ac = row[:, None] * inv, col[:, None] * inv
    cos = np.concatenate([np.cos(ar), np.cos(ar), np.cos(ac), np.cos(ac)], -1)
    sin = np.concatenate([np.sin(ar), np.sin(ar), np.sin(ac), np.sin(ac)], -1)
    pad = np.zeros((DEC_SEQ, LANES - MLA_ROPE_DIM), np.float32)
    cos_lat = np.tile(np.concatenate([cos, pad], -1), (DEC_BATCH, 1))
    sin_lat = np.tile(np.concatenate([sin, pad], -1), (DEC_BATCH, 1))
    cos_ctx = np.concatenate([np.ones((N_CTX, MLA_ROPE_DIM), np.float32), np.zeros((N_CTX, LANES - MLA_ROPE_DIM), np.float32)], -1)
    sin_ctx = np.zeros((N_CTX, LANES), np.float32)
    return (jnp.asarray(np.concatenate([cos_ctx, cos_lat], 0).astype(np.float32)),
            jnp.asarray(np.concatenate([sin_ctx, sin_lat], 0).astype(np.float32)))


def _rope_group(x, cos, sin):
    return x * cos + pltpu.roll(x, MLA_ROPE_DIM, 1) * sin


def _rms(x, g):
    return x * lax.rsqrt(jnp.mean(x * x, axis=-1, keepdims=True) + RMS_EPS) * g


def _qproj_kernel(cq_ref, g_ref, w_ref, cos_ref, sin_ref, q_ref):
    xn = _rms(cq_ref[...], g_ref[...]).astype(BF16)
    qm = jnp.dot(xn, w_ref[...], preferred_element_type=F32)
    cos, sin = cos_ref[...], sin_ref[...]
    for h in range(MLA_HEADS):
        a = h * MLA_HEAD_PAD
        q_ref[:, a:a + LANES] = qm[:, a:a + LANES].astype(BF16)
        q_ref[:, a + LANES:a + 2 * LANES] = _rope_group(qm[:, a + LANES:a + 2 * LANES], cos, sin).astype(BF16)


def q_proj(z_cq, q_norm_g, w_uq_l, cos_t, sin_t, layer):
    tm = MLA_TM
    width = MLA_HEADS * MLA_HEAD_PAD
    return pl.pallas_call(
        _qproj_kernel,
        grid=(N_TOK // tm,),
        in_specs=[
            pl.BlockSpec((tm, Q_LORA), lambda i: (i, 0)),
            pl.BlockSpec((None, 1, Q_LORA), lambda i: (layer, 0, 0)),
            pl.BlockSpec((Q_LORA, width), lambda i: (0, 0)),
            pl.BlockSpec((tm, LANES), lambda i: (i, 0)),
            pl.BlockSpec((tm, LANES), lambda i: (i, 0)),
        ],
        out_specs=pl.BlockSpec((tm, width), lambda i: (i, 0)),
        out_shape=jax.ShapeDtypeStruct((N_TOK, width), BF16),
        compiler_params=_params(("arbitrary",)),
        name="q_proj",
    )(z_cq, q_norm_g, w_uq_l, cos_t, sin_t)


def prep_w_uq(w_uq_l):
    w = w_uq_l.reshape(Q_LORA, MLA_HEADS, MLA_QK_DIM)
    pe = w[..., MLA_NOPE_DIM:]
    w = jnp.concatenate([w, _rot_cols(pe)], axis=-1)
    return w.reshape(Q_LORA, MLA_HEADS * MLA_HEAD_PAD).astype(BF16)


def _kvproj_kernel(zkv_ref, g_ref, wk_ref, wv_ref, cos_ref, sin_ref, *outs, normalize):
    if normalize:
        k_ref, v_ref, ckv_ref, kpe_ref = outs
    else:
        k_ref, v_ref = outs
    ckv = zkv_ref[:, 0:KV_LORA]
    if normalize:
        ckv = _rms(ckv, g_ref[...])
        ckv_ref[...] = ckv
    pe_group = zkv_ref[:, KV_LORA:KV_LORA + LANES]
    if normalize:
        kpe_ref[...] = pe_group[:, 0:MLA_ROPE_DIM]
    pe = _rope_group(pe_group, cos_ref[...], sin_ref[...]).astype(BF16)
    cb = ckv.astype(BF16)
    kn = jnp.dot(cb, wk_ref[...], preferred_element_type=F32)
    v_ref[...] = jnp.dot(cb, wv_ref[...], preferred_element_type=F32).astype(BF16)
    for h in range(MLA_HEADS):
        a = h * MLA_HEAD_PAD
        k_ref[:, a:a + LANES] = kn[:, h * LANES:(h + 1) * LANES].astype(BF16)
        k_ref[:, a + LANES:a + 2 * LANES] = pe


def kv_proj(z_kv, kv_norm_g, w_uk_l, w_uv_l, cos_t, sin_t, layer):
    tm = MLA_TM
    kw, vw = MLA_HEADS * MLA_HEAD_PAD, MLA_HEADS * MLA_V_DIM
    return pl.pallas_call(
        functools.partial(_kvproj_kernel, normalize=True),
        grid=(N_TOK // tm,),
        in_specs=[
            pl.BlockSpec((tm, W_KV), lambda i: (i, 0)),
            pl.BlockSpec((None, 1, KV_LORA), lambda i: (layer, 0, 0)),
            pl.BlockSpec((KV_LORA, MLA_HEADS * MLA_NOPE_DIM), lambda i: (0, 0)),
            pl.BlockSpec((KV_LORA, vw), lambda i: (0, 0)),
            pl.BlockSpec((tm, LANES), lambda i: (i, 0)),
            pl.BlockSpec((tm, LANES), lambda i: (i, 0)),
        ],
        out_specs=[
            pl.BlockSpec((tm, kw), lambda i: (i, 0)),
            pl.BlockSpec((tm, vw), lambda i: (i, 0)),
            pl.BlockSpec((tm, KV_LORA), lambda i: (i, 0)),
            pl.BlockSpec((tm, MLA_ROPE_DIM), lambda i: (i, 0)),
        ],
        out_shape=[
            jax.ShapeDtypeStruct((N_TOK, kw), BF16),
            jax.ShapeDtypeStruct((N_TOK, vw), BF16),
            jax.ShapeDtypeStruct((N_TOK, KV_LORA), F32),
            jax.ShapeDtypeStruct((N_TOK, MLA_ROPE_DIM), F32),
        ],
        compiler_params=_params(("arbitrary",)),
        name="kv_proj",
    )(z_kv, kv_norm_g, w_uk_l, w_uv_l, cos_t, sin_t)


def kv_proj_cache(cache_kv, kv_norm_g, w_uk_l, w_uv_l, cos_c, sin_c, layer):
    tm = PAST_LEN
    kw, vw = MLA_HEADS * MLA_HEAD_PAD, MLA_HEADS * MLA_V_DIM
    n = DEC_BATCH * PAST_LEN
    return pl.pallas_call(
        functools.partial(_kvproj_kernel, normalize=False),
        grid=(DEC_BATCH,),
        in_specs=[
            pl.BlockSpec((tm, W_KV), lambda b: (b, 0)),
            pl.BlockSpec((None, 1, KV_LORA), lambda b: (layer, 0, 0)),
            pl.BlockSpec((KV_LORA, MLA_HEADS * MLA_NOPE_DIM), lambda b: (0, 0)),
            pl.BlockSpec((KV_LORA, vw), lambda b: (0, 0)),
            pl.BlockSpec((tm, LANES), lambda b: (0, 0)),
            pl.BlockSpec((tm, LANES), lambda b: (0, 0)),
        ],
        out_specs=[pl.BlockSpec((tm, kw), lambda b: (b, 0)), pl.BlockSpec((tm, vw), lambda b: (b, 0))],
        out_shape=[jax.ShapeDtypeStruct((n, kw), BF16), jax.ShapeDtypeStruct((n, vw), BF16)],
        compiler_params=_params(("arbitrary",)),
        name="kv_proj_cache",
    )(cache_kv, kv_norm_g, w_uk_l, w_uv_l, cos_c, sin_c)


def _attn_kernel(q_ref, *refs):
    o_ref = refs[-1]
    segs = [(refs[i], refs[i + 1]) for i in range(0, len(refs) - 1, 2)]
    nt = (((1,), (1,)), ((), ()))
    q = q_ref[...]
    scores = [lax.dot_general(q, k_ref[...], nt, preferred_element_type=F32) for k_ref, _ in segs]
    m = scores[0].max(axis=-1, keepdims=True)
    for s in scores[1:]:
        m = jnp.maximum(m, s.max(axis=-1, keepdims=True))
    l, o = 0.0, 0.0
    for s, (_, v_ref) in zip(scores, segs):
        p = jnp.exp((s - m) * (MLA_QK_DIM ** -0.5))
        l = l + jnp.sum(p, axis=-1, keepdims=True)
        o = o + jnp.dot(p.astype(BF16), v_ref[...], preferred_element_type=F32)
    o_ref[...] = o / l


def mla_attention(q, k_tok, v_tok, k_cache, v_cache, latent):
    h = MLA_HEADS
    if latent:
        tq = ATT_TQ
        n_q = DEC_SEQ // tq
        grid = (DEC_BATCH, h, n_q)
        q_map = lambda b, hh, i: (N_CTX // tq + b * n_q + i, hh)
        kv_map = lambda b, hh, i: (N_CTX // DEC_SEQ + b, hh)
        o_map = lambda b, hh, i: (b * n_q + i, hh)
        in_specs = [
            pl.BlockSpec((tq, MLA_HEAD_PAD), q_map),
            pl.BlockSpec((DEC_SEQ, MLA_HEAD_PAD), kv_map),
            pl.BlockSpec((DEC_SEQ, MLA_V_DIM), kv_map),
            pl.BlockSpec((PAST_LEN, MLA_HEAD_PAD), lambda b, hh, i: (b, hh)),
            pl.BlockSpec((PAST_LEN, MLA_V_DIM), lambda b, hh, i: (b, hh)),
        ]
        args = [q, k_tok, v_tok, k_cache, v_cache]
        n_out = N_LAT
    else:
        tq = SEQ
        grid = (BATCH, h, 1)
        q_map = o_map = lambda b, hh, i: (b, hh)
        in_specs = [
            pl.BlockSpec((tq, MLA_HEAD_PAD), q_map),
            pl.BlockSpec((SEQ, MLA_HEAD_PAD), q_map),
            pl.BlockSpec((SEQ, MLA_V_DIM), q_map),
        ]
        args = [q, k_tok, v_tok]
        n_out = N_CTX
    return pl.pallas_call(
        _attn_kernel,
        grid=grid,
        in_specs=in_specs,
        out_specs=pl.BlockSpec((tq, MLA_V_DIM), o_map),
        out_shape=jax.ShapeDtypeStruct((n_out, MLA_WIDTH), F32),
        compiler_params=_params(("arbitrary",) * 3),
        name="mla_attn_lat" if latent else "mla_attn_ctx",
    )(*args)


OUT_TM = 256


def _layer_norm(y, g, b):
    mu = jnp.mean(y, axis=-1, keepdims=True)
    yc = y - mu
    var = jnp.mean(yc * yc, axis=-1, keepdims=True)
    return yc * lax.rsqrt(var + LN_EPS) * g + b


def _outproj_kernel(of_ref, ob_ref, hgg_ref, hgn_ref, org_ref, omla_ref, w_ref, x_ref, mod_ref, g_ref, b_ref, wr_ref,
                    x1_ref, hf_ref, pt_ref):
    o = of_ref[...] + ob_ref[...]
    heads = [o[:, h * HG_DK:(h + 1) * HG_DK] for h in range(HG_HEADS)]
    o = jnp.concatenate([oh * lax.rsqrt(jnp.mean(oh * oh, axis=-1, keepdims=True) + RMS_EPS) for oh in heads], axis=1)
    o_hg = o * hgn_ref[...] * _silu(hgg_ref[...])
    m = jnp.dot(o_hg.astype(BF16), w_ref[0:HG_WIDTH, :], preferred_element_type=F32)
    m += jnp.dot(org_ref[...].astype(BF16), w_ref[HG_WIDTH:HG_WIDTH + RG_WIDTH, :], preferred_element_type=F32)
    m += jnp.dot(omla_ref[...].astype(BF16), w_ref[HG_WIDTH + RG_WIDTH:, :], preferred_element_type=F32)
    md = mod_ref[...]
    x1 = _layer_norm(DN_ALPHA * x_ref[...] + md[2:3] * m, g_ref[...], b_ref[...])
    x1_ref[...] = x1
    hf = x1 * (1.0 + md[4:5]) + md[3:4]
    hf_ref[...] = hf
    logits = jnp.dot(hf.astype(BF16), wr_ref[...], preferred_element_type=F32)
    lane = lax.broadcasted_iota(jnp.int32, logits.shape, 1)
    logits = jnp.where(lane < N_EXPERTS, logits, -jnp.inf)
    e = jnp.exp(logits - jnp.max(logits, axis=-1, keepdims=True))
    p = e / jnp.sum(e, axis=-1, keepdims=True)
    pt_ref[...] = p.T[0:N_EXPERTS, :]


def out_proj(o_hg_f, o_hg_b, z_hg, hg_norm_g, o_rg, o_mla, w_out_l, x, mod6, ln_g, ln_b, w_router_l, layer):
    tm = OUT_TM
    row = lambda w: pl.BlockSpec((tm, w), lambda i: (i, 0))
    const = lambda shape: pl.BlockSpec(shape, lambda i: (0,) * len(shape), pipeline_mode=pl.Buffered(1))
    return pl.pallas_call(
        _outproj_kernel,
        grid=(N_TOK // tm,),
        in_specs=[
            row(HG_WIDTH), row(HG_WIDTH),
            pl.BlockSpec((tm, HG_WIDTH), lambda i: (i, 4)),
            pl.BlockSpec((None, 1, HG_WIDTH), lambda i: (layer, 0, 0)),
            row(RG_WIDTH), row(MLA_WIDTH),
            const((D_MODEL, D_MODEL)),
            row(D_MODEL),
            pl.BlockSpec((None, 6, D_MODEL), lambda i: (layer * N_MODROWS + _mod_row(i, tm), 0, 0)),
            pl.BlockSpec((None, 1, D_MODEL), lambda i: (layer, 0, 0)),
            pl.BlockSpec((None, 1, D_MODEL), lambda i: (layer, 0, 0)),
            const((D_MODEL, LANES)),
        ],
        out_specs=[row(D_MODEL), row(D_MODEL), pl.BlockSpec((N_EXPERTS, tm), lambda i: (0, i))],
        out_shape=[jax.ShapeDtypeStruct((N_TOK, D_MODEL), F32),
                   jax.ShapeDtypeStruct((N_TOK, D_MODEL), F32),
                   jax.ShapeDtypeStruct((N_EXPERTS, N_TOK), F32)],
        compiler_params=_params(("arbitrary",)),
        name="out_proj",
    )(o_hg_f, o_hg_b, z_hg, hg_norm_g, o_rg, o_mla, w_out_l, x, mod6, ln_g, ln_b, w_router_l)


CAP_CTX = CAP_FACTOR * N_CTX // N_EXPERTS
CAP_LAT = CAP_FACTOR * N_LAT // N_EXPERTS
ROUTE_SETS = ((0, N_CTX, CAP_CTX), (N_CTX, N_LAT, CAP_LAT))
SLOTS = CAP_CTX + CAP_LAT


def _route_thr_kernel(pt_ref, thr_ref, need_ref):
    for si, (start, n, cap) in enumerate(ROUTE_SETS):
        p = pt_ref[:, start:start + n]

        def count(mask):
            return jnp.sum(mask.astype(F32), axis=1, keepdims=True)

        def body(i, t):
            cand = t | jnp.left_shift(jnp.int32(1), 30 - i)
            return jnp.where(count(p >= pltpu.bitcast(cand, F32)) >= cap, cand, t)

        t = pltpu.bitcast(lax.fori_loop(0, 31, body, jnp.zeros((N_EXPERTS, 1), jnp.int32)), F32)
        need = cap - count(p > t)
        thr_ref[si] = jnp.broadcast_to(t, (N_EXPERTS, LANES))
        need_ref[si] = jnp.broadcast_to(need, (N_EXPERTS, LANES))


def route_threshold(p_t):
    n_sets = len(ROUTE_SETS)
    return pl.pallas_call(
        _route_thr_kernel,
        out_shape=[jax.ShapeDtypeStruct((n_sets, N_EXPERTS, LANES), F32),
                   jax.ShapeDtypeStruct((n_sets, N_EXPERTS, LANES), F32)],
        compiler_params=_params(None),
        name="route_threshold",
    )(p_t)


def _route_lists_kernel(p_ref, thr_ref, need_ref, lists_ref, cnt_ref, first_ref, sel_ref, *, nb, cap, tok0, row0):
    nt = (((1,), (1,)), ((), ()))
    r128 = lax.broadcasted_iota(jnp.int32, (LANES, LANES), 0)
    c128 = lax.broadcasted_iota(jnp.int32, (LANES, LANES), 1)
    incl = (r128 <= c128).astype(BF16)
    eye = r128 == c128
    rb = lax.broadcasted_iota(jnp.int32, (nb, nb), 0)
    cb = lax.broadcasted_iota(jnp.int32, (nb, nb), 1)
    below = (cb < rb).astype(BF16)
    incl_b = (rb <= cb).astype(BF16)
    ones8 = jnp.ones((SUBLANES, LANES), BF16)
    s_col = lax.broadcasted_iota(jnp.int32, (cap, 1), 0).astype(F32)
    lane = lax.broadcasted_iota(jnp.int32, (cap, LANES), 1)
    lane_f = lane.astype(F32)
    j_row = lax.broadcasted_iota(jnp.int32, (1, nb), 1).astype(F32)

    def dot(a, b):
        return jnp.dot(a, b, preferred_element_type=F32)

    def block_base(totals, unit):
        hi = jnp.floor(totals * (1.0 / unit))
        lo = totals - unit * hi
        bc = lambda a: jnp.broadcast_to(a, (nb, LANES)).astype(BF16)
        return unit * dot(below, bc(hi)) + dot(below, bc(lo))

    def choose(e, acc):
        p = p_ref[e]
        t = thr_ref[pl.ds(e, 1), :]
        need = need_ref[pl.ds(e, 1), :]
        eq = p == t
        eq_f = eq.astype(F32)
        eq_lp = dot(eq_f.astype(BF16), incl)
        tie_rank = eq_lp - eq_f + block_base(eq_lp[:, LANES - 1:LANES], 16.0)
        sel_f = jnp.where((p > t) | (eq & (tie_rank < need)), 1.0, 0.0)
        sel_ref[e] = sel_f
        return acc + sel_f

    cnt = lax.fori_loop(0, N_EXPERTS, choose, jnp.zeros((nb, LANES), F32))
    cnt_lp = dot(cnt.astype(BF16), incl)
    first = cnt_lp - cnt + block_base(cnt_lp[:, LANES - 1:LANES], 64.0) + row0
    first_hi = jnp.floor(first * (1.0 / LANES))
    first_lo = first - LANES * first_hi

    def expert(e, acc):
        p = p_ref[e]
        sel_f = sel_ref[e]
        sel_b = sel_f.astype(BF16)
        lp = dot(sel_b, incl)
        c_row = lax.dot_general(ones8, sel_b, nt, preferred_element_type=F32)
        incl_row = dot(c_row.astype(BF16), incl_b)[0:1]
        excl_row = incl_row - c_row[0:1]
        oh_j = jnp.where((s_col >= excl_row) & (s_col < incl_row), 1.0, 0.0)
        base_s = jnp.sum(oh_j * excl_row, axis=1, keepdims=True)
        j_s = jnp.sum(oh_j * j_row, axis=1, keepdims=True)
        oh_jb = oh_j.astype(BF16)
        lp_rows = dot(oh_jb, lp.astype(BF16))
        pos = jnp.sum(jnp.where(lp_rows <= s_col - base_s, 1.0, 0.0), axis=1, keepdims=True)
        oh_c = lane_f == pos
        p1 = p.astype(BF16)
        r1 = p - p1.astype(F32)
        p2 = r1.astype(BF16)
        p3 = (r1 - p2.astype(F32)).astype(BF16)
        p_rows = dot(oh_jb, p1) + dot(oh_jb, p2) + dot(oh_jb, p3)
        gate = jnp.sum(jnp.where(oh_c, p_rows, 0.0), axis=1, keepdims=True)
        rank_rows = dot(oh_jb, acc.astype(BF16))
        first_rows = LANES * dot(oh_jb, first_hi.astype(BF16)) + dot(oh_jb, first_lo.astype(BF16))
        dst = jnp.sum(jnp.where(oh_c, rank_rows + first_rows, 0.0), axis=1, keepdims=True)
        idx = tok0 + LANES * j_s + pos
        lists_ref[e] = jnp.where(lane == 0, idx, jnp.where(lane == 1, dst, jnp.where(lane == 2, gate, 0.0)))
        return acc + sel_f

    lax.fori_loop(0, N_EXPERTS, expert, jnp.zeros((nb, LANES), F32))

    ones_b = jnp.ones((LANES, LANES), BF16)

    def column(a, j):
        diag = jnp.where(eye, jnp.broadcast_to(a[j:j + 1, :], (LANES, LANES)), 0.0)
        return dot(diag.astype(BF16), ones_b)

    for j in range(nb):
        rows = slice(j * LANES, (j + 1) * LANES)
        cnt_ref[rows, :] = column(cnt, j)
        first_ref[rows, :] = LANES * column(first_hi, j) + column(first_lo, j)


def route_lists(p_blk, thr, need, set_index):
    tok0, n, cap = ROUTE_SETS[set_index]
    nb = n // LANES
    row0 = float(CAP_FACTOR * tok0)
    kern = functools.partial(_route_lists_kernel, nb=nb, cap=cap, tok0=tok0, row0=row0)
    return pl.pallas_call(
        kern,
        grid=(1,),
        in_specs=[
            pl.BlockSpec((N_EXPERTS, nb, LANES), lambda i: (0, 0, 0)),
            pl.BlockSpec((None, N_EXPERTS, LANES), lambda i: (set_index, 0, 0)),
            pl.BlockSpec((None, N_EXPERTS, LANES), lambda i: (set_index, 0, 0)),
        ],
        out_specs=[pl.BlockSpec((N_EXPERTS, cap, LANES), lambda i: (0, 0, 0)),
                   pl.BlockSpec((n, LANES), lambda i: (0, 0)),
                   pl.BlockSpec((n, LANES), lambda i: (0, 0))],
        out_shape=[jax.ShapeDtypeStruct((N_EXPERTS, cap, LANES), F32),
                   jax.ShapeDtypeStruct((n, LANES), F32),
                   jax.ShapeDtypeStruct((n, LANES), F32)],
        scratch_shapes=[pltpu.VMEM((N_EXPERTS, nb, LANES), F32)],
        compiler_params=_params(("arbitrary",)),
        name="route_lists_lat" if set_index else "route_lists_ctx",
    )(p_blk, thr, need)


FFN_TM = 512
FFN_ISSUE_UNROLL = 8


def _ffn_kernel(tok_ref, dst_ref, hf_ref, lists_ref, wg_ref, wu_ref, wd_ref, yc_ref, x_buf, y_buf, sem_in, sem_out):
    tm = FFN_TM
    step = pl.program_id(0) * pl.num_programs(1) + pl.program_id(1)
    n_steps = pl.num_programs(0) * pl.num_programs(1)

    def gather(i, _):
        pltpu.make_async_copy(hf_ref.at[pl.ds(tok_ref[0, 0, i], 1), :], x_buf.at[pl.ds(i, 1), :], sem_in).start()
        return 0

    def scatter(i, _):
        pltpu.make_async_copy(y_buf.at[pl.ds(i, 1), :], yc_ref.at[pl.ds(dst_ref[0, 0, i], 1), :], sem_out).start()
        return 0

    def wait_scatter():
        pltpu.make_async_copy(y_buf, yc_ref.at[pl.ds(0, tm), :], sem_out).wait()

    lax.fori_loop(0, tm, gather, 0, unroll=FFN_ISSUE_UNROLL)

    @pl.when(step > 0)
    def _():
        wait_scatter()

    pltpu.make_async_copy(hf_ref.at[pl.ds(0, tm), :], x_buf, sem_in).wait()
    x = x_buf[...].astype(BF16)
    gate = jnp.dot(x, wg_ref[...], preferred_element_type=F32)
    up = jnp.dot(x, wu_ref[...], preferred_element_type=F32)
    hid = (_silu(gate) * up).astype(BF16)
    y_buf[...] = jnp.dot(hid, wd_ref[...], preferred_element_type=F32) * lists_ref[:, 2:3]
    lax.fori_loop(0, tm, scatter, 0, unroll=FFN_ISSUE_UNROLL)

    @pl.when(step == n_steps - 1)
    def _():
        wait_scatter()


def expert_ffn(tok, dst, hf, lists, w_gate_l, w_up_l, w_down_l):
    tm = FFN_TM
    per_e = SLOTS // tm
    idx_spec = pl.BlockSpec((1, 1, tm), lambda e, i: (e * per_e + i, 0, 0), memory_space=pltpu.SMEM)
    as_tiles = lambda a: a.reshape(N_EXPERTS * per_e, 1, tm)
    return pl.pallas_call(
        _ffn_kernel,
        grid=(N_EXPERTS, per_e),
        in_specs=[
            idx_spec, idx_spec,
            pl.BlockSpec(memory_space=pl.ANY),
            pl.BlockSpec((None, tm, LANES), lambda e, i: (e, i, 0)),
            pl.BlockSpec((None, D_MODEL, D_FF_EXPERT), lambda e, i: (e, 0, 0)),
            pl.BlockSpec((None, D_MODEL, D_FF_EXPERT), lambda e, i: (e, 0, 0)),
            pl.BlockSpec((None, D_FF_EXPERT, D_MODEL), lambda e, i: (e, 0, 0)),
        ],
        out_specs=pl.BlockSpec(memory_space=pl.ANY),
        out_shape=jax.ShapeDtypeStruct((N_CHOICES, D_MODEL), F32),
        scratch_shapes=[pltpu.VMEM((tm, D_MODEL), F32), pltpu.VMEM((tm, D_MODEL), F32),
                        pltpu.SemaphoreType.DMA(()), pltpu.SemaphoreType.DMA(())],
        compiler_params=_params(("arbitrary", "arbitrary")),
        name="expert_ffn",
    )(as_tiles(tok), as_tiles(dst), hf, lists, w_gate_l, w_up_l, w_down_l)


COMB_TM = 512
COMB_ROWS = 256
N_CHOICES = N_EXPERTS * SLOTS


def _combine_kernel(rows_ref, y_ref, cnt_ref, first_ref, x1_ref, mod_ref, g_ref, b_ref, o_ref, buf, sem, acc_ref):
    i = pl.program_id(0)
    lc = COMB_ROWS
    r0, r1 = rows_ref[i], rows_ref[i + 1]
    base = lax.div(r0, lc) * lc
    n_chunks = lax.div(r1 - base + lc - 1, lc)

    def chunk_start(k):
        return pl.multiple_of(jnp.minimum(base + k * lc, N_CHOICES - lc), lc)

    def copy(k, slot):
        return pltpu.make_async_copy(y_ref.at[pl.ds(chunk_start(k), lc), :], buf.at[slot], sem.at[slot])

    acc_ref[...] = jnp.zeros_like(acc_ref)

    @pl.when(n_chunks > 0)
    def _():
        copy(0, 0).start()

    first = first_ref[:, 0:1]
    last = first + cnt_ref[:, 0:1]
    col = lax.broadcasted_iota(jnp.int32, (1, lc), 1)

    def body(k, _):
        slot = lax.rem(k, 2)
        copy(k, slot).wait()

        @pl.when(k + 1 < n_chunks)
        def _():
            copy(k + 1, 1 - slot).start()

        row_id = (chunk_start(k) + col).astype(F32)
        mine = (row_id >= first) & (row_id < last) & (row_id >= (base + k * lc).astype(F32))
        onehot = jnp.where(mine, 1.0, 0.0).astype(BF16)
        y = buf[slot]
        hi = y.astype(BF16)
        lo = (y - hi.astype(F32)).astype(BF16)
        acc_ref[...] += (jnp.dot(onehot, hi, preferred_element_type=F32)
                         + jnp.dot(onehot, lo, preferred_element_type=F32))
        return 0

    lax.fori_loop(0, n_chunks, body, 0)
    g2 = mod_ref[...][5:6]
    o_ref[...] = _layer_norm(DN_ALPHA * x1_ref[...] + g2 * acc_ref[...], g_ref[...], b_ref[...])


def moe_combine(tile_rows, y_choices, cnt, first, x1, mod6, ln_g, ln_b, layer):
    tm = COMB_TM
    row = lambda w: pl.BlockSpec((tm, w), lambda i, t: (i, 0))
    grid_spec = pltpu.PrefetchScalarGridSpec(
        num_scalar_prefetch=1,
        grid=(N_TOK // tm,),
        in_specs=[
            pl.BlockSpec(memory_space=pl.ANY),
            row(LANES), row(LANES), row(D_MODEL),
            pl.BlockSpec((None, 6, D_MODEL), lambda i, t: (layer * N_MODROWS + _mod_row(i, tm), 0, 0)),
            pl.BlockSpec((None, 1, D_MODEL), lambda i, t: (layer, 0, 0)),
            pl.BlockSpec((None, 1, D_MODEL), lambda i, t: (layer, 0, 0)),
        ],
        out_specs=row(D_MODEL),
        scratch_shapes=[pltpu.VMEM((2, COMB_ROWS, D_MODEL), F32), pltpu.SemaphoreType.DMA((2,)),
                        pltpu.VMEM((tm, D_MODEL), F32)],
    )
    return pl.pallas_call(
        _combine_kernel,
        grid_spec=grid_spec,
        out_shape=jax.ShapeDtypeStruct((N_TOK, D_MODEL), F32),
        compiler_params=_params(("arbitrary",)),
        name="moe_combine",
    )(tile_rows, y_choices, cnt, first, x1, mod6, ln_g, ln_b)


def moe_block(x1, hf, p_t, mod6, ln_g, ln_b, w_gate_l, w_up_l, w_down_l, layer):
    thr, need = route_threshold(p_t)
    parts = []
    for si, (tok0, n, cap) in enumerate(ROUTE_SETS):
        p_blk = p_t[:, tok0:tok0 + n].reshape(N_EXPERTS, n // LANES, LANES)
        parts.append(route_lists(p_blk, thr, need, si))
    lists = jnp.concatenate([p[0] for p in parts], axis=1)
    cnt = jnp.concatenate([p[1] for p in parts], axis=0)
    first = jnp.concatenate([p[2] for p in parts], axis=0)
    tok = lists[:, :, 0].astype(jnp.int32)
    dst = lists[:, :, 1].astype(jnp.int32)
    y_choices = expert_ffn(tok, dst, hf, lists, w_gate_l, w_up_l, w_down_l)
    tile_rows = jnp.concatenate([first[::COMB_TM, 0], jnp.full((1,), N_CHOICES, F32)]).astype(jnp.int32)
    return moe_combine(tile_rows, y_choices, cnt, first, x1, mod6, ln_g, ln_b, layer)


def kernel(x_prompt, x_sample, cache_mla_ckv, cache_mla_kpe, state_hgrn, state_rglru, c, c_ctx,
           w_in, w_out, hg_lb_logits, hg_norm_g, rg_conv_w, rg_conv_b, rg_w_r, rg_b_r, rg_w_i, rg_b_i,
           rg_lambda, mla_q_norm_g, mla_kv_norm_g, mla_w_uq, mla_w_uk, mla_w_uv, ada_w, ada_b,
           ln1_g, ln1_b, ln2_g, ln2_b, moe_router, moe_w_gate, moe_w_up, moe_w_down):
    x = jnp.concatenate([x_prompt.reshape(N_CTX, D_MODEL), x_sample.reshape(N_LAT, D_MODEL)], axis=0)
    cvec = jnp.concatenate([c_ctx[None, :], c, jnp.zeros((SUBLANES - N_MODROWS, D_MODEL), F32)], axis=0)
    mod = ada_mod(cvec, ada_w, ada_b)
    mod6 = mod[:, :N_MODROWS].reshape(DEPTH * N_MODROWS, 6, D_MODEL)
    cos_t, sin_t = rope_tables()
    vec = lambda a: a.reshape(DEPTH, 1, a.shape[-1])
    hg_ng, cb, qg, kg = vec(hg_norm_g), vec(rg_conv_b), vec(mla_q_norm_g), vec(mla_kv_norm_g)
    g1, b1, g2, b2 = vec(ln1_g), vec(ln1_b), vec(ln2_g), vec(ln2_b)
    w_r, w_i = rg_w_r.astype(BF16), rg_w_i.astype(BF16)
    router = jnp.pad(moe_router, ((0, 0), (0, 0), (0, LANES - N_EXPERTS))).astype(BF16)

    ckvs, kpes, hgs, rgs = [], [], [], []
    for l in range(DEPTH):
        z_hg, z_rg, z_cq, z_kv = in_proj(x, mod6, prep_w_in(w_in[l]), l)

        o_hg_f, o_hg_b, hg_fin_f, hg_fin_b = hgrn_mixer(z_hg, hg_lb_logits, state_hgrn, l)

        rg_args = (rg_conv_w, cb, w_r, rg_b_r, w_i, rg_b_i, rg_lambda)
        o_rg_c, rg_fin = rglru_mixer(z_rg, *rg_args, None, l, False)
        (o_rg_l,) = rglru_mixer(z_rg, *rg_args, state_rglru, l, True)
        o_rg = jnp.concatenate([o_rg_c, o_rg_l], axis=0)

        w_uk, w_uv = mla_w_uk[l].astype(BF16), mla_w_uv[l].astype(BF16)
        q = q_proj(z_cq, qg, prep_w_uq(mla_w_uq[l]), cos_t, sin_t, l)
        k_tok, v_tok, ckv_n, kpe = kv_proj(z_kv, kg, w_uk, w_uv, cos_t, sin_t, l)
        cache = jnp.concatenate([cache_mla_ckv[:, l].reshape(DEC_BATCH * PAST_LEN, KV_LORA),
                                 cache_mla_kpe[:, l].reshape(DEC_BATCH * PAST_LEN, MLA_ROPE_DIM),
                                 jnp.zeros((DEC_BATCH * PAST_LEN, MLA_ROPE_DIM), F32)], axis=1)
        k_cache, v_cache = kv_proj_cache(cache, kg, w_uk, w_uv, cos_t[:PAST_LEN], sin_t[:PAST_LEN], l)
        o_mla = jnp.concatenate([mla_attention(q, k_tok, v_tok, None, None, False),
                                 mla_attention(q, k_tok, v_tok, k_cache, v_cache, True)], axis=0)

        x1, hf, p_t = out_proj(o_hg_f, o_hg_b, z_hg, hg_ng, o_rg, o_mla, w_out[l].astype(BF16), x, mod6, g1, b1,
                               router[l], l)
        x = moe_block(x1, hf, p_t, mod6, g2, b2, moe_w_gate[l].astype(BF16), moe_w_up[l].astype(BF16),
                      moe_w_down[l].astype(BF16), l)

        ckvs.append(ckv_n[:N_CTX].reshape(BATCH, SEQ, KV_LORA))
        kpes.append(kpe[:N_CTX].reshape(BATCH, SEQ, MLA_ROPE_DIM))
        hgs.append(jnp.stack([hg_fin_f, hg_fin_b], axis=1))
        rgs.append(rg_fin)

    y_prompt = x[:N_CTX].reshape(BATCH, SEQ, D_MODEL)
    y_sample = x[N_CTX:].reshape(DEC_BATCH, DEC_SEQ, D_MODEL)
    return (y_prompt, y_sample, jnp.stack(ckvs, axis=1), jnp.stack(kpes, axis=1),
            jnp.stack(hgs, axis=1), jnp.stack(rgs, axis=1))
```

```python
import functools
import math

import jax
import jax.numpy as jnp
import numpy as np
from jax import lax
from jax.experimental import pallas as pl
from jax.experimental.pallas import tpu as pltpu

F32 = jnp.float32
BF16 = jnp.bfloat16

D_MODEL = 2048
BATCH = 16
SEQ = 256
DEPTH = 2
DEC_BATCH = 2
DEC_SEQ = 4096
PAST_LEN = 512
GRID_W = 64
HG_WIDTH = 512
HG_HEADS = 4
HG_DK = 128
RG_WIDTH = 512
RG_HEADS = 4
RG_BLOCK = 128
RG_CONV = 4
RG_C = 8.0
MLA_WIDTH = 1024
MLA_HEADS = 8
MLA_V_DIM = 128
MLA_NOPE_DIM = 128
MLA_ROPE_DIM = 64
MLA_QK_DIM = 192
Q_LORA = 512
KV_LORA = 512
ROPE_THETA = 10000.0
Q_BLOCK = 128
N_EXPERTS = 16
CAP_FACTOR = 2
D_FF_EXPERT = 1024
DN_ALPHA = (2.0 * DEPTH) ** 0.25
LN_EPS = 1e-5
RMS_EPS = 1e-6

N_CTX = BATCH * SEQ
N_LAT = DEC_BATCH * DEC_SEQ
N_TOK = N_CTX + N_LAT
N_MODROWS = 1 + DEC_BATCH

LANES = 128
SUBLANES = 8
VMEM_LIMIT_BYTES = 56 * 1024 * 1024

W_HG = 5 * HG_WIDTH
W_RG = 2 * RG_WIDTH
W_KV = KV_LORA + 2 * MLA_ROPE_DIM
IN_GROUPS = (W_HG, W_RG, Q_LORA, W_KV)
IN_COLS_OWN = sum(IN_GROUPS)


def _params(sem, vmem=VMEM_LIMIT_BYTES):
    return pltpu.CompilerParams(dimension_semantics=sem, vmem_limit_bytes=vmem)


def _mod_row(i, tm):
    n_ctx_tiles = N_CTX // tm
    per_batch = DEC_SEQ // tm
    return jnp.where(i < n_ctx_tiles, 0, 1 + (i - n_ctx_tiles) // per_batch)


ADA_TN = 1024


def _ada_kernel(c_ref, w_ref, b_ref, o_ref):
    c = c_ref[...]
    s = (c * jax.nn.sigmoid(c)).astype(BF16)
    o_ref[...] = jnp.dot(s, w_ref[...].astype(BF16), preferred_element_type=F32) + b_ref[...]


def ada_mod(cvec, ada_w, ada_b):
    ncol = 6 * D_MODEL
    return pl.pallas_call(
        _ada_kernel,
        grid=(DEPTH, ncol // ADA_TN),
        in_specs=[
            pl.BlockSpec((SUBLANES, D_MODEL), lambda l, j: (0, 0)),
            pl.BlockSpec((None, D_MODEL, ADA_TN), lambda l, j: (l, 0, j)),
            pl.BlockSpec((None, 1, ADA_TN), lambda l, j: (l, 0, j)),
        ],
        out_specs=pl.BlockSpec((None, SUBLANES, ADA_TN), lambda l, j: (l, 0, j)),
        out_shape=jax.ShapeDtypeStruct((DEPTH, SUBLANES, ncol), F32),
        compiler_params=_params(("arbitrary", "arbitrary")),
        name="ada_mod",
    )(cvec, ada_w, ada_b.reshape(DEPTH, 1, ncol))


INPROJ_TM = 256


def _inproj_kernel(x_ref, mod_ref, w_ref, wpe_ref, ohg_ref, org_ref, ocq_ref, okv_ref):
    m = mod_ref[...]
    hm = (x_ref[...] * (1.0 + m[1:2]) + m[0:1]).astype(BF16)
    a = 0
    for o_ref, width in zip((ohg_ref, org_ref, ocq_ref), IN_GROUPS[:3]):
        o_ref[...] = jnp.dot(hm, w_ref[:, a:a + width], preferred_element_type=F32)
        a += width
    okv_ref[:, 0:KV_LORA] = jnp.dot(hm, w_ref[:, a:a + KV_LORA], preferred_element_type=F32)
    okv_ref[:, KV_LORA:] = jnp.dot(hm, wpe_ref[...], preferred_element_type=F32)


def in_proj(x, mod6, w_in_l, w_pe_l, layer):
    tm = INPROJ_TM
    const = lambda a: pl.BlockSpec(a.shape, lambda i: (0, 0), pipeline_mode=pl.Buffered(1))
    return pl.pallas_call(
        _inproj_kernel,
        grid=(N_TOK // tm,),
        in_specs=[
            pl.BlockSpec((tm, D_MODEL), lambda i: (i, 0)),
            pl.BlockSpec((None, 6, D_MODEL), lambda i: (layer * N_MODROWS + _mod_row(i, tm), 0, 0)),
            const(w_in_l), const(w_pe_l),
        ],
        out_specs=[pl.BlockSpec((tm, w), lambda i: (i, 0)) for w in IN_GROUPS],
        out_shape=[jax.ShapeDtypeStruct((N_TOK, w), F32) for w in IN_GROUPS],
        compiler_params=_params(("arbitrary",)),
        name="in_proj",
    )(x, mod6, w_in_l, w_pe_l)


def _rot_cols(w):
    quarter = MLA_ROPE_DIM // 4
    j = np.arange(MLA_ROPE_DIM)
    first = (j % (2 * quarter)) < quarter
    src = np.where(first, j + quarter, j - quarter)
    sign = np.where(first, -1.0, 1.0).astype(np.float32)
    return w[..., src] * sign


def prep_w_pe(w_in_l):
    kpe = w_in_l[:, -MLA_ROPE_DIM:]
    return jnp.concatenate([kpe, _rot_cols(kpe)], axis=1).astype(BF16)


HG_CHUNK = 64
HG_SAFE_EXP = 80.0


def _silu(x):
    return x * jax.nn.sigmoid(x)


HG_TB = SEQ
HG_BLOCKS = N_TOK // HG_TB
HG_CTX_BLOCKS = N_CTX // HG_TB
HG_SEQ_BLOCKS = DEC_SEQ // HG_TB


def _hg_exact_att(q, k, cum):
    c = q.shape[0]
    lane = lax.broadcasted_iota(jnp.int32, (c, c), 1)
    row = lax.broadcasted_iota(jnp.int32, (c, 1), 0)

    def body(s_idx, att):
        sel = row == s_idx
        cum_s = jnp.sum(jnp.where(sel, cum, 0.0), axis=0, keepdims=True)
        k_s = jnp.sum(jnp.where(sel, k, 0.0), axis=0, keepdims=True)
        dec = jnp.exp(jnp.minimum(cum - cum_s, 0.0))
        col = jnp.sum(q * k_s * dec, axis=1, keepdims=True)
        return jnp.where(lane == s_idx, col, att)

    return lax.fori_loop(0, c, body, jnp.zeros((c, c), F32))


def _hgrn_kernel(qf_ref, ff_ref, vf_ref, qb_ref, fb_ref, vb_ref, lbl_ref, s0f_ref, s0b_ref,
                 of_ref, ob_ref, sf_ref, sb_ref, st_scr, *, layer):
    c = HG_CHUNK
    n_c = HG_TB // c
    heads = range(HG_HEADS)
    i = pl.program_id(0)
    blk = (i, HG_BLOCKS - 1 - i)
    is_ctx = tuple(b < HG_CTX_BLOCKS for b in blk)
    pos = tuple(lax.rem(b - HG_CTX_BLOCKS, HG_SEQ_BLOCKS) for b in blk)
    starts = (is_ctx[0] | (pos[0] == 0), is_ctx[1] | (pos[1] == HG_SEQ_BLOCKS - 1))

    for d, s0_ref in enumerate((s0f_ref, s0b_ref)):
        @pl.when(starts[d] & is_ctx[d])
        def _():
            st_scr[d] = jnp.zeros((HG_HEADS, HG_DK, HG_DK), F32)

        @pl.when(starts[d] & jnp.logical_not(is_ctx[d]))
        def _():
            for h in heads:
                st_scr[d, h] = s0_ref[h].T

    lg = lbl_ref[...]
    e = jnp.exp(lg - jnp.max(lg, axis=0, keepdims=True))
    sm = e / jnp.sum(e, axis=0, keepdims=True)
    lb = jnp.zeros_like(sm[0])
    for j in range(1, layer + 1):
        lb = lb + sm[j]

    r = lax.broadcasted_iota(jnp.int32, (c, c), 0)
    s = lax.broadcasted_iota(jnp.int32, (c, c), 1)
    masks = (r >= s, r <= s)
    marks = ((0, c // 2 - 1, c - 1), (c - 1, c // 2, 0))
    in_refs = ((qf_ref, ff_ref, vf_ref), (qb_ref, fb_ref, vb_ref))
    out_refs = (of_ref, ob_ref)

    work = ([], [])
    guard = jnp.float32(0.0)
    for d in range(2):
        q_ref, f_ref, v_ref = in_refs[d]
        lbd = lb[d:d + 1]
        first, mid, last = marks[d]
        for ci in (range(n_c) if d == 0 else reversed(range(n_c))):
            rows = slice(ci * c, (ci + 1) * c)
            f = lbd + (1.0 - lbd) * jax.nn.sigmoid(f_ref[rows, :])
            cum = jnp.dot(masks[d].astype(F32), jnp.log(f), precision=lax.Precision.HIGHEST,
                          preferred_element_type=F32)
            c_mid, c_last = cum[mid:mid + 1], cum[last:last + 1]
            guard = jnp.maximum(guard, jnp.max(jnp.maximum(cum[first:first + 1] - c_mid, c_mid - c_last)))
            work[d].append((rows, _silu(q_ref[rows, :]), 1.0 - f, v_ref[rows, :].astype(BF16), cum))

    nt = (((1,), (1,)), ((), ()))
    tn = (((0,), (0,)), ((), ()))

    def run(exact):
        for d in range(2):
            _, mid, last = marks[d]
            sts = [st_scr[d, h] for h in heads]
            for rows, q, k, v, cum in work[d]:
                c_mid, c_last = cum[mid:mid + 1], cum[last:last + 1]
                q_in = (q * jnp.exp(cum)).astype(BF16)
                k_end = (k * jnp.exp(c_last - cum)).astype(BF16)
                dec = jnp.exp(c_last)
                if not exact:
                    qt = (q * jnp.exp(cum - c_mid)).astype(BF16)
                    kt = (k * jnp.exp(c_mid - cum)).astype(BF16)
                outs = []
                for h in heads:
                    ls = slice(h * LANES, (h + 1) * LANES)
                    inter = lax.dot_general(q_in[:, ls], sts[h].astype(BF16), nt, preferred_element_type=F32)
                    if exact:
                        att = _hg_exact_att(q[:, ls], k[:, ls], cum[:, ls])
                    else:
                        att = lax.dot_general(qt[:, ls], kt[:, ls], nt, preferred_element_type=F32)
                    att = jnp.where(masks[d], att, 0.0).astype(BF16)
                    outs.append(inter + jnp.dot(att, v[:, ls], preferred_element_type=F32))
                    sts[h] = sts[h] * dec[:, ls] + lax.dot_general(v[:, ls], k_end[:, ls], tn,
                                                                   preferred_element_type=F32)
                out_refs[d][rows, :] = jnp.concatenate(outs, axis=1)
            for h in heads:
                st_scr[d, h] = sts[h]

    unsafe = guard > HG_SAFE_EXP
    pl.when(unsafe)(lambda: run(True))
    pl.when(jnp.logical_not(unsafe))(lambda: run(False))

    for d, fin_ref in enumerate((sf_ref, sb_ref)):
        @pl.when(is_ctx[d])
        def _():
            for h in heads:
                fin_ref[h] = st_scr[d, h].T


def hgrn_mixer(z_hg, lb_logits, state, layer):
    last = HG_BLOCKS - 1
    blocks = (lambda i: i, lambda i: last - i)
    lat_batch = lambda b: jnp.clip(lax.div(b - HG_CTX_BLOCKS, HG_SEQ_BLOCKS), 0, DEC_BATCH - 1)
    ctx_seq = lambda b: jnp.minimum(b, BATCH - 1)

    def col(d, group):
        return pl.BlockSpec((HG_TB, HG_WIDTH), lambda i: (blocks[d](i), group))

    def s0(d):
        return pl.BlockSpec((None, None, None, HG_HEADS, HG_DK, HG_DK),
                            lambda i: (lat_batch(blocks[d](i)), layer, d, 0, 0, 0))

    def fin(d):
        return pl.BlockSpec((None, HG_HEADS, HG_DK, HG_DK), lambda i: (ctx_seq(blocks[d](i)), 0, 0, 0))

    return pl.pallas_call(
        functools.partial(_hgrn_kernel, layer=layer),
        grid=(HG_BLOCKS,),
        in_specs=[col(0, 0), col(0, 1), col(0, 3), col(1, 0), col(1, 2), col(1, 3),
                  pl.BlockSpec((DEPTH, 2, HG_WIDTH), lambda i: (0, 0, 0)), s0(0), s0(1)],
        out_specs=[col(0, 0), col(1, 0), fin(0), fin(1)],
        out_shape=[jax.ShapeDtypeStruct((N_TOK, HG_WIDTH), F32), jax.ShapeDtypeStruct((N_TOK, HG_WIDTH), F32),
                   jax.ShapeDtypeStruct((BATCH, HG_HEADS, HG_DK, HG_DK), F32),
                   jax.ShapeDtypeStruct((BATCH, HG_HEADS, HG_DK, HG_DK), F32)],
        scratch_shapes=[pltpu.VMEM((2, HG_HEADS, HG_DK, HG_DK), F32)],
        compiler_params=_params(("arbitrary",)),
        name="hgrn",
    )(z_hg, z_hg, z_hg, z_hg, z_hg, z_hg, lb_logits, state, state)


RG_ROWS = 256
RG_PAD = SUBLANES


def _tile_scan(a, b, reverse):
    row = lax.broadcasted_iota(jnp.int32, a.shape, 0)
    for sh in (1, 2, 4):
        if reverse:
            a_s, b_s = pltpu.roll(a, SUBLANES - sh, 0), pltpu.roll(b, SUBLANES - sh, 0)
            valid = row < SUBLANES - sh
        else:
            a_s, b_s = pltpu.roll(a, sh, 0), pltpu.roll(b, sh, 0)
            valid = row >= sh
        b = jnp.where(valid, a * b_s + b, b)
        a = jnp.where(valid, a * a_s, a)
    return a, b


def _rglru_kernel(*refs, seq_len, has_state):
    if has_state:
        (x_ref, y_ref, cw_ref, cb_ref, wr_ref, br_ref, wi_ref, bi_ref, lam_ref, h0_ref,
         o_ref, xp_scr, a_scr, b_scr) = refs
    else:
        (x_ref, y_ref, cw_ref, cb_ref, wr_ref, br_ref, wi_ref, bi_ref, lam_ref,
         o_ref, hfin_ref, xp_scr, a_scr, b_scr) = refs
    t = seq_len
    zeros = jnp.zeros((RG_PAD, LANES), F32)
    xp_scr[0:RG_PAD, :] = zeros
    xp_scr[RG_PAD + t:, :] = zeros
    xp_scr[RG_PAD:RG_PAD + t, :] = x_ref[...]

    cw = cw_ref[...]
    cb = cb_ref[...]
    lam = lam_ref[...]
    sp = jnp.maximum(-lam, 0.0) + jnp.log1p(jnp.exp(-jnp.abs(lam)))

    def gates(ci, _):
        r0 = pl.multiple_of(ci * RG_ROWS, RG_ROWS)
        xc = cb
        for j in range(RG_CONV):
            xc = xc + cw[j:j + 1] * xp_scr[pl.ds(r0 + RG_PAD - RG_CONV // 2 + j, RG_ROWS), :]
        rows = pl.ds(r0, RG_ROWS)
        xcb = xc.astype(BF16)
        for d in range(2):
            r = jax.nn.sigmoid(jnp.dot(xcb, wr_ref[d], preferred_element_type=F32) + br_ref[d])
            ig = jax.nn.sigmoid(jnp.dot(xcb, wi_ref[d], preferred_element_type=F32) + bi_ref[d])
            log_a = -RG_C * r * sp[d:d + 1]
            a_scr[d, rows, :] = jnp.exp(log_a)
            b_scr[d, rows, :] = jnp.sqrt(1.0 - jnp.exp(2.0 * log_a)) * (ig * xc)
        return 0

    lax.fori_loop(0, t // RG_ROWS, gates, 0)

    n_tiles = t // SUBLANES

    def scan(j, carry):
        h_f, h_b = carry
        rows_f = pl.ds(pl.multiple_of(j * SUBLANES, SUBLANES), SUBLANES)
        rows_b = pl.ds(pl.multiple_of((n_tiles - 1 - j) * SUBLANES, SUBLANES), SUBLANES)
        aa, bb = _tile_scan(a_scr[0, rows_f, :], b_scr[0, rows_f, :], False)
        hf = aa * h_f + bb
        b_scr[0, rows_f, :] = hf
        aa, bb = _tile_scan(a_scr[1, rows_b, :], b_scr[1, rows_b, :], True)
        hb = aa * h_b + bb
        b_scr[1, rows_b, :] = hb
        return hf[SUBLANES - 1:SUBLANES], hb[0:1]

    if has_state:
        h0 = h0_ref[...]
        init = (h0[0:1], h0[1:2])
    else:
        init = (jnp.zeros((1, LANES), F32), jnp.zeros((1, LANES), F32))
    h_f, h_b = lax.fori_loop(0, n_tiles, scan, init)
    if not has_state:
        hfin_ref[0:1, :] = h_f
        hfin_ref[1:2, :] = h_b
    o_ref[...] = (b_scr[0] + b_scr[1]) * jax.nn.gelu(y_ref[...], approximate=True)


def rglru_mixer(z_rg, conv_w, conv_b, w_r, b_r, w_i, b_i, lam, state, layer, latent):
    seq_len = DEC_SEQ if latent else SEQ
    n_seq = DEC_BATCH if latent else BATCH
    blk0 = N_CTX // seq_len if latent else 0
    h = RG_HEADS
    vec = lambda rows: pl.BlockSpec((None, rows, LANES), lambda b, hh: (layer, 0, hh))
    wspec = pl.BlockSpec((None, 2, None, RG_BLOCK, RG_BLOCK), lambda b, hh: (layer, 0, hh, 0, 0))
    bspec = pl.BlockSpec((None, 2, 1, LANES), lambda b, hh: (layer, 0, 0, hh))
    in_specs = [
        pl.BlockSpec((seq_len, LANES), lambda b, hh: (blk0 + b, hh)),
        pl.BlockSpec((seq_len, LANES), lambda b, hh: (blk0 + b, h + hh)),
        vec(RG_CONV), vec(1), wspec, bspec, wspec, bspec, vec(2),
    ]
    args = [z_rg, z_rg, conv_w, conv_b, w_r, b_r.reshape(DEPTH, 2, 1, RG_WIDTH), w_i,
            b_i.reshape(DEPTH, 2, 1, RG_WIDTH), lam]
    o_spec = pl.BlockSpec((seq_len, LANES), lambda b, hh: (b, hh))
    o_shape = jax.ShapeDtypeStruct((n_seq * seq_len, RG_WIDTH), F32)
    if latent:
        in_specs.append(pl.BlockSpec((None, None, 2, LANES), lambda b, hh: (b, layer, 0, hh)))
        args.append(state)
        out_specs, out_shape = [o_spec], [o_shape]
    else:
        out_specs = [o_spec, pl.BlockSpec((None, 2, LANES), lambda b, hh: (b, 0, hh))]
        out_shape = [o_shape, jax.ShapeDtypeStruct((BATCH, 2, RG_WIDTH), F32)]
    return pl.pallas_call(
        functools.partial(_rglru_kernel, seq_len=seq_len, has_state=latent),
        grid=(n_seq, h),
        in_specs=in_specs,
        out_specs=out_specs,
        out_shape=out_shape,
        scratch_shapes=[pltpu.VMEM((seq_len + 2 * RG_PAD, LANES), F32),
                        pltpu.VMEM((2, seq_len, LANES), F32), pltpu.VMEM((2, seq_len, LANES), F32)],
        compiler_params=_params(("arbitrary", "arbitrary")),
        name="rglru_lat" if latent else "rglru_ctx",
    )(*args)


MLA_HEAD_PAD = 2 * LANES
MLA_TM = 512
ATT_TQ = 256


def rope_tables():
    half = MLA_ROPE_DIM // 2
    t = np.arange(DEC_SEQ)
    row = (t // GRID_W).astype(np.float32)
    col = (t % GRID_W).astype(np.float32)
    inv = (ROPE_THETA ** (-np.arange(0, half, 2, dtype=np.float32) / half)).astype(np.float32)
    ar, ac = row[:, None] * inv, col[:, None] * inv
    cos = np.concatenate([np.cos(ar), np.cos(ar), np.cos(ac), np.cos(ac)], -1)
    sin = np.concatenate([np.sin(ar), np.sin(ar), np.sin(ac), np.sin(ac)], -1)
    pad = np.zeros((DEC_SEQ, LANES - MLA_ROPE_DIM), np.float32)
    cos_lat = np.tile(np.concatenate([cos, pad], -1), (DEC_BATCH, 1))
    sin_lat = np.tile(np.concatenate([sin, pad], -1), (DEC_BATCH, 1))
    cos_ctx = np.concatenate([np.ones((N_CTX, MLA_ROPE_DIM), np.float32), np.zeros((N_CTX, LANES - MLA_ROPE_DIM), np.float32)], -1)
    sin_ctx = np.zeros((N_CTX, LANES), np.float32)
    return (jnp.asarray(np.concatenate([cos_ctx, cos_lat], 0).astype(np.float32)),
            jnp.asarray(np.concatenate([sin_ctx, sin_lat], 0).astype(np.float32)))


def _rope_group(x, cos, sin):
    return x * cos + pltpu.roll(x, MLA_ROPE_DIM, 1) * sin


def _rms(x, g):
    return x * lax.rsqrt(jnp.mean(x * x, axis=-1, keepdims=True) + RMS_EPS) * g


def _qproj_kernel(cq_ref, g_ref, w_ref, cos_ref, sin_ref, q_ref):
    xn = _rms(cq_ref[...], g_ref[...]).astype(BF16)
    qm = jnp.dot(xn, w_ref[...], preferred_element_type=F32)
    cos, sin = cos_ref[...], sin_ref[...]
    for h in range(MLA_HEADS):
        a = h * MLA_HEAD_PAD
        q_ref[:, a:a + LANES] = qm[:, a:a + LANES].astype(BF16)
        q_ref[:, a + LANES:a + 2 * LANES] = _rope_group(qm[:, a + LANES:a + 2 * LANES], cos, sin).astype(BF16)


def q_proj(z_cq, q_norm_g, w_uq_l, cos_t, sin_t, layer):
    tm = MLA_TM
    width = MLA_HEADS * MLA_HEAD_PAD
    return pl.pallas_call(
        _qproj_kernel,
        grid=(N_TOK // tm,),
        in_specs=[
            pl.BlockSpec((tm, Q_LORA), lambda i: (i, 0)),
            pl.BlockSpec((None, 1, Q_LORA), lambda i: (layer, 0, 0)),
            pl.BlockSpec((Q_LORA, width), lambda i: (0, 0)),
            pl.BlockSpec((tm, LANES), lambda i: (i, 0)),
            pl.BlockSpec((tm, LANES), lambda i: (i, 0)),
        ],
        out_specs=pl.BlockSpec((tm, width), lambda i: (i, 0)),
        out_shape=jax.ShapeDtypeStruct((N_TOK, width), BF16),
        compiler_params=_params(("arbitrary",)),
        name="q_proj",
    )(z_cq, q_norm_g, w_uq_l, cos_t, sin_t)


def prep_w_uq(w_uq_l):
    w = w_uq_l.reshape(Q_LORA, MLA_HEADS, MLA_QK_DIM)
    pe = w[..., MLA_NOPE_DIM:]
    w = jnp.concatenate([w, _rot_cols(pe)], axis=-1)
    return w.reshape(Q_LORA, MLA_HEADS * MLA_HEAD_PAD).astype(BF16)


def _kvproj_kernel(zkv_ref, g_ref, wk_ref, wv_ref, cos_ref, sin_ref, *outs, normalize):
    if normalize:
        k_ref, v_ref, ckv_ref, kpe_ref = outs
    else:
        k_ref, v_ref = outs
    ckv = zkv_ref[:, 0:KV_LORA]
    if normalize:
        ckv = _rms(ckv, g_ref[...])
        ckv_ref[...] = ckv
    pe_group = zkv_ref[:, KV_LORA:KV_LORA + LANES]
    if normalize:
        kpe_ref[...] = pe_group[:, 0:MLA_ROPE_DIM]
    pe = _rope_group(pe_group, cos_ref[...], sin_ref[...]).astype(BF16)
    cb = ckv.astype(BF16)
    kn = jnp.dot(cb, wk_ref[...], preferred_element_type=F32)
    v_ref[...] = jnp.dot(cb, wv_ref[...], preferred_element_type=F32).astype(BF16)
    for h in range(MLA_HEADS):
        a = h * MLA_HEAD_PAD
        k_ref[:, a:a + LANES] = kn[:, h * LANES:(h + 1) * LANES].astype(BF16)
        k_ref[:, a + LANES:a + 2 * LANES] = pe


def kv_proj(z_kv, kv_norm_g, w_uk_l, w_uv_l, cos_t, sin_t, layer):
    tm = MLA_TM
    kw, vw = MLA_HEADS * MLA_HEAD_PAD, MLA_HEADS * MLA_V_DIM
    return pl.pallas_call(
        functools.partial(_kvproj_kernel, normalize=True),
        grid=(N_TOK // tm,),
        in_specs=[
            pl.BlockSpec((tm, W_KV), lambda i: (i, 0)),
            pl.BlockSpec((None, 1, KV_LORA), lambda i: (layer, 0, 0)),
            pl.BlockSpec((KV_LORA, MLA_HEADS * MLA_NOPE_DIM), lambda i: (0, 0)),
            pl.BlockSpec((KV_LORA, vw), lambda i: (0, 0)),
            pl.BlockSpec((tm, LANES), lambda i: (i, 0)),
            pl.BlockSpec((tm, LANES), lambda i: (i, 0)),
        ],
        out_specs=[
            pl.BlockSpec((tm, kw), lambda i: (i, 0)),
            pl.BlockSpec((tm, vw), lambda i: (i, 0)),
            pl.BlockSpec((tm, KV_LORA), lambda i: (i, 0)),
            pl.BlockSpec((tm, MLA_ROPE_DIM), lambda i: (i, 0)),
        ],
        out_shape=[
            jax.ShapeDtypeStruct((N_TOK, kw), BF16),
            jax.ShapeDtypeStruct((N_TOK, vw), BF16),
            jax.ShapeDtypeStruct((N_TOK, KV_LORA), F32),
            jax.ShapeDtypeStruct((N_TOK, MLA_ROPE_DIM), F32),
        ],
        compiler_params=_params(("arbitrary",)),
        name="kv_proj",
    )(z_kv, kv_norm_g, w_uk_l, w_uv_l, cos_t, sin_t)


def kv_proj_cache(cache_kv, kv_norm_g, w_uk_l, w_uv_l, cos_c, sin_c, layer):
    tm = PAST_LEN
    kw, vw = MLA_HEADS * MLA_HEAD_PAD, MLA_HEADS * MLA_V_DIM
    n = DEC_BATCH * PAST_LEN
    return pl.pallas_call(
        functools.partial(_kvproj_kernel, normalize=False),
        grid=(DEC_BATCH,),
        in_specs=[
            pl.BlockSpec((tm, W_KV), lambda b: (b, 0)),
            pl.BlockSpec((None, 1, KV_LORA), lambda b: (layer, 0, 0)),
            pl.BlockSpec((KV_LORA, MLA_HEADS * MLA_NOPE_DIM), lambda b: (0, 0)),
            pl.BlockSpec((KV_LORA, vw), lambda b: (0, 0)),
            pl.BlockSpec((tm, LANES), lambda b: (0, 0)),
            pl.BlockSpec((tm, LANES), lambda b: (0, 0)),
        ],
        out_specs=[pl.BlockSpec((tm, kw), lambda b: (b, 0)), pl.BlockSpec((tm, vw), lambda b: (b, 0))],
        out_shape=[jax.ShapeDtypeStruct((n, kw), BF16), jax.ShapeDtypeStruct((n, vw), BF16)],
        compiler_params=_params(("arbitrary",)),
        name="kv_proj_cache",
    )(cache_kv, kv_norm_g, w_uk_l, w_uv_l, cos_c, sin_c)


def _attn_kernel(q_ref, *refs):
    o_ref = refs[-1]
    segs = [(refs[i], refs[i + 1]) for i in range(0, len(refs) - 1, 2)]
    nt = (((1,), (1,)), ((), ()))
    q = q_ref[...]
    scores = [lax.dot_general(q, k_ref[...], nt, preferred_element_type=F32) for k_ref, _ in segs]
    m = scores[0].max(axis=-1, keepdims=True)
    for s in scores[1:]:
        m = jnp.maximum(m, s.max(axis=-1, keepdims=True))
    l, o = 0.0, 0.0
    for s, (_, v_ref) in zip(scores, segs):
        p = jnp.exp2((s - m) * (MLA_QK_DIM ** -0.5 * math.log2(math.e)))
        l = l + jnp.sum(p, axis=-1, keepdims=True)
        o = o + jnp.dot(p.astype(BF16), v_ref[...], preferred_element_type=F32)
    o_ref[...] = o / l


def mla_attention(q, k_tok, v_tok, k_cache, v_cache, latent):
    h = MLA_HEADS
    if latent:
        tq = ATT_TQ
        n_q = DEC_SEQ // tq
        grid = (DEC_BATCH, h, n_q)
        q_map = lambda b, hh, i: (N_CTX // tq + b * n_q + i, hh)
        kv_map = lambda b, hh, i: (N_CTX // DEC_SEQ + b, hh)
        o_map = lambda b, hh, i: (b * n_q + i, hh)
        in_specs = [
            pl.BlockSpec((tq, MLA_HEAD_PAD), q_map),
            pl.BlockSpec((DEC_SEQ, MLA_HEAD_PAD), kv_map),
            pl.BlockSpec((DEC_SEQ, MLA_V_DIM), kv_map),
            pl.BlockSpec((PAST_LEN, MLA_HEAD_PAD), lambda b, hh, i: (b, hh)),
            pl.BlockSpec((PAST_LEN, MLA_V_DIM), lambda b, hh, i: (b, hh)),
        ]
        args = [q, k_tok, v_tok, k_cache, v_cache]
        n_out = N_LAT
    else:
        tq = SEQ
        grid = (BATCH, h, 1)
        q_map = o_map = lambda b, hh, i: (b, hh)
        in_specs = [
            pl.BlockSpec((tq, MLA_HEAD_PAD), q_map),
            pl.BlockSpec((SEQ, MLA_HEAD_PAD), q_map),
            pl.BlockSpec((SEQ, MLA_V_DIM), q_map),
        ]
        args = [q, k_tok, v_tok]
        n_out = N_CTX
    return pl.pallas_call(
        _attn_kernel,
        grid=grid,
        in_specs=in_specs,
        out_specs=pl.BlockSpec((tq, MLA_V_DIM), o_map),
        out_shape=jax.ShapeDtypeStruct((n_out, MLA_WIDTH), F32),
        compiler_params=_params(("arbitrary",) * 3),
        name="mla_attn_lat" if latent else "mla_attn_ctx",
    )(*args)


OUT_TM = 256


def _layer_norm(y, g, b):
    mu = jnp.mean(y, axis=-1, keepdims=True)
    yc = y - mu
    var = jnp.mean(yc * yc, axis=-1, keepdims=True)
    return yc * lax.rsqrt(var + LN_EPS) * g + b


def _outproj_kernel(of_ref, ob_ref, hgg_ref, hgn_ref, org_ref, omla_ref, w_ref, x_ref, mod_ref, g_ref, b_ref, wr_ref,
                    x1_ref, hf_ref, pt_ref):
    o = of_ref[...] + ob_ref[...]
    heads = [o[:, h * HG_DK:(h + 1) * HG_DK] for h in range(HG_HEADS)]
    o = jnp.concatenate([oh * lax.rsqrt(jnp.mean(oh * oh, axis=-1, keepdims=True) + RMS_EPS) for oh in heads], axis=1)
    o_hg = o * hgn_ref[...] * _silu(hgg_ref[...])
    m = jnp.dot(o_hg.astype(BF16), w_ref[0:HG_WIDTH, :], preferred_element_type=F32)
    m += jnp.dot(org_ref[...].astype(BF16), w_ref[HG_WIDTH:HG_WIDTH + RG_WIDTH, :], preferred_element_type=F32)
    m += jnp.dot(omla_ref[...].astype(BF16), w_ref[HG_WIDTH + RG_WIDTH:, :], preferred_element_type=F32)
    md = mod_ref[...]
    x1 = _layer_norm(DN_ALPHA * x_ref[...] + md[2:3] * m, g_ref[...], b_ref[...])
    x1_ref[...] = x1
    hf = x1 * (1.0 + md[4:5]) + md[3:4]
    hf_ref[...] = hf
    logits = jnp.dot(hf.astype(BF16), wr_ref[...], preferred_element_type=F32)
    lane = lax.broadcasted_iota(jnp.int32, logits.shape, 1)
    logits = jnp.where(lane < N_EXPERTS, logits, -jnp.inf)
    e = jnp.exp(logits - jnp.max(logits, axis=-1, keepdims=True))
    p = e / jnp.sum(e, axis=-1, keepdims=True)
    pt_ref[...] = p.T[0:N_EXPERTS, :]


def out_proj(o_hg_f, o_hg_b, z_hg, hg_norm_g, o_rg, o_mla, w_out_l, x, mod6, ln_g, ln_b, w_router_l, layer):
    tm = OUT_TM
    row = lambda w: pl.BlockSpec((tm, w), lambda i: (i, 0))
    const = lambda shape: pl.BlockSpec(shape, lambda i: (0,) * len(shape), pipeline_mode=pl.Buffered(1))
    return pl.pallas_call(
        _outproj_kernel,
        grid=(N_TOK // tm,),
        in_specs=[
            row(HG_WIDTH), row(HG_WIDTH),
            pl.BlockSpec((tm, HG_WIDTH), lambda i: (i, 4)),
            pl.BlockSpec((None, 1, HG_WIDTH), lambda i: (layer, 0, 0)),
            row(RG_WIDTH), row(MLA_WIDTH),
            const((D_MODEL, D_MODEL)),
            row(D_MODEL),
            pl.BlockSpec((None, 6, D_MODEL), lambda i: (layer * N_MODROWS + _mod_row(i, tm), 0, 0)),
            pl.BlockSpec((None, 1, D_MODEL), lambda i: (layer, 0, 0)),
            pl.BlockSpec((None, 1, D_MODEL), lambda i: (layer, 0, 0)),
            const((D_MODEL, LANES)),
        ],
        out_specs=[row(D_MODEL), row(D_MODEL), pl.BlockSpec((N_EXPERTS, tm), lambda i: (0, i))],
        out_shape=[jax.ShapeDtypeStruct((N_TOK, D_MODEL), F32),
                   jax.ShapeDtypeStruct((N_TOK, D_MODEL), F32),
                   jax.ShapeDtypeStruct((N_EXPERTS, N_TOK), F32)],
        compiler_params=_params(("arbitrary",)),
        name="out_proj",
    )(o_hg_f, o_hg_b, z_hg, hg_norm_g, o_rg, o_mla, w_out_l, x, mod6, ln_g, ln_b, w_router_l)


CAP_CTX = CAP_FACTOR * N_CTX // N_EXPERTS
CAP_LAT = CAP_FACTOR * N_LAT // N_EXPERTS
ROUTE_SETS = ((0, N_CTX, CAP_CTX), (N_CTX, N_LAT, CAP_LAT))
SLOTS = CAP_CTX + CAP_LAT


def _route_thr_kernel(pt_ref, thr_ref, need_ref):
    for si, (start, n, cap) in enumerate(ROUTE_SETS):
        p = pt_ref[:, start:start + n]

        def count(mask):
            return jnp.sum(mask.astype(F32), axis=1, keepdims=True)

        def body(i, t):
            cand = t | jnp.left_shift(jnp.int32(1), 30 - i)
            return jnp.where(count(p >= pltpu.bitcast(cand, F32)) >= cap, cand, t)

        t = pltpu.bitcast(lax.fori_loop(0, 31, body, jnp.zeros((N_EXPERTS, 1), jnp.int32)), F32)
        need = cap - count(p > t)
        thr_ref[si] = jnp.broadcast_to(t, (N_EXPERTS, LANES))
        need_ref[si] = jnp.broadcast_to(need, (N_EXPERTS, LANES))


def route_threshold(p_t):
    n_sets = len(ROUTE_SETS)
    return pl.pallas_call(
        _route_thr_kernel,
        out_shape=[jax.ShapeDtypeStruct((n_sets, N_EXPERTS, LANES), F32),
                   jax.ShapeDtypeStruct((n_sets, N_EXPERTS, LANES), F32)],
        compiler_params=_params(None),
        name="route_threshold",
    )(p_t)


def _route_lists_kernel(p_ref, thr_ref, need_ref, lists_ref, cnt_ref, first_ref, sel_ref, *, nb, cap, tok0, row0):
    nt = (((1,), (1,)), ((), ()))
    r128 = lax.broadcasted_iota(jnp.int32, (LANES, LANES), 0)
    c128 = lax.broadcasted_iota(jnp.int32, (LANES, LANES), 1)
    incl = (r128 <= c128).astype(BF16)
    eye = r128 == c128
    rb = lax.broadcasted_iota(jnp.int32, (nb, nb), 0)
    cb = lax.broadcasted_iota(jnp.int32, (nb, nb), 1)
    below = (cb < rb).astype(BF16)
    incl_b = (rb <= cb).astype(BF16)
    ones8 = jnp.ones((SUBLANES, LANES), BF16)
    s_col = lax.broadcasted_iota(jnp.int32, (cap, 1), 0).astype(F32)
    lane = lax.broadcasted_iota(jnp.int32, (cap, LANES), 1)
    lane_f = lane.astype(F32)
    j_row = lax.broadcasted_iota(jnp.int32, (1, nb), 1).astype(F32)

    def dot(a, b):
        return jnp.dot(a, b, preferred_element_type=F32)

    def block_base(totals, unit):
        hi = jnp.floor(totals * (1.0 / unit))
        lo = totals - unit * hi
        bc = lambda a: jnp.broadcast_to(a, (nb, LANES)).astype(BF16)
        return unit * dot(below, bc(hi)) + dot(below, bc(lo))

    def choose(e, acc):
        p = p_ref[e]
        t = thr_ref[pl.ds(e, 1), :]
        need = need_ref[pl.ds(e, 1), :]
        eq = p == t
        eq_f = eq.astype(F32)
        eq_lp = dot(eq_f.astype(BF16), incl)
        tie_rank = eq_lp - eq_f + block_base(eq_lp[:, LANES - 1:LANES], 16.0)
        sel_f = jnp.where((p > t) | (eq & (tie_rank < need)), 1.0, 0.0)
        sel_ref[e] = sel_f
        return acc + sel_f

    cnt = lax.fori_loop(0, N_EXPERTS, choose, jnp.zeros((nb, LANES), F32))
    cnt_lp = dot(cnt.astype(BF16), incl)
    first = cnt_lp - cnt + block_base(cnt_lp[:, LANES - 1:LANES], 64.0) + row0
    first_hi = jnp.floor(first * (1.0 / LANES))
    first_lo = first - LANES * first_hi

    def expert(e, acc):
        p = p_ref[e]
        sel_f = sel_ref[e]
        sel_b = sel_f.astype(BF16)
        lp = dot(sel_b, incl)
        c_row = lax.dot_general(ones8, sel_b, nt, preferred_element_type=F32)
        incl_row = dot(c_row.astype(BF16), incl_b)[0:1]
        excl_row = incl_row - c_row[0:1]
        oh_j = jnp.where((s_col >= excl_row) & (s_col < incl_row), 1.0, 0.0)
        base_s = jnp.sum(oh_j * excl_row, axis=1, keepdims=True)
        j_s = jnp.sum(oh_j * j_row, axis=1, keepdims=True)
        oh_jb = oh_j.astype(BF16)
        lp_rows = dot(oh_jb, lp.astype(BF16))
        pos = jnp.sum(jnp.where(lp_rows <= s_col - base_s, 1.0, 0.0), axis=1, keepdims=True)
        oh_c = lane_f == pos
        p1 = p.astype(BF16)
        r1 = p - p1.astype(F32)
        p2 = r1.astype(BF16)
        p3 = (r1 - p2.astype(F32)).astype(BF16)
        p_rows = dot(oh_jb, p1) + dot(oh_jb, p2) + dot(oh_jb, p3)
        gate = jnp.sum(jnp.where(oh_c, p_rows, 0.0), axis=1, keepdims=True)
        rank_rows = dot(oh_jb, acc.astype(BF16))
        first_rows = LANES * dot(oh_jb, first_hi.astype(BF16)) + dot(oh_jb, first_lo.astype(BF16))
        dst = jnp.sum(jnp.where(oh_c, rank_rows + first_rows, 0.0), axis=1, keepdims=True)
        idx = tok0 + LANES * j_s + pos
        lists_ref[e] = jnp.where(lane == 0, idx, jnp.where(lane == 1, dst, jnp.where(lane == 2, gate, 0.0)))
        return acc + sel_f

    lax.fori_loop(0, N_EXPERTS, expert, jnp.zeros((nb, LANES), F32))

    ones_b = jnp.ones((LANES, LANES), BF16)

    def column(a, j):
        diag = jnp.where(eye, jnp.broadcast_to(a[j:j + 1, :], (LANES, LANES)), 0.0)
        return dot(diag.astype(BF16), ones_b)

    for j in range(nb):
        rows = slice(j * LANES, (j + 1) * LANES)
        cnt_ref[rows, :] = column(cnt, j)
        first_ref[rows, :] = LANES * column(first_hi, j) + column(first_lo, j)


def route_lists(p_blk, thr, need, set_index):
    tok0, n, cap = ROUTE_SETS[set_index]
    nb = n // LANES
    row0 = float(CAP_FACTOR * tok0)
    kern = functools.partial(_route_lists_kernel, nb=nb, cap=cap, tok0=tok0, row0=row0)
    return pl.pallas_call(
        kern,
        grid=(1,),
        in_specs=[
            pl.BlockSpec((N_EXPERTS, nb, LANES), lambda i: (0, 0, 0)),
            pl.BlockSpec((None, N_EXPERTS, LANES), lambda i: (set_index, 0, 0)),
            pl.BlockSpec((None, N_EXPERTS, LANES), lambda i: (set_index, 0, 0)),
        ],
        out_specs=[pl.BlockSpec((N_EXPERTS, cap, LANES), lambda i: (0, 0, 0)),
                   pl.BlockSpec((n, LANES), lambda i: (0, 0)),
                   pl.BlockSpec((n, LANES), lambda i: (0, 0))],
        out_shape=[jax.ShapeDtypeStruct((N_EXPERTS, cap, LANES), F32),
                   jax.ShapeDtypeStruct((n, LANES), F32),
                   jax.ShapeDtypeStruct((n, LANES), F32)],
        scratch_shapes=[pltpu.VMEM((N_EXPERTS, nb, LANES), F32)],
        compiler_params=_params(("arbitrary",)),
        name="route_lists_lat" if set_index else "route_lists_ctx",
    )(p_blk, thr, need)


FFN_TM = 512
FFN_FC = 256
FFN_ISSUE_UNROLL = 8


def _ffn_kernel(tok_ref, dst_ref, hf_ref, lists_ref, wg_ref, wu_ref, wd_ref, yc_ref,
                wg_b, wu_b, wd_b, x_buf, y_buf, sem_in, sem_out):
    e, f = pl.program_id(0), pl.program_id(1)
    n_e, n_f = pl.num_programs(0), pl.num_programs(1)
    tm, fc = FFN_TM, FFN_FC
    n_tiles = SLOTS // tm

    def gather(r):
        slot = r % 2

        def issue(i, _):
            pltpu.make_async_copy(hf_ref.at[pl.ds(tok_ref[0, 0, r * tm + i], 1), :],
                                  x_buf.at[slot, pl.ds(i, 1), :], sem_in.at[slot]).start()
            return 0

        lax.fori_loop(0, tm, issue, 0, unroll=FFN_ISSUE_UNROLL)

    def scatter(r):
        slot = r % 2

        def issue(i, _):
            pltpu.make_async_copy(y_buf.at[slot, pl.ds(i, 1), :],
                                  yc_ref.at[pl.ds(dst_ref[0, 0, r * tm + i], 1), :], sem_out.at[slot]).start()
            return 0

        lax.fori_loop(0, tm, issue, 0, unroll=FFN_ISSUE_UNROLL)

    def wait_gather(slot):
        pltpu.make_async_copy(hf_ref.at[pl.ds(0, tm), :], x_buf.at[slot], sem_in.at[slot]).wait()

    def wait_scatter(slot):
        pltpu.make_async_copy(y_buf.at[slot], yc_ref.at[pl.ds(0, tm), :], sem_out.at[slot]).wait()

    @pl.when(f == 0)
    def _():
        gather(0)

    cols = pl.ds(pl.multiple_of(f * fc, fc), fc)
    wg_b[:, cols] = wg_ref[...].astype(BF16)
    wu_b[:, cols] = wu_ref[...].astype(BF16)
    wd_b[cols, :] = wd_ref[...].astype(BF16)

    @pl.when(f == n_f - 1)
    def _():
        for r in range(n_tiles):
            slot = r % 2
            if r + 1 < n_tiles:
                gather(r + 1)
            wait_gather(slot)
            x = x_buf[slot].astype(BF16)
            gate = jnp.dot(x, wg_b[...], preferred_element_type=F32)
            up = jnp.dot(x, wu_b[...], preferred_element_type=F32)
            y = jnp.dot((_silu(gate) * up).astype(BF16), wd_b[...], preferred_element_type=F32)
            if r >= 2:
                wait_scatter(slot)
            else:
                pl.when(e > 0)(functools.partial(wait_scatter, slot))
            y_buf[slot] = y * lists_ref[r * tm:(r + 1) * tm, 2:3]
            scatter(r)

        @pl.when(e == n_e - 1)
        def _():
            wait_scatter((n_tiles - 2) % 2)
            wait_scatter((n_tiles - 1) % 2)


def expert_ffn(tok, dst, hf, lists, w_gate, w_up, w_down, layer):
    fc = FFN_FC
    idx_spec = pl.BlockSpec((1, 1, SLOTS), lambda e, f: (e, 0, 0), memory_space=pltpu.SMEM)
    return pl.pallas_call(
        _ffn_kernel,
        grid=(N_EXPERTS, D_FF_EXPERT // fc),
        in_specs=[
            idx_spec, idx_spec,
            pl.BlockSpec(memory_space=pl.ANY),
            pl.BlockSpec((None, SLOTS, LANES), lambda e, f: (e, 0, 0)),
            pl.BlockSpec((None, None, D_MODEL, fc), lambda e, f: (layer, e, 0, f)),
            pl.BlockSpec((None, None, D_MODEL, fc), lambda e, f: (layer, e, 0, f)),
            pl.BlockSpec((None, None, fc, D_MODEL), lambda e, f: (layer, e, f, 0)),
        ],
        out_specs=pl.BlockSpec(memory_space=pl.ANY),
        out_shape=jax.ShapeDtypeStruct((N_CHOICES, D_MODEL), F32),
        scratch_shapes=[pltpu.VMEM((D_MODEL, D_FF_EXPERT), BF16), pltpu.VMEM((D_MODEL, D_FF_EXPERT), BF16),
                        pltpu.VMEM((D_FF_EXPERT, D_MODEL), BF16),
                        pltpu.VMEM((2, FFN_TM, D_MODEL), F32), pltpu.VMEM((2, FFN_TM, D_MODEL), F32),
                        pltpu.SemaphoreType.DMA((2,)), pltpu.SemaphoreType.DMA((2,))],
        compiler_params=_params(("arbitrary", "arbitrary")),
        name="expert_ffn",
    )(tok.reshape(N_EXPERTS, 1, SLOTS), dst.reshape(N_EXPERTS, 1, SLOTS), hf, lists, w_gate, w_up, w_down)


COMB_TM = 512
COMB_ROWS = 256
N_CHOICES = N_EXPERTS * SLOTS


def _combine_kernel(rows_ref, y_ref, cnt_ref, first_ref, x1_ref, mod_ref, g_ref, b_ref, o_ref, buf, sem, acc_ref):
    i = pl.program_id(0)
    lc = COMB_ROWS
    r0, r1 = rows_ref[i], rows_ref[i + 1]
    base = lax.div(r0, lc) * lc
    n_chunks = lax.div(r1 - base + lc - 1, lc)

    def chunk_start(k):
        return pl.multiple_of(jnp.minimum(base + k * lc, N_CHOICES - lc), lc)

    def copy(k, slot):
        return pltpu.make_async_copy(y_ref.at[pl.ds(chunk_start(k), lc), :], buf.at[slot], sem.at[slot])

    acc_ref[...] = jnp.zeros_like(acc_ref)

    @pl.when(n_chunks > 0)
    def _():
        copy(0, 0).start()

    first = first_ref[:, 0:1]
    last = first + cnt_ref[:, 0:1]
    col = lax.broadcasted_iota(jnp.int32, (1, lc), 1)

    def body(k, _):
        slot = lax.rem(k, 2)
        copy(k, slot).wait()

        @pl.when(k + 1 < n_chunks)
        def _():
            copy(k + 1, 1 - slot).start()

        row_id = (chunk_start(k) + col).astype(F32)
        mine = (row_id >= first) & (row_id < last) & (row_id >= (base + k * lc).astype(F32))
        onehot = jnp.where(mine, 1.0, 0.0).astype(BF16)
        y = buf[slot]
        hi = y.astype(BF16)
        lo = (y - hi.astype(F32)).astype(BF16)
        acc_ref[...] += (jnp.dot(onehot, hi, preferred_element_type=F32)
                         + jnp.dot(onehot, lo, preferred_element_type=F32))
        return 0

    lax.fori_loop(0, n_chunks, body, 0)
    g2 = mod_ref[...][5:6]
    o_ref[...] = _layer_norm(DN_ALPHA * x1_ref[...] + g2 * acc_ref[...], g_ref[...], b_ref[...])


def moe_combine(tile_rows, y_choices, cnt, first, x1, mod6, ln_g, ln_b, layer):
    tm = COMB_TM
    row = lambda w: pl.BlockSpec((tm, w), lambda i, t: (i, 0))
    grid_spec = pltpu.PrefetchScalarGridSpec(
        num_scalar_prefetch=1,
        grid=(N_TOK // tm,),
        in_specs=[
            pl.BlockSpec(memory_space=pl.ANY),
            row(LANES), row(LANES), row(D_MODEL),
            pl.BlockSpec((None, 6, D_MODEL), lambda i, t: (layer * N_MODROWS + _mod_row(i, tm), 0, 0)),
            pl.BlockSpec((None, 1, D_MODEL), lambda i, t: (layer, 0, 0)),
            pl.BlockSpec((None, 1, D_MODEL), lambda i, t: (layer, 0, 0)),
        ],
        out_specs=row(D_MODEL),
        scratch_shapes=[pltpu.VMEM((2, COMB_ROWS, D_MODEL), F32), pltpu.SemaphoreType.DMA((2,)),
                        pltpu.VMEM((tm, D_MODEL), F32)],
    )
    return pl.pallas_call(
        _combine_kernel,
        grid_spec=grid_spec,
        out_shape=jax.ShapeDtypeStruct((N_TOK, D_MODEL), F32),
        compiler_params=_params(("arbitrary",)),
        name="moe_combine",
    )(tile_rows, y_choices, cnt, first, x1, mod6, ln_g, ln_b)


def moe_block(x1, hf, p_t, mod6, ln_g, ln_b, w_gate, w_up, w_down, layer):
    thr, need = route_threshold(p_t)
    parts = []
    for si, (tok0, n, cap) in enumerate(ROUTE_SETS):
        p_blk = p_t[:, tok0:tok0 + n].reshape(N_EXPERTS, n // LANES, LANES)
        parts.append(route_lists(p_blk, thr, need, si))
    lists = jnp.concatenate([p[0] for p in parts], axis=1)
    cnt = jnp.concatenate([p[1] for p in parts], axis=0)
    first = jnp.concatenate([p[2] for p in parts], axis=0)
    tok = lists[:, :, 0].astype(jnp.int32)
    dst = lists[:, :, 1].astype(jnp.int32)
    y_choices = expert_ffn(tok, dst, hf, lists, w_gate, w_up, w_down, layer)
    tile_rows = jnp.concatenate([first[::COMB_TM, 0], jnp.full((1,), N_CHOICES, F32)]).astype(jnp.int32)
    return moe_combine(tile_rows, y_choices, cnt, first, x1, mod6, ln_g, ln_b, layer)


def kernel(x_prompt, x_sample, cache_mla_ckv, cache_mla_kpe, state_hgrn, state_rglru, c, c_ctx,
           w_in, w_out, hg_lb_logits, hg_norm_g, rg_conv_w, rg_conv_b, rg_w_r, rg_b_r, rg_w_i, rg_b_i,
           rg_lambda, mla_q_norm_g, mla_kv_norm_g, mla_w_uq, mla_w_uk, mla_w_uv, ada_w, ada_b,
           ln1_g, ln1_b, ln2_g, ln2_b, moe_router, moe_w_gate, moe_w_up, moe_w_down):
    x = jnp.concatenate([x_prompt.reshape(N_CTX, D_MODEL), x_sample.reshape(N_LAT, D_MODEL)], axis=0)
    cvec = jnp.concatenate([c_ctx[None, :], c, jnp.zeros((SUBLANES - N_MODROWS, D_MODEL), F32)], axis=0)
    mod = ada_mod(cvec, ada_w, ada_b)
    mod6 = mod[:, :N_MODROWS].reshape(DEPTH * N_MODROWS, 6, D_MODEL)
    cos_t, sin_t = rope_tables()
    vec = lambda a: a.reshape(DEPTH, 1, a.shape[-1])
    hg_ng, cb, qg, kg = vec(hg_norm_g), vec(rg_conv_b), vec(mla_q_norm_g), vec(mla_kv_norm_g)
    g1, b1, g2, b2 = vec(ln1_g), vec(ln1_b), vec(ln2_g), vec(ln2_b)
    w_r, w_i, w_in_b = rg_w_r.astype(BF16), rg_w_i.astype(BF16), w_in.astype(BF16)
    router = jnp.pad(moe_router, ((0, 0), (0, 0), (0, LANES - N_EXPERTS))).astype(BF16)

    ckvs, kpes, hgs, rgs = [], [], [], []
    for l in range(DEPTH):
        z_hg, z_rg, z_cq, z_kv = in_proj(x, mod6, w_in_b[l], prep_w_pe(w_in[l]), l)

        o_hg_f, o_hg_b, hg_fin_f, hg_fin_b = hgrn_mixer(z_hg, hg_lb_logits, state_hgrn, l)

        rg_args = (rg_conv_w, cb, w_r, rg_b_r, w_i, rg_b_i, rg_lambda)
        o_rg_c, rg_fin = rglru_mixer(z_rg, *rg_args, None, l, False)
        (o_rg_l,) = rglru_mixer(z_rg, *rg_args, state_rglru, l, True)
        o_rg = jnp.concatenate([o_rg_c, o_rg_l], axis=0)

        w_uk, w_uv = mla_w_uk[l].astype(BF16), mla_w_uv[l].astype(BF16)
        q = q_proj(z_cq, qg, prep_w_uq(mla_w_uq[l]), cos_t, sin_t, l)
        k_tok, v_tok, ckv_n, kpe = kv_proj(z_kv, kg, w_uk, w_uv, cos_t, sin_t, l)
        cache = jnp.concatenate([cache_mla_ckv[:, l].reshape(DEC_BATCH * PAST_LEN, KV_LORA),
                                 cache_mla_kpe[:, l].reshape(DEC_BATCH * PAST_LEN, MLA_ROPE_DIM),
                                 jnp.zeros((DEC_BATCH * PAST_LEN, MLA_ROPE_DIM), F32)], axis=1)
        k_cache, v_cache = kv_proj_cache(cache, kg, w_uk, w_uv, cos_t[:PAST_LEN], sin_t[:PAST_LEN], l)
        o_mla = jnp.concatenate([mla_attention(q, k_tok, v_tok, None, None, False),
                                 mla_attention(q, k_tok, v_tok, k_cache, v_cache, True)], axis=0)

        x1, hf, p_t = out_proj(o_hg_f, o_hg_b, z_hg, hg_ng, o_rg, o_mla, w_out[l].astype(BF16), x, mod6, g1, b1,
                               router[l], l)
        x = moe_block(x1, hf, p_t, mod6, g2, b2, moe_w_gate, moe_w_up, moe_w_down, l)

        ckvs.append(ckv_n[:N_CTX].reshape(BATCH, SEQ, KV_LORA))
        kpes.append(kpe[:N_CTX].reshape(BATCH, SEQ, MLA_ROPE_DIM))
        hgs.append(jnp.stack([hg_fin_f, hg_fin_b], axis=1))
        rgs.append(rg_fin)

    y_prompt = x[:N_CTX].reshape(BATCH, SEQ, D_MODEL)
    y_sample = x[N_CTX:].reshape(DEC_BATCH, DEC_SEQ, D_MODEL)
    return (y_prompt, y_sample, jnp.stack(ckvs, axis=1), jnp.stack(kpes, axis=1),
            jnp.stack(hgs, axis=1), jnp.stack(rgs, axis=1))
```

```python
import functools
import math

import jax
import jax.numpy as jnp
import numpy as np
from jax import lax
from jax.experimental import pallas as pl
from jax.experimental.pallas import tpu as pltpu

F32 = jnp.float32
BF16 = jnp.bfloat16

D_MODEL = 2048
BATCH = 16
SEQ = 256
DEPTH = 2
DEC_BATCH = 2
DEC_SEQ = 4096
PAST_LEN = 512
GRID_W = 64
HG_WIDTH = 512
HG_HEADS = 4
HG_DK = 128
RG_WIDTH = 512
RG_HEADS = 4
RG_BLOCK = 128
RG_CONV = 4
RG_C = 8.0
MLA_WIDTH = 1024
MLA_HEADS = 8
MLA_V_DIM = 128
MLA_NOPE_DIM = 128
MLA_ROPE_DIM = 64
MLA_QK_DIM = 192
Q_LORA = 512
KV_LORA = 512
ROPE_THETA = 10000.0
Q_BLOCK = 128
N_EXPERTS = 16
CAP_FACTOR = 2
D_FF_EXPERT = 1024
DN_ALPHA = (2.0 * DEPTH) ** 0.25
LN_EPS = 1e-5
RMS_EPS = 1e-6

N_CTX = BATCH * SEQ
N_LAT = DEC_BATCH * DEC_SEQ
N_TOK = N_CTX + N_LAT
N_MODROWS = 1 + DEC_BATCH

LANES = 128
SUBLANES = 8
VMEM_LIMIT_BYTES = 56 * 1024 * 1024

W_HG = 5 * HG_WIDTH
W_RG = 2 * RG_WIDTH
W_KV = KV_LORA + 2 * MLA_ROPE_DIM
IN_GROUPS = (W_HG, W_RG, Q_LORA, W_KV)
IN_COLS_OWN = sum(IN_GROUPS)


def _params(sem, vmem=VMEM_LIMIT_BYTES):
    return pltpu.CompilerParams(dimension_semantics=sem, vmem_limit_bytes=vmem)


def _mod_row(i, tm):
    n_ctx_tiles = N_CTX // tm
    per_batch = DEC_SEQ // tm
    return jnp.where(i < n_ctx_tiles, 0, 1 + (i - n_ctx_tiles) // per_batch)


ADA_TN = 1024


def _ada_kernel(c_ref, w_ref, b_ref, o_ref):
    c = c_ref[...]
    s = (c * jax.nn.sigmoid(c)).astype(BF16)
    o_ref[...] = jnp.dot(s, w_ref[...].astype(BF16), preferred_element_type=F32) + b_ref[...]


def ada_mod(cvec, ada_w, ada_b):
    ncol = 6 * D_MODEL
    return pl.pallas_call(
        _ada_kernel,
        grid=(DEPTH, ncol // ADA_TN),
        in_specs=[
            pl.BlockSpec((SUBLANES, D_MODEL), lambda l, j: (0, 0)),
            pl.BlockSpec((None, D_MODEL, ADA_TN), lambda l, j: (l, 0, j)),
            pl.BlockSpec((None, 1, ADA_TN), lambda l, j: (l, 0, j)),
        ],
        out_specs=pl.BlockSpec((None, SUBLANES, ADA_TN), lambda l, j: (l, 0, j)),
        out_shape=jax.ShapeDtypeStruct((DEPTH, SUBLANES, ncol), F32),
        compiler_params=_params(("arbitrary", "arbitrary")),
        name="ada_mod",
    )(cvec, ada_w, ada_b.reshape(DEPTH, 1, ncol))


INPROJ_TM = 256


def _inproj_kernel(x_ref, mod_ref, w_ref, wpe_ref, ohg_ref, org_ref, ocq_ref, okv_ref):
    m = mod_ref[...]
    hm = (x_ref[...] * (1.0 + m[1:2]) + m[0:1]).astype(BF16)
    a = 0
    for o_ref, width in zip((ohg_ref, org_ref, ocq_ref), IN_GROUPS[:3]):
        o_ref[...] = jnp.dot(hm, w_ref[:, a:a + width], preferred_element_type=F32)
        a += width
    okv_ref[:, 0:KV_LORA] = jnp.dot(hm, w_ref[:, a:a + KV_LORA], preferred_element_type=F32)
    okv_ref[:, KV_LORA:] = jnp.dot(hm, wpe_ref[...], preferred_element_type=F32)


def in_proj(x, mod6, w_in_l, w_pe_l, layer):
    tm = INPROJ_TM
    const = lambda a: pl.BlockSpec(a.shape, lambda i: (0, 0), pipeline_mode=pl.Buffered(1))
    return pl.pallas_call(
        _inproj_kernel,
        grid=(N_TOK // tm,),
        in_specs=[
            pl.BlockSpec((tm, D_MODEL), lambda i: (i, 0)),
            pl.BlockSpec((None, 6, D_MODEL), lambda i: (layer * N_MODROWS + _mod_row(i, tm), 0, 0)),
            const(w_in_l), const(w_pe_l),
        ],
        out_specs=[pl.BlockSpec((tm, w), lambda i: (i, 0)) for w in IN_GROUPS],
        out_shape=[jax.ShapeDtypeStruct((N_TOK, w), F32) for w in IN_GROUPS],
        compiler_params=_params(("arbitrary",)),
        name="in_proj",
    )(x, mod6, w_in_l, w_pe_l)


def _rot_cols(w):
    quarter = MLA_ROPE_DIM // 4
    j = np.arange(MLA_ROPE_DIM)
    first = (j % (2 * quarter)) < quarter
    src = np.where(first, j + quarter, j - quarter)
    sign = np.where(first, -1.0, 1.0).astype(np.float32)
    return w[..., src] * sign


def prep_w_pe(w_in_l):
    kpe = w_in_l[:, -MLA_ROPE_DIM:]
    return jnp.concatenate([kpe, _rot_cols(kpe)], axis=1).astype(BF16)


HG_CHUNK = 64
HG_SAFE_EXP = 80.0


def _silu(x):
    return x * jax.nn.sigmoid(x)


HG_TB = SEQ
HG_BLOCKS = N_TOK // HG_TB
HG_CTX_BLOCKS = N_CTX // HG_TB
HG_SEQ_BLOCKS = DEC_SEQ // HG_TB


def _hg_exact_att(q, k, cum):
    c = q.shape[0]
    lane = lax.broadcasted_iota(jnp.int32, (c, c), 1)
    row = lax.broadcasted_iota(jnp.int32, (c, 1), 0)

    def body(s_idx, att):
        sel = row == s_idx
        cum_s = jnp.sum(jnp.where(sel, cum, 0.0), axis=0, keepdims=True)
        k_s = jnp.sum(jnp.where(sel, k, 0.0), axis=0, keepdims=True)
        dec = jnp.exp(jnp.minimum(cum - cum_s, 0.0))
        col = jnp.sum(q * k_s * dec, axis=1, keepdims=True)
        return jnp.where(lane == s_idx, col, att)

    return lax.fori_loop(0, c, body, jnp.zeros((c, c), F32))


def _hgrn_kernel(qf_ref, ff_ref, vf_ref, qb_ref, fb_ref, vb_ref, lbl_ref, s0f_ref, s0b_ref,
                 of_ref, ob_ref, sf_ref, sb_ref, st_scr, *, layer):
    c = HG_CHUNK
    n_c = HG_TB // c
    heads = range(HG_HEADS)
    i = pl.program_id(0)
    blk = (i, HG_BLOCKS - 1 - i)
    is_ctx = tuple(b < HG_CTX_BLOCKS for b in blk)
    pos = tuple(lax.rem(b - HG_CTX_BLOCKS, HG_SEQ_BLOCKS) for b in blk)
    starts = (is_ctx[0] | (pos[0] == 0), is_ctx[1] | (pos[1] == HG_SEQ_BLOCKS - 1))

    for d, s0_ref in enumerate((s0f_ref, s0b_ref)):
        @pl.when(starts[d] & is_ctx[d])
        def _():
            st_scr[d] = jnp.zeros((HG_HEADS, HG_DK, HG_DK), F32)

        @pl.when(starts[d] & jnp.logical_not(is_ctx[d]))
        def _():
            for h in heads:
                st_scr[d, h] = s0_ref[h].T

    lg = lbl_ref[...]
    e = jnp.exp(lg - jnp.max(lg, axis=0, keepdims=True))
    sm = e / jnp.sum(e, axis=0, keepdims=True)
    lb = jnp.zeros_like(sm[0])
    for j in range(1, layer + 1):
        lb = lb + sm[j]

    r = lax.broadcasted_iota(jnp.int32, (c, c), 0)
    s = lax.broadcasted_iota(jnp.int32, (c, c), 1)
    masks = (r >= s, r <= s)
    marks = ((0, c // 2 - 1, c - 1), (c - 1, c // 2, 0))
    in_refs = ((qf_ref, ff_ref, vf_ref), (qb_ref, fb_ref, vb_ref))
    out_refs = (of_ref, ob_ref)

    work = ([], [])
    guard = jnp.float32(0.0)
    for d in range(2):
        q_ref, f_ref, v_ref = in_refs[d]
        lbd = lb[d:d + 1]
        first, mid, last = marks[d]
        for ci in (range(n_c) if d == 0 else reversed(range(n_c))):
            rows = slice(ci * c, (ci + 1) * c)
            f = lbd + (1.0 - lbd) * jax.nn.sigmoid(f_ref[rows, :])
            cum = jnp.dot(masks[d].astype(F32), jnp.log(f), precision=lax.Precision.HIGHEST,
                          preferred_element_type=F32)
            c_mid, c_last = cum[mid:mid + 1], cum[last:last + 1]
            guard = jnp.maximum(guard, jnp.max(jnp.maximum(cum[first:first + 1] - c_mid, c_mid - c_last)))
            work[d].append((rows, _silu(q_ref[rows, :]), 1.0 - f, v_ref[rows, :].astype(BF16), cum))

    nt = (((1,), (1,)), ((), ()))
    tn = (((0,), (0,)), ((), ()))

    def run(exact):
        for d in range(2):
            _, mid, last = marks[d]
            sts = [st_scr[d, h] for h in heads]
            for rows, q, k, v, cum in work[d]:
                c_mid, c_last = cum[mid:mid + 1], cum[last:last + 1]
                q_in = (q * jnp.exp(cum)).astype(BF16)
                k_end = (k * jnp.exp(c_last - cum)).astype(BF16)
                dec = jnp.exp(c_last)
                if not exact:
                    qt = (q * jnp.exp(cum - c_mid)).astype(BF16)
                    kt = (k * jnp.exp(c_mid - cum)).astype(BF16)
                outs = []
                for h in heads:
                    ls = slice(h * LANES, (h + 1) * LANES)
                    inter = lax.dot_general(q_in[:, ls], sts[h].astype(BF16), nt, preferred_element_type=F32)
                    if exact:
                        att = _hg_exact_att(q[:, ls], k[:, ls], cum[:, ls])
                    else:
                        att = lax.dot_general(qt[:, ls], kt[:, ls], nt, preferred_element_type=F32)
                    att = jnp.where(masks[d], att, 0.0).astype(BF16)
                    outs.append(inter + jnp.dot(att, v[:, ls], preferred_element_type=F32))
                    sts[h] = sts[h] * dec[:, ls] + lax.dot_general(v[:, ls], k_end[:, ls], tn,
                                                                   preferred_element_type=F32)
                out_refs[d][rows, :] = jnp.concatenate(outs, axis=1)
            for h in heads:
                st_scr[d, h] = sts[h]

    unsafe = guard > HG_SAFE_EXP
    pl.when(unsafe)(lambda: run(True))
    pl.when(jnp.logical_not(unsafe))(lambda: run(False))

    for d, fin_ref in enumerate((sf_ref, sb_ref)):
        @pl.when(is_ctx[d])
        def _():
            for h in heads:
                fin_ref[h] = st_scr[d, h].T


def hgrn_mixer(z_hg, lb_logits, state, layer):
    last = HG_BLOCKS - 1
    blocks = (lambda i: i, lambda i: last - i)
    lat_batch = lambda b: jnp.clip(lax.div(b - HG_CTX_BLOCKS, HG_SEQ_BLOCKS), 0, DEC_BATCH - 1)
    ctx_seq = lambda b: jnp.minimum(b, BATCH - 1)

    def col(d, group):
        return pl.BlockSpec((HG_TB, HG_WIDTH), lambda i: (blocks[d](i), group))

    def s0(d):
        return pl.BlockSpec((None, None, None, HG_HEADS, HG_DK, HG_DK),
                            lambda i: (lat_batch(blocks[d](i)), layer, d, 0, 0, 0))

    def fin(d):
        return pl.BlockSpec((None, HG_HEADS, HG_DK, HG_DK), lambda i: (ctx_seq(blocks[d](i)), 0, 0, 0))

    return pl.pallas_call(
        functools.partial(_hgrn_kernel, layer=layer),
        grid=(HG_BLOCKS,),
        in_specs=[col(0, 0), col(0, 1), col(0, 3), col(1, 0), col(1, 2), col(1, 3),
                  pl.BlockSpec((DEPTH, 2, HG_WIDTH), lambda i: (0, 0, 0)), s0(0), s0(1)],
        out_specs=[col(0, 0), col(1, 0), fin(0), fin(1)],
        out_shape=[jax.ShapeDtypeStruct((N_TOK, HG_WIDTH), F32), jax.ShapeDtypeStruct((N_TOK, HG_WIDTH), F32),
                   jax.ShapeDtypeStruct((BATCH, HG_HEADS, HG_DK, HG_DK), F32),
                   jax.ShapeDtypeStruct((BATCH, HG_HEADS, HG_DK, HG_DK), F32)],
        scratch_shapes=[pltpu.VMEM((2, HG_HEADS, HG_DK, HG_DK), F32)],
        compiler_params=_params(("arbitrary",)),
        name="hgrn",
    )(z_hg, z_hg, z_hg, z_hg, z_hg, z_hg, lb_logits, state, state)


RG_ROWS = 256
RG_PAD = SUBLANES
RG_SCAN_UNROLL = 8


def _tile_scan(a, b, reverse):
    row = lax.broadcasted_iota(jnp.int32, a.shape, 0)
    for sh in (1, 2, 4):
        if reverse:
            a_s, b_s = pltpu.roll(a, SUBLANES - sh, 0), pltpu.roll(b, SUBLANES - sh, 0)
            valid = row < SUBLANES - sh
        else:
            a_s, b_s = pltpu.roll(a, sh, 0), pltpu.roll(b, sh, 0)
            valid = row >= sh
        b = jnp.where(valid, a * b_s + b, b)
        a = jnp.where(valid, a * a_s, a)
    return a, b


def _rglru_kernel(*refs, seq_len, has_state):
    if has_state:
        (x_ref, y_ref, cw_ref, cb_ref, wr_ref, br_ref, wi_ref, bi_ref, lam_ref, h0_ref,
         o_ref, xp_scr, a_scr, b_scr) = refs
    else:
        (x_ref, y_ref, cw_ref, cb_ref, wr_ref, br_ref, wi_ref, bi_ref, lam_ref,
         o_ref, hfin_ref, xp_scr, a_scr, b_scr) = refs
    t = seq_len
    zeros = jnp.zeros((RG_PAD, LANES), F32)
    xp_scr[0:RG_PAD, :] = zeros
    xp_scr[RG_PAD + t:, :] = zeros
    xp_scr[RG_PAD:RG_PAD + t, :] = x_ref[...]

    cw = cw_ref[...]
    cb = cb_ref[...]
    lam = lam_ref[...]
    sp = jnp.maximum(-lam, 0.0) + jnp.log1p(jnp.exp(-jnp.abs(lam)))

    def gates(ci, _):
        r0 = pl.multiple_of(ci * RG_ROWS, RG_ROWS)
        xc = cb
        for j in range(RG_CONV):
            xc = xc + cw[j:j + 1] * xp_scr[pl.ds(r0 + RG_PAD - RG_CONV // 2 + j, RG_ROWS), :]
        rows = pl.ds(r0, RG_ROWS)
        xcb = xc.astype(BF16)
        for d in range(2):
            r = jax.nn.sigmoid(jnp.dot(xcb, wr_ref[d], preferred_element_type=F32) + br_ref[d])
            ig = jax.nn.sigmoid(jnp.dot(xcb, wi_ref[d], preferred_element_type=F32) + bi_ref[d])
            log_a = -RG_C * r * sp[d:d + 1]
            a_scr[d, rows, :] = jnp.exp(log_a)
            b_scr[d, rows, :] = jnp.sqrt(1.0 - jnp.exp(2.0 * log_a)) * (ig * xc)
        return 0

    lax.fori_loop(0, t // RG_ROWS, gates, 0)

    n_tiles = t // SUBLANES

    def scan(j, carry):
        h_f, h_b = carry
        rows_f = pl.ds(pl.multiple_of(j * SUBLANES, SUBLANES), SUBLANES)
        rows_b = pl.ds(pl.multiple_of((n_tiles - 1 - j) * SUBLANES, SUBLANES), SUBLANES)
        aa, bb = _tile_scan(a_scr[0, rows_f, :], b_scr[0, rows_f, :], False)
        hf = aa * h_f + bb
        b_scr[0, rows_f, :] = hf
        aa, bb = _tile_scan(a_scr[1, rows_b, :], b_scr[1, rows_b, :], True)
        hb = aa * h_b + bb
        b_scr[1, rows_b, :] = hb
        return hf[SUBLANES - 1:SUBLANES], hb[0:1]

    if has_state:
        h0 = h0_ref[...]
        init = (h0[0:1], h0[1:2])
    else:
        init = (jnp.zeros((1, LANES), F32), jnp.zeros((1, LANES), F32))
    h_f, h_b = lax.fori_loop(0, n_tiles, scan, init, unroll=RG_SCAN_UNROLL)
    if not has_state:
        hfin_ref[0:1, :] = h_f
        hfin_ref[1:2, :] = h_b
    o_ref[...] = (b_scr[0] + b_scr[1]) * jax.nn.gelu(y_ref[...], approximate=True)


def rglru_mixer(z_rg, conv_w, conv_b, w_r, b_r, w_i, b_i, lam, state, layer, latent):
    seq_len = DEC_SEQ if latent else SEQ
    n_seq = DEC_BATCH if latent else BATCH
    blk0 = N_CTX // seq_len if latent else 0
    h = RG_HEADS
    vec = lambda rows: pl.BlockSpec((None, rows, LANES), lambda b, hh: (layer, 0, hh))
    wspec = pl.BlockSpec((None, 2, None, RG_BLOCK, RG_BLOCK), lambda b, hh: (layer, 0, hh, 0, 0))
    bspec = pl.BlockSpec((None, 2, 1, LANES), lambda b, hh: (layer, 0, 0, hh))
    in_specs = [
        pl.BlockSpec((seq_len, LANES), lambda b, hh: (blk0 + b, hh)),
        pl.BlockSpec((seq_len, LANES), lambda b, hh: (blk0 + b, h + hh)),
        vec(RG_CONV), vec(1), wspec, bspec, wspec, bspec, vec(2),
    ]
    args = [z_rg, z_rg, conv_w, conv_b, w_r, b_r.reshape(DEPTH, 2, 1, RG_WIDTH), w_i,
            b_i.reshape(DEPTH, 2, 1, RG_WIDTH), lam]
    o_spec = pl.BlockSpec((seq_len, LANES), lambda b, hh: (b, hh))
    o_shape = jax.ShapeDtypeStruct((n_seq * seq_len, RG_WIDTH), F32)
    if latent:
        in_specs.append(pl.BlockSpec((None, None, 2, LANES), lambda b, hh: (b, layer, 0, hh)))
        args.append(state)
        out_specs, out_shape = [o_spec], [o_shape]
    else:
        out_specs = [o_spec, pl.BlockSpec((None, 2, LANES), lambda b, hh: (b, 0, hh))]
        out_shape = [o_shape, jax.ShapeDtypeStruct((BATCH, 2, RG_WIDTH), F32)]
    return pl.pallas_call(
        functools.partial(_rglru_kernel, seq_len=seq_len, has_state=latent),
        grid=(n_seq, h),
        in_specs=in_specs,
        out_specs=out_specs,
        out_shape=out_shape,
        scratch_shapes=[pltpu.VMEM((seq_len + 2 * RG_PAD, LANES), F32),
                        pltpu.VMEM((2, seq_len, LANES), F32), pltpu.VMEM((2, seq_len, LANES), F32)],
        compiler_params=_params(("arbitrary", "arbitrary")),
        name="rglru_lat" if latent else "rglru_ctx",
    )(*args)


MLA_HEAD_PAD = 2 * LANES
MLA_TM = 512
ATT_TQ = 256


def rope_tables():
    half = MLA_ROPE_DIM // 2
    t = np.arange(DEC_SEQ)
    row = (t // GRID_W).astype(np.float32)
    col = (t % GRID_W).astype(np.float32)
    inv = (ROPE_THETA ** (-np.arange(0, half, 2, dtype=np.float32) / half)).astype(np.float32)
    ar, ac = row[:, None] * inv, col[:, None] * inv
    cos = np.concatenate([np.cos(ar), np.cos(ar), np.cos(ac), np.cos(ac)], -1)
    sin = np.concatenate([np.sin(ar), np.sin(ar), np.sin(ac), np.sin(ac)], -1)
    pad = np.zeros((DEC_SEQ, LANES - MLA_ROPE_DIM), np.float32)
    cos_lat = np.tile(np.concatenate([cos, pad], -1), (DEC_BATCH, 1))
    sin_lat = np.tile(np.concatenate([sin, pad], -1), (DEC_BATCH, 1))
    cos_ctx = np.concatenate([np.ones((N_CTX, MLA_ROPE_DIM), np.float32), np.zeros((N_CTX, LANES - MLA_ROPE_DIM), np.float32)], -1)
    sin_ctx = np.zeros((N_CTX, LANES), np.float32)
    return (jnp.asarray(np.concatenate([cos_ctx, cos_lat], 0).astype(np.float32)),
            jnp.asarray(np.concatenate([sin_ctx, sin_lat], 0).astype(np.float32)))


def _rope_group(x, cos, sin):
    return x * cos + pltpu.roll(x, MLA_ROPE_DIM, 1) * sin


def _rms(x, g):
    return x * lax.rsqrt(jnp.mean(x * x, axis=-1, keepdims=True) + RMS_EPS) * g


def _qproj_kernel(cq_ref, g_ref, w_ref, cos_ref, sin_ref, q_ref):
    xn = _rms(cq_ref[...], g_ref[...]).astype(BF16)
    qm = jnp.dot(xn, w_ref[...], preferred_element_type=F32)
    cos, sin = cos_ref[...], sin_ref[...]
    for h in range(MLA_HEADS):
        a = h * MLA_HEAD_PAD
        q_ref[:, a:a + LANES] = qm[:, a:a + LANES].astype(BF16)
        q_ref[:, a + LANES:a + 2 * LANES] = _rope_group(qm[:, a + LANES:a + 2 * LANES], cos, sin).astype(BF16)


def q_proj(z_cq, q_norm_g, w_uq_l, cos_t, sin_t, layer):
    tm = MLA_TM
    width = MLA_HEADS * MLA_HEAD_PAD
    return pl.pallas_call(
        _qproj_kernel,
        grid=(N_TOK // tm,),
        in_specs=[
            pl.BlockSpec((tm, Q_LORA), lambda i: (i, 0)),
            pl.BlockSpec((None, 1, Q_LORA), lambda i: (layer, 0, 0)),
            pl.BlockSpec((Q_LORA, width), lambda i: (0, 0)),
            pl.BlockSpec((tm, LANES), lambda i: (i, 0)),
            pl.BlockSpec((tm, LANES), lambda i: (i, 0)),
        ],
        out_specs=pl.BlockSpec((tm, width), lambda i: (i, 0)),
        out_shape=jax.ShapeDtypeStruct((N_TOK, width), BF16),
        compiler_params=_params(("arbitrary",)),
        name="q_proj",
    )(z_cq, q_norm_g, w_uq_l, cos_t, sin_t)


def prep_w_uq(w_uq_l):
    w = w_uq_l.reshape(Q_LORA, MLA_HEADS, MLA_QK_DIM)
    pe = w[..., MLA_NOPE_DIM:]
    w = jnp.concatenate([w, _rot_cols(pe)], axis=-1)
    return w.reshape(Q_LORA, MLA_HEADS * MLA_HEAD_PAD).astype(BF16)


def _kvproj_kernel(zkv_ref, g_ref, wk_ref, wv_ref, cos_ref, sin_ref, *outs, normalize):
    if normalize:
        k_ref, v_ref, ckv_ref, kpe_ref = outs
    else:
        k_ref, v_ref = outs
    ckv = zkv_ref[:, 0:KV_LORA]
    if normalize:
        ckv = _rms(ckv, g_ref[...])
        ckv_ref[...] = ckv
    pe_group = zkv_ref[:, KV_LORA:KV_LORA + LANES]
    if normalize:
        kpe_ref[...] = pe_group[:, 0:MLA_ROPE_DIM]
    pe = _rope_group(pe_group, cos_ref[...], sin_ref[...]).astype(BF16)
    cb = ckv.astype(BF16)
    kn = jnp.dot(cb, wk_ref[...], preferred_element_type=F32)
    v_ref[...] = jnp.dot(cb, wv_ref[...], preferred_element_type=F32).astype(BF16)
    for h in range(MLA_HEADS):
        a = h * MLA_HEAD_PAD
        k_ref[:, a:a + LANES] = kn[:, h * LANES:(h + 1) * LANES].astype(BF16)
        k_ref[:, a + LANES:a + 2 * LANES] = pe


def kv_proj(z_kv, kv_norm_g, w_uk_l, w_uv_l, cos_t, sin_t, layer):
    tm = MLA_TM
    kw, vw = MLA_HEADS * MLA_HEAD_PAD, MLA_HEADS * MLA_V_DIM
    return pl.pallas_call(
        functools.partial(_kvproj_kernel, normalize=True),
        grid=(N_TOK // tm,),
        in_specs=[
            pl.BlockSpec((tm, W_KV), lambda i: (i, 0)),
            pl.BlockSpec((None, 1, KV_LORA), lambda i: (layer, 0, 0)),
            pl.BlockSpec((KV_LORA, MLA_HEADS * MLA_NOPE_DIM), lambda i: (0, 0)),
            pl.BlockSpec((KV_LORA, vw), lambda i: (0, 0)),
            pl.BlockSpec((tm, LANES), lambda i: (i, 0)),
            pl.BlockSpec((tm, LANES), lambda i: (i, 0)),
        ],
        out_specs=[
            pl.BlockSpec((tm, kw), lambda i: (i, 0)),
            pl.BlockSpec((tm, vw), lambda i: (i, 0)),
            pl.BlockSpec((tm, KV_LORA), lambda i: (i, 0)),
            pl.BlockSpec((tm, MLA_ROPE_DIM), lambda i: (i, 0)),
        ],
        out_shape=[
            jax.ShapeDtypeStruct((N_TOK, kw), BF16),
            jax.ShapeDtypeStruct((N_TOK, vw), BF16),
            jax.ShapeDtypeStruct((N_TOK, KV_LORA), F32),
            jax.ShapeDtypeStruct((N_TOK, MLA_ROPE_DIM), F32),
        ],
        compiler_params=_params(("arbitrary",)),
        name="kv_proj",
    )(z_kv, kv_norm_g, w_uk_l, w_uv_l, cos_t, sin_t)


def kv_proj_cache(cache_kv, kv_norm_g, w_uk_l, w_uv_l, cos_c, sin_c, layer):
    tm = PAST_LEN
    kw, vw = MLA_HEADS * MLA_HEAD_PAD, MLA_HEADS * MLA_V_DIM
    n = DEC_BATCH * PAST_LEN
    return pl.pallas_call(
        functools.partial(_kvproj_kernel, normalize=False),
        grid=(DEC_BATCH,),
        in_specs=[
            pl.BlockSpec((tm, W_KV), lambda b: (b, 0)),
            pl.BlockSpec((None, 1, KV_LORA), lambda b: (layer, 0, 0)),
            pl.BlockSpec((KV_LORA, MLA_HEADS * MLA_NOPE_DIM), lambda b: (0, 0)),
            pl.BlockSpec((KV_LORA, vw), lambda b: (0, 0)),
            pl.BlockSpec((tm, LANES), lambda b: (0, 0)),
            pl.BlockSpec((tm, LANES), lambda b: (0, 0)),
        ],
        out_specs=[pl.BlockSpec((tm, kw), lambda b: (b, 0)), pl.BlockSpec((tm, vw), lambda b: (b, 0))],
        out_shape=[jax.ShapeDtypeStruct((n, kw), BF16), jax.ShapeDtypeStruct((n, vw), BF16)],
        compiler_params=_params(("arbitrary",)),
        name="kv_proj_cache",
    )(cache_kv, kv_norm_g, w_uk_l, w_uv_l, cos_c, sin_c)


def _attn_kernel(q_ref, *refs):
    o_ref = refs[-1]
    segs = [(refs[i], refs[i + 1]) for i in range(0, len(refs) - 1, 2)]
    nt = (((1,), (1,)), ((), ()))
    q = q_ref[...]
    scores = [lax.dot_general(q, k_ref[...], nt, preferred_element_type=F32) for k_ref, _ in segs]
    m = scores[0].max(axis=-1, keepdims=True)
    for s in scores[1:]:
        m = jnp.maximum(m, s.max(axis=-1, keepdims=True))
    l, o = 0.0, 0.0
    for s, (_, v_ref) in zip(scores, segs):
        p = jnp.exp2((s - m) * (MLA_QK_DIM ** -0.5 * math.log2(math.e)))
        l = l + jnp.sum(p, axis=-1, keepdims=True)
        o = o + jnp.dot(p.astype(BF16), v_ref[...], preferred_element_type=F32)
    o_ref[...] = o / l


def mla_attention(q, k_tok, v_tok, k_cache, v_cache, latent):
    h = MLA_HEADS
    if latent:
        tq = ATT_TQ
        n_q = DEC_SEQ // tq
        grid = (DEC_BATCH, h, n_q)
        q_map = lambda b, hh, i: (N_CTX // tq + b * n_q + i, hh)
        kv_map = lambda b, hh, i: (N_CTX // DEC_SEQ + b, hh)
        o_map = lambda b, hh, i: (b * n_q + i, hh)
        in_specs = [
            pl.BlockSpec((tq, MLA_HEAD_PAD), q_map),
            pl.BlockSpec((DEC_SEQ, MLA_HEAD_PAD), kv_map),
            pl.BlockSpec((DEC_SEQ, MLA_V_DIM), kv_map),
            pl.BlockSpec((PAST_LEN, MLA_HEAD_PAD), lambda b, hh, i: (b, hh)),
            pl.BlockSpec((PAST_LEN, MLA_V_DIM), lambda b, hh, i: (b, hh)),
        ]
        args = [q, k_tok, v_tok, k_cache, v_cache]
        n_out = N_LAT
    else:
        tq = SEQ
        grid = (BATCH, h, 1)
        q_map = o_map = lambda b, hh, i: (b, hh)
        in_specs = [
            pl.BlockSpec((tq, MLA_HEAD_PAD), q_map),
            pl.BlockSpec((SEQ, MLA_HEAD_PAD), q_map),
            pl.BlockSpec((SEQ, MLA_V_DIM), q_map),
        ]
        args = [q, k_tok, v_tok]
        n_out = N_CTX
    return pl.pallas_call(
        _attn_kernel,
        grid=grid,
        in_specs=in_specs,
        out_specs=pl.BlockSpec((tq, MLA_V_DIM), o_map),
        out_shape=jax.ShapeDtypeStruct((n_out, MLA_WIDTH), F32),
        compiler_params=_params(("arbitrary",) * 3),
        name="mla_attn_lat" if latent else "mla_attn_ctx",
    )(*args)


OUT_TM = 256


def _layer_norm(y, g, b):
    mu = jnp.mean(y, axis=-1, keepdims=True)
    yc = y - mu
    var = jnp.mean(yc * yc, axis=-1, keepdims=True)
    return yc * lax.rsqrt(var + LN_EPS) * g + b


def _outproj_kernel(of_ref, ob_ref, hgg_ref, hgn_ref, org_ref, omla_ref, w_ref, x_ref, mod_ref, g_ref, b_ref, wr_ref,
                    x1_ref, hf_ref, pt_ref):
    o = of_ref[...] + ob_ref[...]
    heads = [o[:, h * HG_DK:(h + 1) * HG_DK] for h in range(HG_HEADS)]
    o = jnp.concatenate([oh * lax.rsqrt(jnp.mean(oh * oh, axis=-1, keepdims=True) + RMS_EPS) for oh in heads], axis=1)
    o_hg = o * hgn_ref[...] * _silu(hgg_ref[...])
    m = jnp.dot(o_hg.astype(BF16), w_ref[0:HG_WIDTH, :], preferred_element_type=F32)
    m += jnp.dot(org_ref[...].astype(BF16), w_ref[HG_WIDTH:HG_WIDTH + RG_WIDTH, :], preferred_element_type=F32)
    m += jnp.dot(omla_ref[...].astype(BF16), w_ref[HG_WIDTH + RG_WIDTH:, :], preferred_element_type=F32)
    md = mod_ref[...]
    x1 = _layer_norm(DN_ALPHA * x_ref[...] + md[2:3] * m, g_ref[...], b_ref[...])
    x1_ref[...] = x1
    hf = x1 * (1.0 + md[4:5]) + md[3:4]
    hf_ref[...] = hf
    logits = jnp.dot(hf.astype(BF16), wr_ref[...], preferred_element_type=F32)
    lane = lax.broadcasted_iota(jnp.int32, logits.shape, 1)
    logits = jnp.where(lane < N_EXPERTS, logits, -jnp.inf)
    e = jnp.exp(logits - jnp.max(logits, axis=-1, keepdims=True))
    p = e / jnp.sum(e, axis=-1, keepdims=True)
    pt_ref[...] = p.T[0:N_EXPERTS, :]


def out_proj(o_hg_f, o_hg_b, z_hg, hg_norm_g, o_rg, o_mla, w_out_l, x, mod6, ln_g, ln_b, w_router_l, layer):
    tm = OUT_TM
    row = lambda w: pl.BlockSpec((tm, w), lambda i: (i, 0))
    const = lambda shape: pl.BlockSpec(shape, lambda i: (0,) * len(shape), pipeline_mode=pl.Buffered(1))
    return pl.pallas_call(
        _outproj_kernel,
        grid=(N_TOK // tm,),
        in_specs=[
            row(HG_WIDTH), row(HG_WIDTH),
            pl.BlockSpec((tm, HG_WIDTH), lambda i: (i, 4)),
            pl.BlockSpec((None, 1, HG_WIDTH), lambda i: (layer, 0, 0)),
            row(RG_WIDTH), row(MLA_WIDTH),
            const((D_MODEL, D_MODEL)),
            row(D_MODEL),
            pl.BlockSpec((None, 6, D_MODEL), lambda i: (layer * N_MODROWS + _mod_row(i, tm), 0, 0)),
            pl.BlockSpec((None, 1, D_MODEL), lambda i: (layer, 0, 0)),
            pl.BlockSpec((None, 1, D_MODEL), lambda i: (layer, 0, 0)),
            const((D_MODEL, LANES)),
        ],
        out_specs=[row(D_MODEL), row(D_MODEL), pl.BlockSpec((N_EXPERTS, tm), lambda i: (0, i))],
        out_shape=[jax.ShapeDtypeStruct((N_TOK, D_MODEL), F32),
                   jax.ShapeDtypeStruct((N_TOK, D_MODEL), F32),
                   jax.ShapeDtypeStruct((N_EXPERTS, N_TOK), F32)],
        compiler_params=_params(("arbitrary",)),
        name="out_proj",
    )(o_hg_f, o_hg_b, z_hg, hg_norm_g, o_rg, o_mla, w_out_l, x, mod6, ln_g, ln_b, w_router_l)


CAP_CTX = CAP_FACTOR * N_CTX // N_EXPERTS
CAP_LAT = CAP_FACTOR * N_LAT // N_EXPERTS
ROUTE_SETS = ((0, N_CTX, CAP_CTX), (N_CTX, N_LAT, CAP_LAT))
SLOTS = CAP_CTX + CAP_LAT


def _route_thr_kernel(pt_ref, thr_ref, need_ref):
    for si, (start, n, cap) in enumerate(ROUTE_SETS):
        p = pt_ref[:, start:start + n]

        def count(mask):
            return jnp.sum(mask.astype(F32), axis=1, keepdims=True)

        def body(i, t):
            cand = t | jnp.left_shift(jnp.int32(1), 30 - i)
            return jnp.where(count(p >= pltpu.bitcast(cand, F32)) >= cap, cand, t)

        t = pltpu.bitcast(lax.fori_loop(0, 31, body, jnp.zeros((N_EXPERTS, 1), jnp.int32)), F32)
        need = cap - count(p > t)
        thr_ref[si] = jnp.broadcast_to(t, (N_EXPERTS, LANES))
        need_ref[si] = jnp.broadcast_to(need, (N_EXPERTS, LANES))


def route_threshold(p_t):
    n_sets = len(ROUTE_SETS)
    return pl.pallas_call(
        _route_thr_kernel,
        out_shape=[jax.ShapeDtypeStruct((n_sets, N_EXPERTS, LANES), F32),
                   jax.ShapeDtypeStruct((n_sets, N_EXPERTS, LANES), F32)],
        compiler_params=_params(None),
        name="route_threshold",
    )(p_t)


def _route_lists_kernel(p_ref, thr_ref, need_ref, lists_ref, cnt_ref, first_ref, sel_ref, *, nb, cap, tok0, row0):
    nt = (((1,), (1,)), ((), ()))
    r128 = lax.broadcasted_iota(jnp.int32, (LANES, LANES), 0)
    c128 = lax.broadcasted_iota(jnp.int32, (LANES, LANES), 1)
    incl = (r128 <= c128).astype(BF16)
    eye = r128 == c128
    rb = lax.broadcasted_iota(jnp.int32, (nb, nb), 0)
    cb = lax.broadcasted_iota(jnp.int32, (nb, nb), 1)
    below = (cb < rb).astype(BF16)
    incl_b = (rb <= cb).astype(BF16)
    ones8 = jnp.ones((SUBLANES, LANES), BF16)
    s_col = lax.broadcasted_iota(jnp.int32, (cap, 1), 0).astype(F32)
    lane = lax.broadcasted_iota(jnp.int32, (cap, LANES), 1)
    lane_f = lane.astype(F32)
    j_row = lax.broadcasted_iota(jnp.int32, (1, nb), 1).astype(F32)

    def dot(a, b):
        return jnp.dot(a, b, preferred_element_type=F32)

    def block_base(totals, unit):
        hi = jnp.floor(totals * (1.0 / unit))
        lo = totals - unit * hi
        bc = lambda a: jnp.broadcast_to(a, (nb, LANES)).astype(BF16)
        return unit * dot(below, bc(hi)) + dot(below, bc(lo))

    def choose(e, acc):
        p = p_ref[e]
        t = thr_ref[pl.ds(e, 1), :]
        need = need_ref[pl.ds(e, 1), :]
        eq = p == t
        eq_f = eq.astype(F32)
        eq_lp = dot(eq_f.astype(BF16), incl)
        tie_rank = eq_lp - eq_f + block_base(eq_lp[:, LANES - 1:LANES], 16.0)
        sel_f = jnp.where((p > t) | (eq & (tie_rank < need)), 1.0, 0.0)
        sel_ref[e] = sel_f
        return acc + sel_f

    cnt = lax.fori_loop(0, N_EXPERTS, choose, jnp.zeros((nb, LANES), F32))
    cnt_lp = dot(cnt.astype(BF16), incl)
    first = cnt_lp - cnt + block_base(cnt_lp[:, LANES - 1:LANES], 64.0) + row0
    first_hi = jnp.floor(first * (1.0 / LANES))
    first_lo = first - LANES * first_hi

    def expert(e, acc):
        p = p_ref[e]
        sel_f = sel_ref[e]
        sel_b = sel_f.astype(BF16)
        lp = dot(sel_b, incl)
        c_row = lax.dot_general(ones8, sel_b, nt, preferred_element_type=F32)
        incl_row = dot(c_row.astype(BF16), incl_b)[0:1]
        excl_row = incl_row - c_row[0:1]
        oh_j = jnp.where((s_col >= excl_row) & (s_col < incl_row), 1.0, 0.0)
        base_s = jnp.sum(oh_j * excl_row, axis=1, keepdims=True)
        j_s = jnp.sum(oh_j * j_row, axis=1, keepdims=True)
        oh_jb = oh_j.astype(BF16)
        lp_rows = dot(oh_jb, lp.astype(BF16))
        pos = jnp.sum(jnp.where(lp_rows <= s_col - base_s, 1.0, 0.0), axis=1, keepdims=True)
        oh_c = lane_f == pos
        p1 = p.astype(BF16)
        r1 = p - p1.astype(F32)
        p2 = r1.astype(BF16)
        p3 = (r1 - p2.astype(F32)).astype(BF16)
        p_rows = dot(oh_jb, p1) + dot(oh_jb, p2) + dot(oh_jb, p3)
        gate = jnp.sum(jnp.where(oh_c, p_rows, 0.0), axis=1, keepdims=True)
        rank_rows = dot(oh_jb, acc.astype(BF16))
        first_rows = LANES * dot(oh_jb, first_hi.astype(BF16)) + dot(oh_jb, first_lo.astype(BF16))
        dst = jnp.sum(jnp.where(oh_c, rank_rows + first_rows, 0.0), axis=1, keepdims=True)
        idx = tok0 + LANES * j_s + pos
        lists_ref[e] = jnp.where(lane == 0, idx, jnp.where(lane == 1, dst, jnp.where(lane == 2, gate, 0.0)))
        return acc + sel_f

    lax.fori_loop(0, N_EXPERTS, expert, jnp.zeros((nb, LANES), F32))

    ones_b = jnp.ones((LANES, LANES), BF16)

    def column(a, j):
        diag = jnp.where(eye, jnp.broadcast_to(a[j:j + 1, :], (LANES, LANES)), 0.0)
        return dot(diag.astype(BF16), ones_b)

    for j in range(nb):
        rows = slice(j * LANES, (j + 1) * LANES)
        cnt_ref[rows, :] = column(cnt, j)
        first_ref[rows, :] = LANES * column(first_hi, j) + column(first_lo, j)


def route_lists(p_blk, thr, need, set_index):
    tok0, n, cap = ROUTE_SETS[set_index]
    nb = n // LANES
    row0 = float(CAP_FACTOR * tok0)
    kern = functools.partial(_route_lists_kernel, nb=nb, cap=cap, tok0=tok0, row0=row0)
    return pl.pallas_call(
        kern,
        grid=(1,),
        in_specs=[
            pl.BlockSpec((N_EXPERTS, nb, LANES), lambda i: (0, 0, 0)),
            pl.BlockSpec((None, N_EXPERTS, LANES), lambda i: (set_index, 0, 0)),
            pl.BlockSpec((None, N_EXPERTS, LANES), lambda i: (set_index, 0, 0)),
        ],
        out_specs=[pl.BlockSpec((N_EXPERTS, cap, LANES), lambda i: (0, 0, 0)),
                   pl.BlockSpec((n, LANES), lambda i: (0, 0)),
                   pl.BlockSpec((n, LANES), lambda i: (0, 0))],
        out_shape=[jax.ShapeDtypeStruct((N_EXPERTS, cap, LANES), F32),
                   jax.ShapeDtypeStruct((n, LANES), F32),
                   jax.ShapeDtypeStruct((n, LANES), F32)],
        scratch_shapes=[pltpu.VMEM((N_EXPERTS, nb, LANES), F32)],
        compiler_params=_params(("arbitrary",)),
        name="route_lists_lat" if set_index else "route_lists_ctx",
    )(p_blk, thr, need)


FFN_TM = 512
FFN_FC = 256
FFN_ISSUE_UNROLL = 8


def _ffn_kernel(tok_ref, dst_ref, hf_ref, lists_ref, wg_ref, wu_ref, wd_ref, yc_ref,
                wg_b, wu_b, wd_b, x_buf, y_buf, sem_in, sem_out):
    e, f = pl.program_id(0), pl.program_id(1)
    n_e, n_f = pl.num_programs(0), pl.num_programs(1)
    tm, fc = FFN_TM, FFN_FC
    n_tiles = SLOTS // tm

    def gather(r):
        slot = r % 2

        def issue(i, _):
            pltpu.make_async_copy(hf_ref.at[pl.ds(tok_ref[0, 0, r * tm + i], 1), :],
                                  x_buf.at[slot, pl.ds(i, 1), :], sem_in.at[slot]).start()
            return 0

        lax.fori_loop(0, tm, issue, 0, unroll=FFN_ISSUE_UNROLL)

    def scatter(r):
        slot = r % 2

        def issue(i, _):
            pltpu.make_async_copy(y_buf.at[slot, pl.ds(i, 1), :],
                                  yc_ref.at[pl.ds(dst_ref[0, 0, r * tm + i], 1), :], sem_out.at[slot]).start()
            return 0

        lax.fori_loop(0, tm, issue, 0, unroll=FFN_ISSUE_UNROLL)

    def wait_gather(slot):
        pltpu.make_async_copy(hf_ref.at[pl.ds(0, tm), :], x_buf.at[slot], sem_in.at[slot]).wait()

    def wait_scatter(slot):
        pltpu.make_async_copy(y_buf.at[slot], yc_ref.at[pl.ds(0, tm), :], sem_out.at[slot]).wait()

    @pl.when(f == 0)
    def _():
        gather(0)

    cols = pl.ds(pl.multiple_of(f * fc, fc), fc)
    wg_b[:, cols] = wg_ref[...].astype(BF16)
    wu_b[:, cols] = wu_ref[...].astype(BF16)
    wd_b[cols, :] = wd_ref[...].astype(BF16)

    @pl.when(f == n_f - 1)
    def _():
        for r in range(n_tiles):
            slot = r % 2
            if r + 1 < n_tiles:
                gather(r + 1)
            wait_gather(slot)
            x = x_buf[slot].astype(BF16)
            gate = jnp.dot(x, wg_b[...], preferred_element_type=F32)
            up = jnp.dot(x, wu_b[...], preferred_element_type=F32)
            y = jnp.dot((_silu(gate) * up).astype(BF16), wd_b[...], preferred_element_type=F32)
            if r >= 2:
                wait_scatter(slot)
            else:
                pl.when(e > 0)(functools.partial(wait_scatter, slot))
            y_buf[slot] = y * lists_ref[r * tm:(r + 1) * tm, 2:3]
            scatter(r)

        @pl.when(e == n_e - 1)
        def _():
            wait_scatter((n_tiles - 2) % 2)
            wait_scatter((n_tiles - 1) % 2)


def expert_ffn(tok, dst, hf, lists, w_gate, w_up, w_down, layer):
    fc = FFN_FC
    idx_spec = pl.BlockSpec((1, 1, SLOTS), lambda e, f: (e, 0, 0), memory_space=pltpu.SMEM)
    return pl.pallas_call(
        _ffn_kernel,
        grid=(N_EXPERTS, D_FF_EXPERT // fc),
        in_specs=[
            idx_spec, idx_spec,
            pl.BlockSpec(memory_space=pl.ANY),
            pl.BlockSpec((None, SLOTS, LANES), lambda e, f: (e, 0, 0)),
            pl.BlockSpec((None, None, D_MODEL, fc), lambda e, f: (layer, e, 0, f)),
            pl.BlockSpec((None, None, D_MODEL, fc), lambda e, f: (layer, e, 0, f)),
            pl.BlockSpec((None, None, fc, D_MODEL), lambda e, f: (layer, e, f, 0)),
        ],
        out_specs=pl.BlockSpec(memory_space=pl.ANY),
        out_shape=jax.ShapeDtypeStruct((N_CHOICES, D_MODEL), F32),
        scratch_shapes=[pltpu.VMEM((D_MODEL, D_FF_EXPERT), BF16), pltpu.VMEM((D_MODEL, D_FF_EXPERT), BF16),
                        pltpu.VMEM((D_FF_EXPERT, D_MODEL), BF16),
                        pltpu.VMEM((2, FFN_TM, D_MODEL), F32), pltpu.VMEM((2, FFN_TM, D_MODEL), F32),
                        pltpu.SemaphoreType.DMA((2,)), pltpu.SemaphoreType.DMA((2,))],
        compiler_params=_params(("arbitrary", "arbitrary")),
        name="expert_ffn",
    )(tok.reshape(N_EXPERTS, 1, SLOTS), dst.reshape(N_EXPERTS, 1, SLOTS), hf, lists, w_gate, w_up, w_down)


COMB_TM = 256
COMB_ROWS = 768
N_CHOICES = N_EXPERTS * SLOTS


def _combine_kernel(rows_ref, y_ref, cnt_ref, first_ref, x1_ref, mod_ref, g_ref, b_ref, *rest, split):
    if split:
        oc_ref, ol_ref, buf, sem = rest
    else:
        o_ref, buf, sem = rest
    i, n = pl.program_id(0), pl.num_programs(0)
    ln = COMB_ROWS

    def base(t):
        return lax.div(rows_ref[t], SUBLANES) * SUBLANES

    def start(t, k):
        return pl.multiple_of(jnp.minimum(base(t) + k * ln, N_CHOICES - ln), SUBLANES)

    def copy(t, k, slot):
        return pltpu.make_async_copy(y_ref.at[pl.ds(start(t, k), ln), :], buf.at[slot], sem.at[slot])

    slot = lax.rem(i, 2)

    @pl.when(i == 0)
    def _():
        copy(0, 0, 0).start()

    @pl.when(i + 1 < n)
    def _():
        copy(i + 1, 0, 1 - slot).start()

    copy(i, 0, slot).wait()
    first = first_ref[:, 0:1]
    last = first + cnt_ref[:, 0:1]
    col = lax.broadcasted_iota(jnp.int32, (1, ln), 1)

    def contribution(k, slot_k):
        row_id = (start(i, k) + col).astype(F32)
        mine = (row_id >= first) & (row_id < last) & (row_id >= (base(i) + k * ln).astype(F32))
        onehot = jnp.where(mine, 1.0, 0.0).astype(BF16)
        y = buf[slot_k]
        hi = y.astype(BF16)
        lo = (y - hi.astype(F32)).astype(BF16)
        return jnp.dot(onehot, hi, preferred_element_type=F32) + jnp.dot(onehot, lo, preferred_element_type=F32)

    def extra(k, acc):
        c = copy(i, k, 2)
        c.start()
        c.wait()
        return acc + contribution(k, 2)

    n_groups = lax.div(rows_ref[i + 1] - base(i) + ln - 1, ln)
    acc = lax.fori_loop(1, n_groups, extra, contribution(0, slot))
    g2 = mod_ref[...][5:6]
    out = _layer_norm(DN_ALPHA * x1_ref[...] + g2 * acc, g_ref[...], b_ref[...])
    if split:
        is_ctx = i < N_CTX // COMB_TM

        @pl.when(is_ctx)
        def _():
            oc_ref[...] = out

        @pl.when(jnp.logical_not(is_ctx))
        def _():
            ol_ref[...] = out
    else:
        o_ref[...] = out


def moe_combine(tile_rows, y_choices, cnt, first, x1, mod6, ln_g, ln_b, layer, split):
    tm = COMB_TM
    n_ctx_tiles = N_CTX // tm
    row = lambda w: pl.BlockSpec((tm, w), lambda i, t: (i, 0))
    if split:
        out_specs = [pl.BlockSpec((tm, D_MODEL), lambda i, t: (jnp.minimum(i, n_ctx_tiles - 1), 0)),
                     pl.BlockSpec((tm, D_MODEL), lambda i, t: (jnp.maximum(i - n_ctx_tiles, 0), 0))]
        out_shape = [jax.ShapeDtypeStruct((N_CTX, D_MODEL), F32), jax.ShapeDtypeStruct((N_LAT, D_MODEL), F32)]
    else:
        out_specs = row(D_MODEL)
        out_shape = jax.ShapeDtypeStruct((N_TOK, D_MODEL), F32)
    grid_spec = pltpu.PrefetchScalarGridSpec(
        num_scalar_prefetch=1,
        grid=(N_TOK // tm,),
        in_specs=[
            pl.BlockSpec(memory_space=pl.ANY),
            row(LANES), row(LANES), row(D_MODEL),
            pl.BlockSpec((None, 6, D_MODEL), lambda i, t: (layer * N_MODROWS + _mod_row(i, tm), 0, 0)),
            pl.BlockSpec((None, 1, D_MODEL), lambda i, t: (layer, 0, 0)),
            pl.BlockSpec((None, 1, D_MODEL), lambda i, t: (layer, 0, 0)),
        ],
        out_specs=out_specs,
        scratch_shapes=[pltpu.VMEM((3, COMB_ROWS, D_MODEL), F32), pltpu.SemaphoreType.DMA((3,))],
    )
    return pl.pallas_call(
        functools.partial(_combine_kernel, split=split),
        grid_spec=grid_spec,
        out_shape=out_shape,
        compiler_params=_params(("arbitrary",)),
        name="moe_combine",
    )(tile_rows, y_choices, cnt, first, x1, mod6, ln_g, ln_b)


def moe_block(x1, hf, p_t, mod6, ln_g, ln_b, w_gate, w_up, w_down, layer, split):
    thr, need = route_threshold(p_t)
    parts = []
    for si, (tok0, n, cap) in enumerate(ROUTE_SETS):
        p_blk = p_t[:, tok0:tok0 + n].reshape(N_EXPERTS, n // LANES, LANES)
        parts.append(route_lists(p_blk, thr, need, si))
    lists = jnp.concatenate([p[0] for p in parts], axis=1)
    cnt = jnp.concatenate([p[1] for p in parts], axis=0)
    first = jnp.concatenate([p[2] for p in parts], axis=0)
    tok = lists[:, :, 0].astype(jnp.int32)
    dst = lists[:, :, 1].astype(jnp.int32)
    y_choices = expert_ffn(tok, dst, hf, lists, w_gate, w_up, w_down, layer)
    tile_rows = jnp.concatenate([first[::COMB_TM, 0], jnp.full((1,), N_CHOICES, F32)]).astype(jnp.int32)
    return moe_combine(tile_rows, y_choices, cnt, first, x1, mod6, ln_g, ln_b, layer, split)


def kernel(x_prompt, x_sample, cache_mla_ckv, cache_mla_kpe, state_hgrn, state_rglru, c, c_ctx,
           w_in, w_out, hg_lb_logits, hg_norm_g, rg_conv_w, rg_conv_b, rg_w_r, rg_b_r, rg_w_i, rg_b_i,
           rg_lambda, mla_q_norm_g, mla_kv_norm_g, mla_w_uq, mla_w_uk, mla_w_uv, ada_w, ada_b,
           ln1_g, ln1_b, ln2_g, ln2_b, moe_router, moe_w_gate, moe_w_up, moe_w_down):
    x = jnp.concatenate([x_prompt.reshape(N_CTX, D_MODEL), x_sample.reshape(N_LAT, D_MODEL)], axis=0)
    cvec = jnp.concatenate([c_ctx[None, :], c, jnp.zeros((SUBLANES - N_MODROWS, D_MODEL), F32)], axis=0)
    mod = ada_mod(cvec, ada_w, ada_b)
    mod6 = mod[:, :N_MODROWS].reshape(DEPTH * N_MODROWS, 6, D_MODEL)
    cos_t, sin_t = rope_tables()
    vec = lambda a: a.reshape(DEPTH, 1, a.shape[-1])
    hg_ng, cb, qg, kg = vec(hg_norm_g), vec(rg_conv_b), vec(mla_q_norm_g), vec(mla_kv_norm_g)
    g1, b1, g2, b2 = vec(ln1_g), vec(ln1_b), vec(ln2_g), vec(ln2_b)
    w_r, w_i, w_in_b = rg_w_r.astype(BF16), rg_w_i.astype(BF16), w_in.astype(BF16)
    router = jnp.pad(moe_router, ((0, 0), (0, 0), (0, LANES - N_EXPERTS))).astype(BF16)

    ckvs, kpes, hgs, rgs = [], [], [], []
    for l in range(DEPTH):
        z_hg, z_rg, z_cq, z_kv = in_proj(x, mod6, w_in_b[l], prep_w_pe(w_in[l]), l)

        o_hg_f, o_hg_b, hg_fin_f, hg_fin_b = hgrn_mixer(z_hg, hg_lb_logits, state_hgrn, l)

        rg_args = (rg_conv_w, cb, w_r, rg_b_r, w_i, rg_b_i, rg_lambda)
        o_rg_c, rg_fin = rglru_mixer(z_rg, *rg_args, None, l, False)
        (o_rg_l,) = rglru_mixer(z_rg, *rg_args, state_rglru, l, True)
        o_rg = jnp.concatenate([o_rg_c, o_rg_l], axis=0)

        w_uk, w_uv = mla_w_uk[l].astype(BF16), mla_w_uv[l].astype(BF16)
        q = q_proj(z_cq, qg, prep_w_uq(mla_w_uq[l]), cos_t, sin_t, l)
        k_tok, v_tok, ckv_n, kpe = kv_proj(z_kv, kg, w_uk, w_uv, cos_t, sin_t, l)
        cache = jnp.concatenate([cache_mla_ckv[:, l].reshape(DEC_BATCH * PAST_LEN, KV_LORA),
                                 cache_mla_kpe[:, l].reshape(DEC_BATCH * PAST_LEN, MLA_ROPE_DIM),
                                 jnp.zeros((DEC_BATCH * PAST_LEN, MLA_ROPE_DIM), F32)], axis=1)
        k_cache, v_cache = kv_proj_cache(cache, kg, w_uk, w_uv, cos_t[:PAST_LEN], sin_t[:PAST_LEN], l)
        o_mla = jnp.concatenate([mla_attention(q, k_tok, v_tok, None, None, False),
                                 mla_attention(q, k_tok, v_tok, k_cache, v_cache, True)], axis=0)

        x1, hf, p_t = out_proj(o_hg_f, o_hg_b, z_hg, hg_ng, o_rg, o_mla, w_out[l].astype(BF16), x, mod6, g1, b1,
                               router[l], l)
        x = moe_block(x1, hf, p_t, mod6, g2, b2, moe_w_gate, moe_w_up, moe_w_down, l, split=(l == DEPTH - 1))

        ckvs.append(ckv_n[:N_CTX].reshape(BATCH, SEQ, KV_LORA))
        kpes.append(kpe[:N_CTX].reshape(BATCH, SEQ, MLA_ROPE_DIM))
        hgs.append(jnp.stack([hg_fin_f, hg_fin_b], axis=1))
        rgs.append(rg_fin)

    y_prompt = x[0].reshape(BATCH, SEQ, D_MODEL)
    y_sample = x[1].reshape(DEC_BATCH, DEC_SEQ, D_MODEL)
    return (y_prompt, y_sample, jnp.stack(ckvs, axis=1), jnp.stack(kpes, axis=1),
            jnp.stack(hgs, axis=1), jnp.stack(rgs, axis=1))
```

```python
import functools
import math

import jax
import jax.numpy as jnp
import numpy as np
from jax import lax
from jax.experimental import pallas as pl
from jax.experimental.pallas import tpu as pltpu

F32 = jnp.float32
BF16 = jnp.bfloat16

D_MODEL = 2048
BATCH = 16
SEQ = 256
DEPTH = 2
DEC_BATCH = 2
DEC_SEQ = 4096
PAST_LEN = 512
GRID_W = 64
HG_WIDTH = 512
HG_HEADS = 4
HG_DK = 128
RG_WIDTH = 512
RG_HEADS = 4
RG_BLOCK = 128
RG_CONV = 4
RG_C = 8.0
MLA_WIDTH = 1024
MLA_HEADS = 8
MLA_V_DIM = 128
MLA_NOPE_DIM = 128
MLA_ROPE_DIM = 64
MLA_QK_DIM = 192
Q_LORA = 512
KV_LORA = 512
ROPE_THETA = 10000.0
Q_BLOCK = 128
N_EXPERTS = 16
CAP_FACTOR = 2
D_FF_EXPERT = 1024
DN_ALPHA = (2.0 * DEPTH) ** 0.25
LN_EPS = 1e-5
RMS_EPS = 1e-6

N_CTX = BATCH * SEQ
N_LAT = DEC_BATCH * DEC_SEQ
N_TOK = N_CTX + N_LAT
N_MODROWS = 1 + DEC_BATCH

LANES = 128
SUBLANES = 8
VMEM_LIMIT_BYTES = 56 * 1024 * 1024

W_HG = 5 * HG_WIDTH
W_RG = 2 * RG_WIDTH
W_KV = KV_LORA + 2 * MLA_ROPE_DIM
IN_GROUPS = (W_HG, W_RG, Q_LORA, W_KV)
IN_COLS_OWN = sum(IN_GROUPS)


def _params(sem, vmem=VMEM_LIMIT_BYTES):
    return pltpu.CompilerParams(dimension_semantics=sem, vmem_limit_bytes=vmem)


def _mod_row(i, tm):
    n_ctx_tiles = N_CTX // tm
    per_batch = DEC_SEQ // tm
    return jnp.where(i < n_ctx_tiles, 0, 1 + (i - n_ctx_tiles) // per_batch)


ADA_TN = 1024


def _ada_kernel(c_ref, w_ref, b_ref, o_ref):
    c = c_ref[...]
    s = (c * jax.nn.sigmoid(c)).astype(BF16)
    o_ref[...] = jnp.dot(s, w_ref[...].astype(BF16), preferred_element_type=F32) + b_ref[...]


def ada_mod(cvec, ada_w, ada_b):
    ncol = 6 * D_MODEL
    return pl.pallas_call(
        _ada_kernel,
        grid=(DEPTH, ncol // ADA_TN),
        in_specs=[
            pl.BlockSpec((SUBLANES, D_MODEL), lambda l, j: (0, 0)),
            pl.BlockSpec((None, D_MODEL, ADA_TN), lambda l, j: (l, 0, j)),
            pl.BlockSpec((None, 1, ADA_TN), lambda l, j: (l, 0, j)),
        ],
        out_specs=pl.BlockSpec((None, SUBLANES, ADA_TN), lambda l, j: (l, 0, j)),
        out_shape=jax.ShapeDtypeStruct((DEPTH, SUBLANES, ncol), F32),
        compiler_params=_params(("arbitrary", "arbitrary")),
        name="ada_mod",
    )(cvec, ada_w, ada_b.reshape(DEPTH, 1, ncol))


INPROJ_TM = 256


def _inproj_kernel(x_ref, mod_ref, w_ref, wpe_ref, ohg_ref, org_ref, ocq_ref, okv_ref):
    m = mod_ref[...]
    hm = (x_ref[...] * (1.0 + m[1:2]) + m[0:1]).astype(BF16)
    a = 0
    for o_ref, width in zip((ohg_ref, org_ref, ocq_ref), IN_GROUPS[:3]):
        o_ref[...] = jnp.dot(hm, w_ref[:, a:a + width], preferred_element_type=F32)
        a += width
    okv_ref[:, 0:KV_LORA] = jnp.dot(hm, w_ref[:, a:a + KV_LORA], preferred_element_type=F32)
    okv_ref[:, KV_LORA:] = jnp.dot(hm, wpe_ref[...], preferred_element_type=F32)


def in_proj(x, mod6, w_in_l, w_pe_l, layer):
    tm = INPROJ_TM
    const = lambda a: pl.BlockSpec(a.shape, lambda i: (0, 0), pipeline_mode=pl.Buffered(1))
    return pl.pallas_call(
        _inproj_kernel,
        grid=(N_TOK // tm,),
        in_specs=[
            pl.BlockSpec((tm, D_MODEL), lambda i: (i, 0)),
            pl.BlockSpec((None, 6, D_MODEL), lambda i: (layer * N_MODROWS + _mod_row(i, tm), 0, 0)),
            const(w_in_l), const(w_pe_l),
        ],
        out_specs=[pl.BlockSpec((tm, w), lambda i: (i, 0)) for w in IN_GROUPS],
        out_shape=[jax.ShapeDtypeStruct((N_TOK, w), F32) for w in IN_GROUPS],
        compiler_params=_params(("arbitrary",)),
        name="in_proj",
    )(x, mod6, w_in_l, w_pe_l)


def _rot_cols(w):
    quarter = MLA_ROPE_DIM // 4
    j = np.arange(MLA_ROPE_DIM)
    first = (j % (2 * quarter)) < quarter
    src = np.where(first, j + quarter, j - quarter)
    sign = np.where(first, -1.0, 1.0).astype(np.float32)
    return w[..., src] * sign


def prep_w_pe(w_in_l):
    kpe = w_in_l[:, -MLA_ROPE_DIM:]
    return jnp.concatenate([kpe, _rot_cols(kpe)], axis=1).astype(BF16)


HG_CHUNK = 64
HG_SAFE_EXP = 80.0


def _silu(x):
    return x * jax.nn.sigmoid(x)


HG_TB = SEQ
HG_BLOCKS = N_TOK // HG_TB
HG_CTX_BLOCKS = N_CTX // HG_TB
HG_SEQ_BLOCKS = DEC_SEQ // HG_TB


def _hg_exact_att(q, k, cum):
    c = q.shape[0]
    lane = lax.broadcasted_iota(jnp.int32, (c, c), 1)
    row = lax.broadcasted_iota(jnp.int32, (c, 1), 0)

    def body(s_idx, att):
        sel = row == s_idx
        cum_s = jnp.sum(jnp.where(sel, cum, 0.0), axis=0, keepdims=True)
        k_s = jnp.sum(jnp.where(sel, k, 0.0), axis=0, keepdims=True)
        dec = jnp.exp(jnp.minimum(cum - cum_s, 0.0))
        col = jnp.sum(q * k_s * dec, axis=1, keepdims=True)
        return jnp.where(lane == s_idx, col, att)

    return lax.fori_loop(0, c, body, jnp.zeros((c, c), F32))


def _hgrn_kernel(qf_ref, ff_ref, vf_ref, qb_ref, fb_ref, vb_ref, lbl_ref, s0f_ref, s0b_ref,
                 of_ref, ob_ref, sf_ref, sb_ref, st_scr, *, layer):
    c = HG_CHUNK
    n_c = HG_TB // c
    heads = range(HG_HEADS)
    i = pl.program_id(0)
    blk = (i, HG_BLOCKS - 1 - i)
    is_ctx = tuple(b < HG_CTX_BLOCKS for b in blk)
    pos = tuple(lax.rem(b - HG_CTX_BLOCKS, HG_SEQ_BLOCKS) for b in blk)
    starts = (is_ctx[0] | (pos[0] == 0), is_ctx[1] | (pos[1] == HG_SEQ_BLOCKS - 1))

    for d, s0_ref in enumerate((s0f_ref, s0b_ref)):
        @pl.when(starts[d] & is_ctx[d])
        def _():
            st_scr[d] = jnp.zeros((HG_HEADS, HG_DK, HG_DK), F32)

        @pl.when(starts[d] & jnp.logical_not(is_ctx[d]))
        def _():
            for h in heads:
                st_scr[d, h] = s0_ref[h].T

    lg = lbl_ref[...]
    e = jnp.exp(lg - jnp.max(lg, axis=0, keepdims=True))
    sm = e / jnp.sum(e, axis=0, keepdims=True)
    lb = jnp.zeros_like(sm[0])
    for j in range(1, layer + 1):
        lb = lb + sm[j]

    r = lax.broadcasted_iota(jnp.int32, (c, c), 0)
    s = lax.broadcasted_iota(jnp.int32, (c, c), 1)
    masks = (r >= s, r <= s)
    marks = ((0, c // 2 - 1, c - 1), (c - 1, c // 2, 0))
    in_refs = ((qf_ref, ff_ref, vf_ref), (qb_ref, fb_ref, vb_ref))
    out_refs = (of_ref, ob_ref)

    work = ([], [])
    guard = jnp.float32(0.0)
    for d in range(2):
        q_ref, f_ref, v_ref = in_refs[d]
        lbd = lb[d:d + 1]
        first, mid, last = marks[d]
        for ci in (range(n_c) if d == 0 else reversed(range(n_c))):
            rows = slice(ci * c, (ci + 1) * c)
            f = lbd + (1.0 - lbd) * jax.nn.sigmoid(f_ref[rows, :])
            cum = jnp.dot(masks[d].astype(F32), jnp.log(f), precision=lax.Precision.HIGHEST,
                          preferred_element_type=F32)
            c_mid, c_last = cum[mid:mid + 1], cum[last:last + 1]
            guard = jnp.maximum(guard, jnp.max(jnp.maximum(cum[first:first + 1] - c_mid, c_mid - c_last)))
            work[d].append((rows, _silu(q_ref[rows, :]), 1.0 - f, v_ref[rows, :].astype(BF16), cum))

    nt = (((1,), (1,)), ((), ()))
    tn = (((0,), (0,)), ((), ()))

    def run(exact):
        for d in range(2):
            _, mid, last = marks[d]
            sts = [st_scr[d, h] for h in heads]
            for rows, q, k, v, cum in work[d]:
                c_mid, c_last = cum[mid:mid + 1], cum[last:last + 1]
                q_in = (q * jnp.exp(cum)).astype(BF16)
                k_end = (k * jnp.exp(c_last - cum)).astype(BF16)
                dec = jnp.exp(c_last)
                if not exact:
                    qt = (q * jnp.exp(cum - c_mid)).astype(BF16)
                    kt = (k * jnp.exp(c_mid - cum)).astype(BF16)
                outs = []
                for h in heads:
                    ls = slice(h * LANES, (h + 1) * LANES)
                    inter = lax.dot_general(q_in[:, ls], sts[h].astype(BF16), nt, preferred_element_type=F32)
                    if exact:
                        att = _hg_exact_att(q[:, ls], k[:, ls], cum[:, ls])
                    else:
                        att = lax.dot_general(qt[:, ls], kt[:, ls], nt, preferred_element_type=F32)
                    att = jnp.where(masks[d], att, 0.0).astype(BF16)
                    outs.append(inter + jnp.dot(att, v[:, ls], preferred_element_type=F32))
                    sts[h] = sts[h] * dec[:, ls] + lax.dot_general(v[:, ls], k_end[:, ls], tn,
                                                                   preferred_element_type=F32)
                out_refs[d][rows, :] = jnp.concatenate(outs, axis=1)
            for h in heads:
                st_scr[d, h] = sts[h]

    unsafe = guard > HG_SAFE_EXP
    pl.when(unsafe)(lambda: run(True))
    pl.when(jnp.logical_not(unsafe))(lambda: run(False))

    for d, fin_ref in enumerate((sf_ref, sb_ref)):
        @pl.when(is_ctx[d])
        def _():
            for h in heads:
                fin_ref[h] = st_scr[d, h].T


def hgrn_mixer(z_hg, lb_logits, state, layer):
    last = HG_BLOCKS - 1
    blocks = (lambda i: i, lambda i: last - i)
    lat_batch = lambda b: jnp.clip(lax.div(b - HG_CTX_BLOCKS, HG_SEQ_BLOCKS), 0, DEC_BATCH - 1)
    ctx_seq = lambda b: jnp.minimum(b, BATCH - 1)

    def col(d, group):
        return pl.BlockSpec((HG_TB, HG_WIDTH), lambda i: (blocks[d](i), group))

    def s0(d):
        return pl.BlockSpec((None, None, None, HG_HEADS, HG_DK, HG_DK),
                            lambda i: (lat_batch(blocks[d](i)), layer, d, 0, 0, 0))

    def fin(d):
        return pl.BlockSpec((None, HG_HEADS, HG_DK, HG_DK), lambda i: (ctx_seq(blocks[d](i)), 0, 0, 0))

    return pl.pallas_call(
        functools.partial(_hgrn_kernel, layer=layer),
        grid=(HG_BLOCKS,),
        in_specs=[col(0, 0), col(0, 1), col(0, 3), col(1, 0), col(1, 2), col(1, 3),
                  pl.BlockSpec((DEPTH, 2, HG_WIDTH), lambda i: (0, 0, 0)), s0(0), s0(1)],
        out_specs=[col(0, 0), col(1, 0), fin(0), fin(1)],
        out_shape=[jax.ShapeDtypeStruct((N_TOK, HG_WIDTH), F32), jax.ShapeDtypeStruct((N_TOK, HG_WIDTH), F32),
                   jax.ShapeDtypeStruct((BATCH, HG_HEADS, HG_DK, HG_DK), F32),
                   jax.ShapeDtypeStruct((BATCH, HG_HEADS, HG_DK, HG_DK), F32)],
        scratch_shapes=[pltpu.VMEM((2, HG_HEADS, HG_DK, HG_DK), F32)],
        compiler_params=_params(("arbitrary",)),
        name="hgrn",
    )(z_hg, z_hg, z_hg, z_hg, z_hg, z_hg, lb_logits, state, state)


RG_ROWS = 256
RG_PAD = SUBLANES
RG_SCAN_UNROLL = 8


def _tile_scan(a, b, reverse):
    row = lax.broadcasted_iota(jnp.int32, a.shape, 0)
    for sh in (1, 2, 4):
        if reverse:
            a_s, b_s = pltpu.roll(a, SUBLANES - sh, 0), pltpu.roll(b, SUBLANES - sh, 0)
            valid = row < SUBLANES - sh
        else:
            a_s, b_s = pltpu.roll(a, sh, 0), pltpu.roll(b, sh, 0)
            valid = row >= sh
        b = jnp.where(valid, a * b_s + b, b)
        a = jnp.where(valid, a * a_s, a)
    return a, b


def _rglru_kernel(*refs, seq_len, has_state):
    if has_state:
        (x_ref, y_ref, cw_ref, cb_ref, wr_ref, br_ref, wi_ref, bi_ref, lam_ref, h0_ref,
         o_ref, xp_scr, a_scr, b_scr) = refs
    else:
        (x_ref, y_ref, cw_ref, cb_ref, wr_ref, br_ref, wi_ref, bi_ref, lam_ref,
         o_ref, hfin_ref, xp_scr, a_scr, b_scr) = refs
    t = seq_len
    zeros = jnp.zeros((RG_PAD, LANES), F32)
    xp_scr[0:RG_PAD, :] = zeros
    xp_scr[RG_PAD + t:, :] = zeros
    xp_scr[RG_PAD:RG_PAD + t, :] = x_ref[...]

    cw = cw_ref[...]
    cb = cb_ref[...]
    lam = lam_ref[...]
    sp = jnp.maximum(-lam, 0.0) + jnp.log1p(jnp.exp(-jnp.abs(lam)))

    def gates(ci, _):
        r0 = pl.multiple_of(ci * RG_ROWS, RG_ROWS)
        xc = cb
        for j in range(RG_CONV):
            xc = xc + cw[j:j + 1] * xp_scr[pl.ds(r0 + RG_PAD - RG_CONV // 2 + j, RG_ROWS), :]
        rows = pl.ds(r0, RG_ROWS)
        xcb = xc.astype(BF16)
        for d in range(2):
            r = jax.nn.sigmoid(jnp.dot(xcb, wr_ref[d], preferred_element_type=F32) + br_ref[d])
            ig = jax.nn.sigmoid(jnp.dot(xcb, wi_ref[d], preferred_element_type=F32) + bi_ref[d])
            log_a = -RG_C * r * sp[d:d + 1]
            a_scr[d, rows, :] = jnp.exp(log_a)
            b_scr[d, rows, :] = jnp.sqrt(1.0 - jnp.exp(2.0 * log_a)) * (ig * xc)
        return 0

    lax.fori_loop(0, t // RG_ROWS, gates, 0)

    n_tiles = t // SUBLANES

    def scan(j, carry):
        h_f, h_b = carry
        rows_f = pl.ds(pl.multiple_of(j * SUBLANES, SUBLANES), SUBLANES)
        rows_b = pl.ds(pl.multiple_of((n_tiles - 1 - j) * SUBLANES, SUBLANES), SUBLANES)
        aa, bb = _tile_scan(a_scr[0, rows_f, :], b_scr[0, rows_f, :], False)
        hf = aa * h_f + bb
        b_scr[0, rows_f, :] = hf
        aa, bb = _tile_scan(a_scr[1, rows_b, :], b_scr[1, rows_b, :], True)
        hb = aa * h_b + bb
        b_scr[1, rows_b, :] = hb
        return hf[SUBLANES - 1:SUBLANES], hb[0:1]

    if has_state:
        h0 = h0_ref[...]
        init = (h0[0:1], h0[1:2])
    else:
        init = (jnp.zeros((1, LANES), F32), jnp.zeros((1, LANES), F32))
    h_f, h_b = lax.fori_loop(0, n_tiles, scan, init, unroll=RG_SCAN_UNROLL)
    if not has_state:
        hfin_ref[0:1, :] = h_f
        hfin_ref[1:2, :] = h_b
    o_ref[...] = (b_scr[0] + b_scr[1]) * jax.nn.gelu(y_ref[...], approximate=True)


def rglru_mixer(z_rg, conv_w, conv_b, w_r, b_r, w_i, b_i, lam, state, layer, latent):
    seq_len = DEC_SEQ if latent else SEQ
    n_seq = DEC_BATCH if latent else BATCH
    blk0 = N_CTX // seq_len if latent else 0
    h = RG_HEADS
    vec = lambda rows: pl.BlockSpec((None, rows, LANES), lambda b, hh: (layer, 0, hh))
    wspec = pl.BlockSpec((None, 2, None, RG_BLOCK, RG_BLOCK), lambda b, hh: (layer, 0, hh, 0, 0))
    bspec = pl.BlockSpec((None, 2, 1, LANES), lambda b, hh: (layer, 0, 0, hh))
    in_specs = [
        pl.BlockSpec((seq_len, LANES), lambda b, hh: (blk0 + b, hh)),
        pl.BlockSpec((seq_len, LANES), lambda b, hh: (blk0 + b, h + hh)),
        vec(RG_CONV), vec(1), wspec, bspec, wspec, bspec, vec(2),
    ]
    args = [z_rg, z_rg, conv_w, conv_b, w_r, b_r.reshape(DEPTH, 2, 1, RG_WIDTH), w_i,
            b_i.reshape(DEPTH, 2, 1, RG_WIDTH), lam]
    o_spec = pl.BlockSpec((seq_len, LANES), lambda b, hh: (b, hh))
    o_shape = jax.ShapeDtypeStruct((n_seq * seq_len, RG_WIDTH), F32)
    if latent:
        in_specs.append(pl.BlockSpec((None, None, 2, LANES), lambda b, hh: (b, layer, 0, hh)))
        args.append(state)
        out_specs, out_shape = [o_spec], [o_shape]
    else:
        out_specs = [o_spec, pl.BlockSpec((None, 2, LANES), lambda b, hh: (b, 0, hh))]
        out_shape = [o_shape, jax.ShapeDtypeStruct((BATCH, 2, RG_WIDTH), F32)]
    return pl.pallas_call(
        functools.partial(_rglru_kernel, seq_len=seq_len, has_state=latent),
        grid=(n_seq, h),
        in_specs=in_specs,
        out_specs=out_specs,
        out_shape=out_shape,
        scratch_shapes=[pltpu.VMEM((seq_len + 2 * RG_PAD, LANES), F32),
                        pltpu.VMEM((2, seq_len, LANES), F32), pltpu.VMEM((2, seq_len, LANES), F32)],
        compiler_params=_params(("arbitrary", "arbitrary")),
        name="rglru_lat" if latent else "rglru_ctx",
    )(*args)


MLA_HEAD_PAD = 2 * LANES
MLA_TM = 512
ATT_TQ = 256


def rope_tables():
    half = MLA_ROPE_DIM // 2
    t = np.arange(DEC_SEQ)
    row = (t // GRID_W).astype(np.float32)
    col = (t % GRID_W).astype(np.float32)
    inv = (ROPE_THETA ** (-np.arange(0, half, 2, dtype=np.float32) / half)).astype(np.float32)
    ar, ac = row[:, None] * inv, col[:, None] * inv
    cos = np.concatenate([np.cos(ar), np.cos(ar), np.cos(ac), np.cos(ac)], -1)
    sin = np.concatenate([np.sin(ar), np.sin(ar), np.sin(ac), np.sin(ac)], -1)
    pad = np.zeros((DEC_SEQ, LANES - MLA_ROPE_DIM), np.float32)
    cos_lat = np.tile(np.concatenate([cos, pad], -1), (DEC_BATCH, 1))
    sin_lat = np.tile(np.concatenate([sin, pad], -1), (DEC_BATCH, 1))
    cos_ctx = np.concatenate([np.ones((N_CTX, MLA_ROPE_DIM), np.float32), np.zeros((N_CTX, LANES - MLA_ROPE_DIM), np.float32)], -1)
    sin_ctx = np.zeros((N_CTX, LANES), np.float32)
    return (jnp.asarray(np.concatenate([cos_ctx, cos_lat], 0).astype(np.float32)),
            jnp.asarray(np.concatenate([sin_ctx, sin_lat], 0).astype(np.float32)))


def _rope_group(x, cos, sin):
    return x * cos + pltpu.roll(x, MLA_ROPE_DIM, 1) * sin


def _rms(x, g):
    return x * lax.rsqrt(jnp.mean(x * x, axis=-1, keepdims=True) + RMS_EPS) * g


def _qproj_kernel(cq_ref, g_ref, w_ref, cos_ref, sin_ref, q_ref):
    xn = _rms(cq_ref[...], g_ref[...]).astype(BF16)
    qm = jnp.dot(xn, w_ref[...], preferred_element_type=F32)
    cos, sin = cos_ref[...], sin_ref[...]
    for h in range(MLA_HEADS):
        a = h * MLA_HEAD_PAD
        q_ref[:, a:a + LANES] = qm[:, a:a + LANES].astype(BF16)
        q_ref[:, a + LANES:a + 2 * LANES] = _rope_group(qm[:, a + LANES:a + 2 * LANES], cos, sin).astype(BF16)


def q_proj(z_cq, q_norm_g, w_uq_l, cos_t, sin_t, layer):
    tm = MLA_TM
    width = MLA_HEADS * MLA_HEAD_PAD
    return pl.pallas_call(
        _qproj_kernel,
        grid=(N_TOK // tm,),
        in_specs=[
            pl.BlockSpec((tm, Q_LORA), lambda i: (i, 0)),
            pl.BlockSpec((None, 1, Q_LORA), lambda i: (layer, 0, 0)),
            pl.BlockSpec((Q_LORA, width), lambda i: (0, 0)),
            pl.BlockSpec((tm, LANES), lambda i: (i, 0)),
            pl.BlockSpec((tm, LANES), lambda i: (i, 0)),
        ],
        out_specs=pl.BlockSpec((tm, width), lambda i: (i, 0)),
        out_shape=jax.ShapeDtypeStruct((N_TOK, width), BF16),
        compiler_params=_params(("arbitrary",)),
        name="q_proj",
    )(z_cq, q_norm_g, w_uq_l, cos_t, sin_t)


def prep_w_uq(w_uq_l):
    w = w_uq_l.reshape(Q_LORA, MLA_HEADS, MLA_QK_DIM)
    pe = w[..., MLA_NOPE_DIM:]
    w = jnp.concatenate([w, _rot_cols(pe)], axis=-1)
    return w.reshape(Q_LORA, MLA_HEADS * MLA_HEAD_PAD).astype(BF16)


def _kvproj_kernel(zkv_ref, g_ref, wk_ref, wv_ref, cos_ref, sin_ref, *outs, normalize):
    if normalize:
        k_ref, v_ref, ckv_ref, kpe_ref = outs
    else:
        k_ref, v_ref = outs
    ckv = zkv_ref[:, 0:KV_LORA]
    if normalize:
        ckv = _rms(ckv, g_ref[...])
        ckv_ref[...] = ckv
    pe_group = zkv_ref[:, KV_LORA:KV_LORA + LANES]
    if normalize:
        kpe_ref[...] = pe_group[:, 0:MLA_ROPE_DIM]
    pe = _rope_group(pe_group, cos_ref[...], sin_ref[...]).astype(BF16)
    cb = ckv.astype(BF16)
    kn = jnp.dot(cb, wk_ref[...], preferred_element_type=F32)
    v_ref[...] = jnp.dot(cb, wv_ref[...], preferred_element_type=F32).astype(BF16)
    for h in range(MLA_HEADS):
        a = h * MLA_HEAD_PAD
        k_ref[:, a:a + LANES] = kn[:, h * LANES:(h + 1) * LANES].astype(BF16)
        k_ref[:, a + LANES:a + 2 * LANES] = pe


def kv_proj(z_kv, kv_norm_g, w_uk_l, w_uv_l, cos_t, sin_t, layer):
    tm = MLA_TM
    kw, vw = MLA_HEADS * MLA_HEAD_PAD, MLA_WIDTH
    return pl.pallas_call(
        functools.partial(_kvproj_kernel, normalize=True),
        grid=(N_TOK // tm,),
        in_specs=[
            pl.BlockSpec((tm, W_KV), lambda i: (i, 0)),
            pl.BlockSpec((None, 1, KV_LORA), lambda i: (layer, 0, 0)),
            pl.BlockSpec((KV_LORA, MLA_HEADS * MLA_NOPE_DIM), lambda i: (0, 0)),
            pl.BlockSpec((KV_LORA, MLA_WIDTH), lambda i: (0, 0)),
            pl.BlockSpec((tm, LANES), lambda i: (i, 0)),
            pl.BlockSpec((tm, LANES), lambda i: (i, 0)),
        ],
        out_specs=[
            pl.BlockSpec((tm, kw), lambda i: (i, 0)),
            pl.BlockSpec((tm, vw), lambda i: (i, 0)),
            pl.BlockSpec((tm, KV_LORA), lambda i: (i, 0)),
            pl.BlockSpec((tm, MLA_ROPE_DIM), lambda i: (i, 0)),
        ],
        out_shape=[
            jax.ShapeDtypeStruct((N_TOK, kw), BF16),
            jax.ShapeDtypeStruct((N_TOK, vw), BF16),
            jax.ShapeDtypeStruct((N_TOK, KV_LORA), F32),
            jax.ShapeDtypeStruct((N_TOK, MLA_ROPE_DIM), F32),
        ],
        compiler_params=_params(("arbitrary",)),
        name="kv_proj",
    )(z_kv, kv_norm_g, w_uk_l, w_uv_l, cos_t, sin_t)


def kv_proj_cache(cache_kv, kv_norm_g, w_uk_l, w_uv_l, cos_c, sin_c, layer):
    tm = PAST_LEN
    kw, vw = MLA_HEADS * MLA_HEAD_PAD, MLA_WIDTH
    n = DEC_BATCH * PAST_LEN
    return pl.pallas_call(
        functools.partial(_kvproj_kernel, normalize=False),
        grid=(DEC_BATCH,),
        in_specs=[
            pl.BlockSpec((tm, W_KV), lambda b: (b, 0)),
            pl.BlockSpec((None, 1, KV_LORA), lambda b: (layer, 0, 0)),
            pl.BlockSpec((KV_LORA, MLA_HEADS * MLA_NOPE_DIM), lambda b: (0, 0)),
            pl.BlockSpec((KV_LORA, MLA_WIDTH), lambda b: (0, 0)),
            pl.BlockSpec((tm, LANES), lambda b: (0, 0)),
            pl.BlockSpec((tm, LANES), lambda b: (0, 0)),
        ],
        out_specs=[pl.BlockSpec((tm, kw), lambda b: (b, 0)), pl.BlockSpec((tm, vw), lambda b: (b, 0))],
        out_shape=[jax.ShapeDtypeStruct((n, kw), BF16), jax.ShapeDtypeStruct((n, vw), BF16)],
        compiler_params=_params(("arbitrary",)),
        name="kv_proj_cache",
    )(cache_kv, kv_norm_g, w_uk_l, w_uv_l, cos_c, sin_c)


def _attn_kernel(q_ref, *refs, n_heads):
    o_ref = refs[-1]
    segs = [(refs[i], refs[i + 1]) for i in range(0, len(refs) - 1, 2)]
    nt = (((1,), (1,)), ((), ()))
    for h in range(n_heads):
        cols = slice(h * MLA_HEAD_PAD, (h + 1) * MLA_HEAD_PAD)
        vcols = slice(h * MLA_V_DIM, (h + 1) * MLA_V_DIM)
        q = q_ref[:, cols]
        scores = [lax.dot_general(q, k_ref[:, cols], nt, preferred_element_type=F32) for k_ref, _ in segs]
        m = scores[0].max(axis=-1, keepdims=True)
        for s in scores[1:]:
            m = jnp.maximum(m, s.max(axis=-1, keepdims=True))
        l, o = 0.0, 0.0
        for s, (_, v_ref) in zip(scores, segs):
            p = jnp.exp2((s - m) * (MLA_QK_DIM ** -0.5 * math.log2(math.e)))
            l = l + jnp.sum(p, axis=-1, keepdims=True)
            o = o + jnp.dot(p.astype(BF16), v_ref[:, vcols], preferred_element_type=F32)
        o_ref[:, vcols] = o / l


def mla_attention(q, k_tok, v_tok, k_cache, v_cache, latent):
    h = MLA_HEADS
    if latent:
        tq, n_heads = ATT_TQ, 1
        n_q = DEC_SEQ // tq
        grid = (DEC_BATCH, h, n_q)
        q_map = lambda b, hh, i: (N_CTX // tq + b * n_q + i, hh)
        kv_map = lambda b, hh, i: (N_CTX // DEC_SEQ + b, hh)
        o_map = lambda b, hh, i: (b * n_q + i, hh)
        in_specs = [
            pl.BlockSpec((tq, MLA_HEAD_PAD), q_map),
            pl.BlockSpec((DEC_SEQ, MLA_HEAD_PAD), kv_map),
            pl.BlockSpec((DEC_SEQ, MLA_V_DIM), kv_map),
            pl.BlockSpec((PAST_LEN, MLA_HEAD_PAD), lambda b, hh, i: (b, hh)),
            pl.BlockSpec((PAST_LEN, MLA_V_DIM), lambda b, hh, i: (b, hh)),
        ]
        args = [q, k_tok, v_tok, k_cache, v_cache]
        n_out = N_LAT
    else:
        tq, n_heads = SEQ, h
        grid = (BATCH,)
        o_map = lambda b: (b, 0)
        in_specs = [pl.BlockSpec((SEQ, h * MLA_HEAD_PAD), o_map)] * 2 + [pl.BlockSpec((SEQ, MLA_WIDTH), o_map)]
        args = [q, k_tok, v_tok]
        n_out = N_CTX
    return pl.pallas_call(
        functools.partial(_attn_kernel, n_heads=n_heads),
        grid=grid,
        in_specs=in_specs,
        out_specs=pl.BlockSpec((tq, n_heads * MLA_V_DIM), o_map),
        out_shape=jax.ShapeDtypeStruct((n_out, MLA_WIDTH), F32),
        compiler_params=_params(("arbitrary",) * len(grid)),
        name="mla_attn_lat" if latent else "mla_attn_ctx",
    )(*args)


OUT_TM = 256


def _layer_norm(y, g, b):
    mu = jnp.mean(y, axis=-1, keepdims=True)
    yc = y - mu
    var = jnp.mean(yc * yc, axis=-1, keepdims=True)
    return yc * lax.rsqrt(var + LN_EPS) * g + b


def _outproj_kernel(of_ref, ob_ref, hgg_ref, hgn_ref, org_ref, omla_ref, w_ref, x_ref, mod_ref, g_ref, b_ref, wr_ref,
                    x1_ref, hf_ref, pt_ref):
    o = of_ref[...] + ob_ref[...]
    heads = [o[:, h * HG_DK:(h + 1) * HG_DK] for h in range(HG_HEADS)]
    o = jnp.concatenate([oh * lax.rsqrt(jnp.mean(oh * oh, axis=-1, keepdims=True) + RMS_EPS) for oh in heads], axis=1)
    o_hg = o * hgn_ref[...] * _silu(hgg_ref[...])
    m = jnp.dot(o_hg.astype(BF16), w_ref[0:HG_WIDTH, :], preferred_element_type=F32)
    m += jnp.dot(org_ref[...].astype(BF16), w_ref[HG_WIDTH:HG_WIDTH + RG_WIDTH, :], preferred_element_type=F32)
    m += jnp.dot(omla_ref[...].astype(BF16), w_ref[HG_WIDTH + RG_WIDTH:, :], preferred_element_type=F32)
    md = mod_ref[...]
    x1 = _layer_norm(DN_ALPHA * x_ref[...] + md[2:3] * m, g_ref[...], b_ref[...])
    x1_ref[...] = x1
    hf = x1 * (1.0 + md[4:5]) + md[3:4]
    hf_ref[...] = hf
    logits = jnp.dot(hf.astype(BF16), wr_ref[...], preferred_element_type=F32)
    lane = lax.broadcasted_iota(jnp.int32, logits.shape, 1)
    logits = jnp.where(lane < N_EXPERTS, logits, -jnp.inf)
    e = jnp.exp(logits - jnp.max(logits, axis=-1, keepdims=True))
    p = e / jnp.sum(e, axis=-1, keepdims=True)
    pt_ref[...] = p.T[0:N_EXPERTS, :]


def out_proj(o_hg_f, o_hg_b, z_hg, hg_norm_g, o_rg, o_mla, w_out_l, x, mod6, ln_g, ln_b, w_router_l, layer):
    tm = OUT_TM
    row = lambda w: pl.BlockSpec((tm, w), lambda i: (i, 0))
    const = lambda shape: pl.BlockSpec(shape, lambda i: (0,) * len(shape), pipeline_mode=pl.Buffered(1))
    return pl.pallas_call(
        _outproj_kernel,
        grid=(N_TOK // tm,),
        in_specs=[
            row(HG_WIDTH), row(HG_WIDTH),
            pl.BlockSpec((tm, HG_WIDTH), lambda i: (i, 4)),
            pl.BlockSpec((None, 1, HG_WIDTH), lambda i: (layer, 0, 0)),
            row(RG_WIDTH), row(MLA_WIDTH),
            const((D_MODEL, D_MODEL)),
            row(D_MODEL),
            pl.BlockSpec((None, 6, D_MODEL), lambda i: (layer * N_MODROWS + _mod_row(i, tm), 0, 0)),
            pl.BlockSpec((None, 1, D_MODEL), lambda i: (layer, 0, 0)),
            pl.BlockSpec((None, 1, D_MODEL), lambda i: (layer, 0, 0)),
            const((D_MODEL, LANES)),
        ],
        out_specs=[row(D_MODEL), row(D_MODEL), pl.BlockSpec((N_EXPERTS, tm), lambda i: (0, i))],
        out_shape=[jax.ShapeDtypeStruct((N_TOK, D_MODEL), F32),
                   jax.ShapeDtypeStruct((N_TOK, D_MODEL), F32),
                   jax.ShapeDtypeStruct((N_EXPERTS, N_TOK), F32)],
        compiler_params=_params(("arbitrary",)),
        name="out_proj",
    )(o_hg_f, o_hg_b, z_hg, hg_norm_g, o_rg, o_mla, w_out_l, x, mod6, ln_g, ln_b, w_router_l)


CAP_CTX = CAP_FACTOR * N_CTX // N_EXPERTS
CAP_LAT = CAP_FACTOR * N_LAT // N_EXPERTS
ROUTE_SETS = ((0, N_CTX, CAP_CTX), (N_CTX, N_LAT, CAP_LAT))
SLOTS = CAP_CTX + CAP_LAT


def _route_thr_kernel(pt_ref, thr_ref, need_ref):
    for si, (start, n, cap) in enumerate(ROUTE_SETS):
        p = pt_ref[:, start:start + n]

        def count(mask):
            return jnp.sum(mask.astype(F32), axis=1, keepdims=True)

        def body(i, t):
            cand = t | jnp.left_shift(jnp.int32(1), 30 - i)
            return jnp.where(count(p >= pltpu.bitcast(cand, F32)) >= cap, cand, t)

        t = pltpu.bitcast(lax.fori_loop(0, 31, body, jnp.zeros((N_EXPERTS, 1), jnp.int32)), F32)
        need = cap - count(p > t)
        thr_ref[si] = jnp.broadcast_to(t, (N_EXPERTS, LANES))
        need_ref[si] = jnp.broadcast_to(need, (N_EXPERTS, LANES))


def route_threshold(p_t):
    n_sets = len(ROUTE_SETS)
    return pl.pallas_call(
        _route_thr_kernel,
        out_shape=[jax.ShapeDtypeStruct((n_sets, N_EXPERTS, LANES), F32),
                   jax.ShapeDtypeStruct((n_sets, N_EXPERTS, LANES), F32)],
        compiler_params=_params(None),
        name="route_threshold",
    )(p_t)


def _route_lists_kernel(p_ref, thr_ref, need_ref, lists_ref, cnt_ref, first_ref, sel_ref, *, nb, cap, tok0, row0):
    nt = (((1,), (1,)), ((), ()))
    r128 = lax.broadcasted_iota(jnp.int32, (LANES, LANES), 0)
    c128 = lax.broadcasted_iota(jnp.int32, (LANES, LANES), 1)
    incl = (r128 <= c128).astype(BF16)
    eye = r128 == c128
    rb = lax.broadcasted_iota(jnp.int32, (nb, nb), 0)
    cb = lax.broadcasted_iota(jnp.int32, (nb, nb), 1)
    below = (cb < rb).astype(BF16)
    incl_b = (rb <= cb).astype(BF16)
    ones8 = jnp.ones((SUBLANES, LANES), BF16)
    s_col = lax.broadcasted_iota(jnp.int32, (cap, 1), 0).astype(F32)
    lane = lax.broadcasted_iota(jnp.int32, (cap, LANES), 1)
    lane_f = lane.astype(F32)
    j_row = lax.broadcasted_iota(jnp.int32, (1, nb), 1).astype(F32)

    def dot(a, b):
        return jnp.dot(a, b, preferred_element_type=F32)

    def block_base(totals, unit):
        hi = jnp.floor(totals * (1.0 / unit))
        lo = totals - unit * hi
        bc = lambda a: jnp.broadcast_to(a, (nb, LANES)).astype(BF16)
        return unit * dot(below, bc(hi)) + dot(below, bc(lo))

    def choose(e, acc):
        p = p_ref[e]
        t = thr_ref[pl.ds(e, 1), :]
        need = need_ref[pl.ds(e, 1), :]
        eq = p == t
        eq_f = eq.astype(F32)
        eq_lp = dot(eq_f.astype(BF16), incl)
        tie_rank = eq_lp - eq_f + block_base(eq_lp[:, LANES - 1:LANES], 16.0)
        sel_f = jnp.where((p > t) | (eq & (tie_rank < need)), 1.0, 0.0)
        sel_ref[e] = sel_f
        return acc + sel_f

    cnt = lax.fori_loop(0, N_EXPERTS, choose, jnp.zeros((nb, LANES), F32))
    cnt_lp = dot(cnt.astype(BF16), incl)
    first = cnt_lp - cnt + block_base(cnt_lp[:, LANES - 1:LANES], 64.0) + row0
    first_hi = jnp.floor(first * (1.0 / LANES))
    first_lo = first - LANES * first_hi

    def expert(e, acc):
        p = p_ref[e]
        sel_f = sel_ref[e]
        sel_b = sel_f.astype(BF16)
        lp = dot(sel_b, incl)
        c_row = lax.dot_general(ones8, sel_b, nt, preferred_element_type=F32)
        incl_row = dot(c_row.astype(BF16), incl_b)[0:1]
        excl_row = incl_row - c_row[0:1]
        oh_j = jnp.where((s_col >= excl_row) & (s_col < incl_row), 1.0, 0.0)
        base_s = jnp.sum(oh_j * excl_row, axis=1, keepdims=True)
        j_s = jnp.sum(oh_j * j_row, axis=1, keepdims=True)
        oh_jb = oh_j.astype(BF16)
        lp_rows = dot(oh_jb, lp.astype(BF16))
        pos = jnp.sum(jnp.where(lp_rows <= s_col - base_s, 1.0, 0.0), axis=1, keepdims=True)
        oh_c = lane_f == pos
        p1 = p.astype(BF16)
        r1 = p - p1.astype(F32)
        p2 = r1.astype(BF16)
        p3 = (r1 - p2.astype(F32)).astype(BF16)
        p_rows = dot(oh_jb, p1) + dot(oh_jb, p2) + dot(oh_jb, p3)
        gate = jnp.sum(jnp.where(oh_c, p_rows, 0.0), axis=1, keepdims=True)
        rank_rows = dot(oh_jb, acc.astype(BF16))
        first_rows = LANES * dot(oh_jb, first_hi.astype(BF16)) + dot(oh_jb, first_lo.astype(BF16))
        dst = jnp.sum(jnp.where(oh_c, rank_rows + first_rows, 0.0), axis=1, keepdims=True)
        idx = tok0 + LANES * j_s + pos
        lists_ref[e] = jnp.where(lane == 0, idx, jnp.where(lane == 1, dst, jnp.where(lane == 2, gate, 0.0)))
        return acc + sel_f

    lax.fori_loop(0, N_EXPERTS, expert, jnp.zeros((nb, LANES), F32))

    ones_b = jnp.ones((LANES, LANES), BF16)

    def column(a, j):
        diag = jnp.where(eye, jnp.broadcast_to(a[j:j + 1, :], (LANES, LANES)), 0.0)
        return dot(diag.astype(BF16), ones_b)

    for j in range(nb):
        rows = slice(j * LANES, (j + 1) * LANES)
        cnt_ref[rows, :] = column(cnt, j)
        first_ref[rows, :] = LANES * column(first_hi, j) + column(first_lo, j)


def route_lists(p_blk, thr, need, set_index):
    tok0, n, cap = ROUTE_SETS[set_index]
    nb = n // LANES
    row0 = float(CAP_FACTOR * tok0)
    kern = functools.partial(_route_lists_kernel, nb=nb, cap=cap, tok0=tok0, row0=row0)
    return pl.pallas_call(
        kern,
        grid=(1,),
        in_specs=[
            pl.BlockSpec((N_EXPERTS, nb, LANES), lambda i: (0, 0, 0)),
            pl.BlockSpec((None, N_EXPERTS, LANES), lambda i: (set_index, 0, 0)),
            pl.BlockSpec((None, N_EXPERTS, LANES), lambda i: (set_index, 0, 0)),
        ],
        out_specs=[pl.BlockSpec((N_EXPERTS, cap, LANES), lambda i: (0, 0, 0)),
                   pl.BlockSpec((n, LANES), lambda i: (0, 0)),
                   pl.BlockSpec((n, LANES), lambda i: (0, 0))],
        out_shape=[jax.ShapeDtypeStruct((N_EXPERTS, cap, LANES), F32),
                   jax.ShapeDtypeStruct((n, LANES), F32),
                   jax.ShapeDtypeStruct((n, LANES), F32)],
        scratch_shapes=[pltpu.VMEM((N_EXPERTS, nb, LANES), F32)],
        compiler_params=_params(("arbitrary",)),
        name="route_lists_lat" if set_index else "route_lists_ctx",
    )(p_blk, thr, need)


FFN_TM = 512
FFN_FC = 256
FFN_ISSUE_UNROLL = 8


def _ffn_kernel(tok_ref, dst_ref, dstp_ref, hf_ref, lists_ref, wg_ref, wu_ref, wd_ref, yc_ref,
                wg_b, wu_b, wd_b, x_buf, y_buf, sem_in, sem_out):
    e, f = pl.program_id(0), pl.program_id(1)
    n_e, n_f = pl.num_programs(0), pl.num_programs(1)
    tm, fc = FFN_TM, FFN_FC
    n_tiles = SLOTS // tm

    def gather(r):
        slot = r % 2

        def issue(i, _):
            pltpu.make_async_copy(hf_ref.at[pl.ds(tok_ref[0, 0, r * tm + i], 1), :],
                                  x_buf.at[slot, pl.ds(i, 1), :], sem_in.at[slot]).start()
            return 0

        lax.fori_loop(0, tm, issue, 0, unroll=FFN_ISSUE_UNROLL)

    def scatter(r, idx_ref):
        slot = r % 2

        def issue(i, _):
            pltpu.make_async_copy(y_buf.at[slot, pl.ds(i, 1), :],
                                  yc_ref.at[pl.ds(idx_ref[0, 0, r * tm + i], 1), :], sem_out.at[slot]).start()
            return 0

        lax.fori_loop(0, tm, issue, 0, unroll=FFN_ISSUE_UNROLL)

    def wait_gather(slot):
        pltpu.make_async_copy(hf_ref.at[pl.ds(0, tm), :], x_buf.at[slot], sem_in.at[slot]).wait()

    def wait_scatter(slot):
        pltpu.make_async_copy(y_buf.at[slot], yc_ref.at[pl.ds(0, tm), :], sem_out.at[slot]).wait()

    assert n_tiles == 3
    for step, tile in ((0, 2), (1, 1)):
        @pl.when(f == step)
        def _():
            gather(step)
            pl.when(e > 0)(functools.partial(scatter, tile, dstp_ref))

    cols = pl.ds(pl.multiple_of(f * fc, fc), fc)
    wg_b[:, cols] = wg_ref[...].astype(BF16)
    wu_b[:, cols] = wu_ref[...].astype(BF16)
    wd_b[cols, :] = wd_ref[...].astype(BF16)

    @pl.when(f == n_f - 1)
    def _():
        for r in range(n_tiles):
            slot = r % 2
            if r == 1:
                gather(2)
            wait_gather(slot)
            x = x_buf[slot].astype(BF16)
            gate = jnp.dot(x, wg_b[...], preferred_element_type=F32)
            up = jnp.dot(x, wu_b[...], preferred_element_type=F32)
            y = jnp.dot((_silu(gate) * up).astype(BF16), wd_b[...], preferred_element_type=F32)
            if r == 2:
                wait_scatter(slot)
            else:
                pl.when(e > 0)(functools.partial(wait_scatter, slot))
            y_buf[slot] = y * lists_ref[r * tm:(r + 1) * tm, 2:3]
            if r == 0:
                scatter(0, dst_ref)

        @pl.when(e == n_e - 1)
        def _():
            scatter(1, dst_ref)
            scatter(2, dst_ref)
            wait_scatter(1)
            wait_scatter(0)


def expert_ffn(tok, dst, hf, lists, w_gate, w_up, w_down, layer):
    fc = FFN_FC
    dst3 = dst.reshape(N_EXPERTS, 1, SLOTS)
    idx_spec = pl.BlockSpec((1, 1, SLOTS), lambda e, f: (e, 0, 0), memory_space=pltpu.SMEM)
    return pl.pallas_call(
        _ffn_kernel,
        grid=(N_EXPERTS, D_FF_EXPERT // fc),
        in_specs=[
            idx_spec, idx_spec,
            pl.BlockSpec((1, 1, SLOTS), lambda e, f: (jnp.maximum(e - 1, 0), 0, 0), memory_space=pltpu.SMEM),
            pl.BlockSpec(memory_space=pl.ANY),
            pl.BlockSpec((None, SLOTS, LANES), lambda e, f: (e, 0, 0)),
            pl.BlockSpec((None, None, D_MODEL, fc), lambda e, f: (layer, e, 0, f)),
            pl.BlockSpec((None, None, D_MODEL, fc), lambda e, f: (layer, e, 0, f)),
            pl.BlockSpec((None, None, fc, D_MODEL), lambda e, f: (layer, e, f, 0)),
        ],
        out_specs=pl.BlockSpec(memory_space=pl.ANY),
        out_shape=jax.ShapeDtypeStruct((N_CHOICES, D_MODEL), F32),
        scratch_shapes=[pltpu.VMEM((D_MODEL, D_FF_EXPERT), BF16), pltpu.VMEM((D_MODEL, D_FF_EXPERT), BF16),
                        pltpu.VMEM((D_FF_EXPERT, D_MODEL), BF16),
                        pltpu.VMEM((2, FFN_TM, D_MODEL), F32), pltpu.VMEM((2, FFN_TM, D_MODEL), F32),
                        pltpu.SemaphoreType.DMA((2,)), pltpu.SemaphoreType.DMA((2,))],
        compiler_params=_params(("arbitrary", "arbitrary")),
        name="expert_ffn",
    )(tok.reshape(N_EXPERTS, 1, SLOTS), dst3, dst3, hf, lists, w_gate, w_up, w_down)


COMB_TM = 256
COMB_ROWS = 768
N_CHOICES = N_EXPERTS * SLOTS


def _combine_kernel(rows_ref, y_ref, cnt_ref, first_ref, x1_ref, mod_ref, g_ref, b_ref, *rest, split):
    if split:
        oc_ref, ol_ref, buf, sem = rest
    else:
        o_ref, buf, sem = rest
    i, n = pl.program_id(0), pl.num_programs(0)
    ln = COMB_ROWS

    def base(t):
        return lax.div(rows_ref[t], SUBLANES) * SUBLANES

    def start(t, k):
        return pl.multiple_of(jnp.minimum(base(t) + k * ln, N_CHOICES - ln), SUBLANES)

    def copy(t, k, slot):
        return pltpu.make_async_copy(y_ref.at[pl.ds(start(t, k), ln), :], buf.at[slot], sem.at[slot])

    slot = lax.rem(i, 2)

    @pl.when(i == 0)
    def _():
        copy(0, 0, 0).start()

    @pl.when(i + 1 < n)
    def _():
        copy(i + 1, 0, 1 - slot).start()

    copy(i, 0, slot).wait()
    first = first_ref[:, 0:1]
    last = first + cnt_ref[:, 0:1]
    col = lax.broadcasted_iota(jnp.int32, (1, ln), 1)

    def contribution(k, slot_k):
        row_id = (start(i, k) + col).astype(F32)
        mine = (row_id >= first) & (row_id < last) & (row_id >= (base(i) + k * ln).astype(F32))
        onehot = jnp.where(mine, 1.0, 0.0).astype(BF16)
        y = buf[slot_k]
        hi = y.astype(BF16)
        lo = (y - hi.astype(F32)).astype(BF16)
        return jnp.dot(onehot, hi, preferred_element_type=F32) + jnp.dot(onehot, lo, preferred_element_type=F32)

    def extra(k, acc):
        c = copy(i, k, 2)
        c.start()
        c.wait()
        return acc + contribution(k, 2)

    n_groups = lax.div(rows_ref[i + 1] - base(i) + ln - 1, ln)
    acc = lax.fori_loop(1, n_groups, extra, contribution(0, slot))
    g2 = mod_ref[...][5:6]
    out = _layer_norm(DN_ALPHA * x1_ref[...] + g2 * acc, g_ref[...], b_ref[...])
    if split:
        is_ctx = i < N_CTX // COMB_TM

        @pl.when(is_ctx)
        def _():
            oc_ref[...] = out

        @pl.when(jnp.logical_not(is_ctx))
        def _():
            ol_ref[...] = out
    else:
        o_ref[...] = out


def moe_combine(tile_rows, y_choices, cnt, first, x1, mod6, ln_g, ln_b, layer, split):
    tm = COMB_TM
    n_ctx_tiles = N_CTX // tm
    row = lambda w: pl.BlockSpec((tm, w), lambda i, t: (i, 0))
    if split:
        out_specs = [pl.BlockSpec((tm, D_MODEL), lambda i, t: (jnp.minimum(i, n_ctx_tiles - 1), 0)),
                     pl.BlockSpec((tm, D_MODEL), lambda i, t: (jnp.maximum(i - n_ctx_tiles, 0), 0))]
        out_shape = [jax.ShapeDtypeStruct((N_CTX, D_MODEL), F32), jax.ShapeDtypeStruct((N_LAT, D_MODEL), F32)]
    else:
        out_specs = row(D_MODEL)
        out_shape = jax.ShapeDtypeStruct((N_TOK, D_MODEL), F32)
    grid_spec = pltpu.PrefetchScalarGridSpec(
        num_scalar_prefetch=1,
        grid=(N_TOK // tm,),
        in_specs=[
            pl.BlockSpec(memory_space=pl.ANY),
            row(LANES), row(LANES), row(D_MODEL),
            pl.BlockSpec((None, 6, D_MODEL), lambda i, t: (layer * N_MODROWS + _mod_row(i, tm), 0, 0)),
            pl.BlockSpec((None, 1, D_MODEL), lambda i, t: (layer, 0, 0)),
            pl.BlockSpec((None, 1, D_MODEL), lambda i, t: (layer, 0, 0)),
        ],
        out_specs=out_specs,
        scratch_shapes=[pltpu.VMEM((3, COMB_ROWS, D_MODEL), F32), pltpu.SemaphoreType.DMA((3,))],
    )
    return pl.pallas_call(
        functools.partial(_combine_kernel, split=split),
        grid_spec=grid_spec,
        out_shape=out_shape,
        compiler_params=_params(("arbitrary",)),
        name="moe_combine",
    )(tile_rows, y_choices, cnt, first, x1, mod6, ln_g, ln_b)


def moe_block(x1, hf, p_t, mod6, ln_g, ln_b, w_gate, w_up, w_down, layer, split):
    thr, need = route_threshold(p_t)
    parts = []
    for si, (tok0, n, cap) in enumerate(ROUTE_SETS):
        p_blk = p_t[:, tok0:tok0 + n].reshape(N_EXPERTS, n // LANES, LANES)
        parts.append(route_lists(p_blk, thr, need, si))
    lists = jnp.concatenate([p[0] for p in parts], axis=1)
    cnt = jnp.concatenate([p[1] for p in parts], axis=0)
    first = jnp.concatenate([p[2] for p in parts], axis=0)
    tok = lists[:, :, 0].astype(jnp.int32)
    dst = lists[:, :, 1].astype(jnp.int32)
    y_choices = expert_ffn(tok, dst, hf, lists, w_gate, w_up, w_down, layer)
    tile_rows = jnp.concatenate([first[::COMB_TM, 0], jnp.full((1,), N_CHOICES, F32)]).astype(jnp.int32)
    return moe_combine(tile_rows, y_choices, cnt, first, x1, mod6, ln_g, ln_b, layer, split)


def kernel(x_prompt, x_sample, cache_mla_ckv, cache_mla_kpe, state_hgrn, state_rglru, c, c_ctx,
           w_in, w_out, hg_lb_logits, hg_norm_g, rg_conv_w, rg_conv_b, rg_w_r, rg_b_r, rg_w_i, rg_b_i,
           rg_lambda, mla_q_norm_g, mla_kv_norm_g, mla_w_uq, mla_w_uk, mla_w_uv, ada_w, ada_b,
           ln1_g, ln1_b, ln2_g, ln2_b, moe_router, moe_w_gate, moe_w_up, moe_w_down):
    x = jnp.concatenate([x_prompt.reshape(N_CTX, D_MODEL), x_sample.reshape(N_LAT, D_MODEL)], axis=0)
    cvec = jnp.concatenate([c_ctx[None, :], c, jnp.zeros((SUBLANES - N_MODROWS, D_MODEL), F32)], axis=0)
    mod = ada_mod(cvec, ada_w, ada_b)
    mod6 = mod[:, :N_MODROWS].reshape(DEPTH * N_MODROWS, 6, D_MODEL)
    cos_t, sin_t = rope_tables()
    vec = lambda a: a.reshape(DEPTH, 1, a.shape[-1])
    hg_ng, cb, qg, kg = vec(hg_norm_g), vec(rg_conv_b), vec(mla_q_norm_g), vec(mla_kv_norm_g)
    g1, b1, g2, b2 = vec(ln1_g), vec(ln1_b), vec(ln2_g), vec(ln2_b)
    w_r, w_i, w_in_b = rg_w_r.astype(BF16), rg_w_i.astype(BF16), w_in.astype(BF16)
    router = jnp.pad(moe_router, ((0, 0), (0, 0), (0, LANES - N_EXPERTS))).astype(BF16)

    ckvs, kpes, hgs, rgs = [], [], [], []
    for l in range(DEPTH):
        z_hg, z_rg, z_cq, z_kv = in_proj(x, mod6, w_in_b[l], prep_w_pe(w_in[l]), l)

        o_hg_f, o_hg_b, hg_fin_f, hg_fin_b = hgrn_mixer(z_hg, hg_lb_logits, state_hgrn, l)

        rg_args = (rg_conv_w, cb, w_r, rg_b_r, w_i, rg_b_i, rg_lambda)
        o_rg_c, rg_fin = rglru_mixer(z_rg, *rg_args, None, l, False)
        (o_rg_l,) = rglru_mixer(z_rg, *rg_args, state_rglru, l, True)
        o_rg = jnp.concatenate([o_rg_c, o_rg_l], axis=0)

        w_uk, w_uv = mla_w_uk[l].astype(BF16), mla_w_uv[l].astype(BF16)
        q = q_proj(z_cq, qg, prep_w_uq(mla_w_uq[l]), cos_t, sin_t, l)
        k_tok, v_tok, ckv_n, kpe = kv_proj(z_kv, kg, w_uk, w_uv, cos_t, sin_t, l)
        cache = jnp.concatenate([cache_mla_ckv[:, l].reshape(DEC_BATCH * PAST_LEN, KV_LORA),
                                 cache_mla_kpe[:, l].reshape(DEC_BATCH * PAST_LEN, MLA_ROPE_DIM),
                                 jnp.zeros((DEC_BATCH * PAST_LEN, MLA_ROPE_DIM), F32)], axis=1)
        k_cache, v_cache = kv_proj_cache(cache, kg, w_uk, w_uv, cos_t[:PAST_LEN], sin_t[:PAST_LEN], l)
        o_mla = jnp.concatenate([mla_attention(q, k_tok, v_tok, None, None, False),
                                 mla_attention(q, k_tok, v_tok, k_cache, v_cache, True)], axis=0)

        x1, hf, p_t = out_proj(o_hg_f, o_hg_b, z_hg, hg_ng, o_rg, o_mla, w_out[l].astype(BF16), x, mod6, g1, b1,
                               router[l], l)
        x = moe_block(x1, hf, p_t, mod6, g2, b2, moe_w_gate, moe_w_up, moe_w_down, l, split=(l == DEPTH - 1))

        ckvs.append(ckv_n[:N_CTX].reshape(BATCH, SEQ, KV_LORA))
        kpes.append(kpe[:N_CTX].reshape(BATCH, SEQ, MLA_ROPE_DIM))
        hgs.append(jnp.stack([hg_fin_f, hg_fin_b], axis=1))
        rgs.append(rg_fin)

    y_prompt = x[0].reshape(BATCH, SEQ, D_MODEL)
    y_sample = x[1].reshape(DEC_BATCH, DEC_SEQ, D_MODEL)
    return (y_prompt, y_sample, jnp.stack(ckvs, axis=1), jnp.stack(kpes, axis=1),
            jnp.stack(hgs, axis=1), jnp.stack(rgs, axis=1))
```

```python
import functools
import math

import jax
import jax.numpy as jnp
import numpy as np
from jax import lax
from jax.experimental import pallas as pl
from jax.experimental.pallas import tpu as pltpu

F32 = jnp.float32
BF16 = jnp.bfloat16

D_MODEL = 2048
BATCH = 16
SEQ = 256
DEPTH = 2
DEC_BATCH = 2
DEC_SEQ = 4096
PAST_LEN = 512
GRID_W = 64
HG_WIDTH = 512
HG_HEADS = 4
HG_DK = 128
RG_WIDTH = 512
RG_HEADS = 4
RG_BLOCK = 128
RG_CONV = 4
RG_C = 8.0
MLA_WIDTH = 1024
MLA_HEADS = 8
MLA_V_DIM = 128
MLA_NOPE_DIM = 128
MLA_ROPE_DIM = 64
MLA_QK_DIM = 192
Q_LORA = 512
KV_LORA = 512
ROPE_THETA = 10000.0
Q_BLOCK = 128
N_EXPERTS = 16
CAP_FACTOR = 2
D_FF_EXPERT = 1024
DN_ALPHA = (2.0 * DEPTH) ** 0.25
LN_EPS = 1e-5
RMS_EPS = 1e-6

N_CTX = BATCH * SEQ
N_LAT = DEC_BATCH * DEC_SEQ
N_TOK = N_CTX + N_LAT
N_MODROWS = 1 + DEC_BATCH

LANES = 128
SUBLANES = 8
VMEM_LIMIT_BYTES = 56 * 1024 * 1024

W_HG = 5 * HG_WIDTH
W_RG = 2 * RG_WIDTH
W_KV = KV_LORA + 2 * MLA_ROPE_DIM
IN_GROUPS = (W_HG, W_RG, Q_LORA, W_KV)
IN_COLS_OWN = sum(IN_GROUPS)


def _params(sem, vmem=VMEM_LIMIT_BYTES):
    return pltpu.CompilerParams(dimension_semantics=sem, vmem_limit_bytes=vmem)


def _mod_row(i, tm):
    n_ctx_tiles = N_CTX // tm
    per_batch = DEC_SEQ // tm
    return jnp.where(i < n_ctx_tiles, 0, 1 + (i - n_ctx_tiles) // per_batch)


ADA_TN = 1024


def _ada_kernel(c_ref, w_ref, b_ref, o_ref):
    c = c_ref[...]
    s = (c * jax.nn.sigmoid(c)).astype(BF16)
    o_ref[...] = jnp.dot(s, w_ref[...].astype(BF16), preferred_element_type=F32) + b_ref[...]


def ada_mod(cvec, ada_w, ada_b):
    ncol = 6 * D_MODEL
    return pl.pallas_call(
        _ada_kernel,
        grid=(DEPTH, ncol // ADA_TN),
        in_specs=[
            pl.BlockSpec((SUBLANES, D_MODEL), lambda l, j: (0, 0)),
            pl.BlockSpec((None, D_MODEL, ADA_TN), lambda l, j: (l, 0, j)),
            pl.BlockSpec((None, 1, ADA_TN), lambda l, j: (l, 0, j)),
        ],
        out_specs=pl.BlockSpec((None, SUBLANES, ADA_TN), lambda l, j: (l, 0, j)),
        out_shape=jax.ShapeDtypeStruct((DEPTH, SUBLANES, ncol), F32),
        compiler_params=_params(("arbitrary", "arbitrary")),
        name="ada_mod",
    )(cvec, ada_w, ada_b.reshape(DEPTH, 1, ncol))


INPROJ_TM = 256


def _inproj_kernel(x_ref, mod_ref, w_ref, wpe_ref, ohg_ref, org_ref, ocq_ref, okv_ref):
    m = mod_ref[...]
    hm = (x_ref[...] * (1.0 + m[1:2]) + m[0:1]).astype(BF16)
    a = 0
    for o_ref, width in zip((ohg_ref, org_ref, ocq_ref), IN_GROUPS[:3]):
        o_ref[...] = jnp.dot(hm, w_ref[:, a:a + width], preferred_element_type=F32)
        a += width
    okv_ref[:, 0:KV_LORA] = jnp.dot(hm, w_ref[:, a:a + KV_LORA], preferred_element_type=F32)
    okv_ref[:, KV_LORA:] = jnp.dot(hm, wpe_ref[...], preferred_element_type=F32)


def in_proj(x, mod6, w_in_l, w_pe_l, layer):
    tm = INPROJ_TM
    const = lambda a: pl.BlockSpec(a.shape, lambda i: (0, 0), pipeline_mode=pl.Buffered(1))
    return pl.pallas_call(
        _inproj_kernel,
        grid=(N_TOK // tm,),
        in_specs=[
            pl.BlockSpec((tm, D_MODEL), lambda i: (i, 0)),
            pl.BlockSpec((None, 6, D_MODEL), lambda i: (layer * N_MODROWS + _mod_row(i, tm), 0, 0)),
            const(w_in_l), const(w_pe_l),
        ],
        out_specs=[pl.BlockSpec((tm, w), lambda i: (i, 0)) for w in IN_GROUPS],
        out_shape=[jax.ShapeDtypeStruct((N_TOK, w), F32) for w in IN_GROUPS],
        compiler_params=_params(("arbitrary",)),
        name="in_proj",
    )(x, mod6, w_in_l, w_pe_l)


def _rot_cols(w):
    quarter = MLA_ROPE_DIM // 4
    j = np.arange(MLA_ROPE_DIM)
    first = (j % (2 * quarter)) < quarter
    src = np.where(first, j + quarter, j - quarter)
    sign = np.where(first, -1.0, 1.0).astype(np.float32)
    return w[..., src] * sign


def prep_w_pe(w_in_l):
    kpe = w_in_l[:, -MLA_ROPE_DIM:]
    return jnp.concatenate([kpe, _rot_cols(kpe)], axis=1).astype(BF16)


HG_CHUNK = 64
HG_SAFE_EXP = 80.0


def _silu(x):
    return x * jax.nn.sigmoid(x)


HG_TB = SEQ
HG_BLOCKS = N_TOK // HG_TB
HG_CTX_BLOCKS = N_CTX // HG_TB
HG_SEQ_BLOCKS = DEC_SEQ // HG_TB


def _hg_exact_att(q, k, cum):
    c = q.shape[0]
    lane = lax.broadcasted_iota(jnp.int32, (c, c), 1)
    row = lax.broadcasted_iota(jnp.int32, (c, 1), 0)

    def body(s_idx, att):
        sel = row == s_idx
        cum_s = jnp.sum(jnp.where(sel, cum, 0.0), axis=0, keepdims=True)
        k_s = jnp.sum(jnp.where(sel, k, 0.0), axis=0, keepdims=True)
        dec = jnp.exp(jnp.minimum(cum - cum_s, 0.0))
        col = jnp.sum(q * k_s * dec, axis=1, keepdims=True)
        return jnp.where(lane == s_idx, col, att)

    return lax.fori_loop(0, c, body, jnp.zeros((c, c), F32))


def _hgrn_kernel(qf_ref, ff_ref, vf_ref, qb_ref, fb_ref, vb_ref, lbl_ref, s0f_ref, s0b_ref,
                 of_ref, ob_ref, sf_ref, sb_ref, st_scr, *, layer):
    c = HG_CHUNK
    n_c = HG_TB // c
    heads = range(HG_HEADS)
    i = pl.program_id(0)
    blk = (i, HG_BLOCKS - 1 - i)
    is_ctx = tuple(b < HG_CTX_BLOCKS for b in blk)
    pos = tuple(lax.rem(b - HG_CTX_BLOCKS, HG_SEQ_BLOCKS) for b in blk)
    starts = (is_ctx[0] | (pos[0] == 0), is_ctx[1] | (pos[1] == HG_SEQ_BLOCKS - 1))

    for d, s0_ref in enumerate((s0f_ref, s0b_ref)):
        @pl.when(starts[d] & is_ctx[d])
        def _():
            st_scr[d] = jnp.zeros((HG_HEADS, HG_DK, HG_DK), F32)

        @pl.when(starts[d] & jnp.logical_not(is_ctx[d]))
        def _():
            for h in heads:
                st_scr[d, h] = s0_ref[h].T

    lg = lbl_ref[...]
    e = jnp.exp(lg - jnp.max(lg, axis=0, keepdims=True))
    sm = e / jnp.sum(e, axis=0, keepdims=True)
    lb = jnp.zeros_like(sm[0])
    for j in range(1, layer + 1):
        lb = lb + sm[j]

    r = lax.broadcasted_iota(jnp.int32, (c, c), 0)
    s = lax.broadcasted_iota(jnp.int32, (c, c), 1)
    masks = (r >= s, r <= s)
    marks = ((0, c // 2 - 1, c - 1), (c - 1, c // 2, 0))
    in_refs = ((qf_ref, ff_ref, vf_ref), (qb_ref, fb_ref, vb_ref))
    out_refs = (of_ref, ob_ref)

    work = ([], [])
    guard = jnp.float32(0.0)
    for d in range(2):
        q_ref, f_ref, v_ref = in_refs[d]
        lbd = lb[d:d + 1]
        first, mid, last = marks[d]
        for ci in (range(n_c) if d == 0 else reversed(range(n_c))):
            rows = slice(ci * c, (ci + 1) * c)
            f = lbd + (1.0 - lbd) * jax.nn.sigmoid(f_ref[rows, :])
            cum = jnp.dot(masks[d].astype(F32), jnp.log(f), precision=lax.Precision.HIGHEST,
                          preferred_element_type=F32)
            c_mid, c_last = cum[mid:mid + 1], cum[last:last + 1]
            guard = jnp.maximum(guard, jnp.max(jnp.maximum(cum[first:first + 1] - c_mid, c_mid - c_last)))
            work[d].append((rows, _silu(q_ref[rows, :]), 1.0 - f, v_ref[rows, :].astype(BF16), cum))

    nt = (((1,), (1,)), ((), ()))
    tn = (((0,), (0,)), ((), ()))

    def run(exact):
        for d in range(2):
            _, mid, last = marks[d]
            sts = [st_scr[d, h] for h in heads]
            for rows, q, k, v, cum in work[d]:
                c_mid, c_last = cum[mid:mid + 1], cum[last:last + 1]
                q_in = (q * jnp.exp(cum)).astype(BF16)
                k_end = (k * jnp.exp(c_last - cum)).astype(BF16)
                dec = jnp.exp(c_last)
                if not exact:
                    qt = (q * jnp.exp(cum - c_mid)).astype(BF16)
                    kt = (k * jnp.exp(c_mid - cum)).astype(BF16)
                outs = []
                for h in heads:
                    ls = slice(h * LANES, (h + 1) * LANES)
                    inter = lax.dot_general(q_in[:, ls], sts[h].astype(BF16), nt, preferred_element_type=F32)
                    if exact:
                        att = _hg_exact_att(q[:, ls], k[:, ls], cum[:, ls])
                    else:
                        att = lax.dot_general(qt[:, ls], kt[:, ls], nt, preferred_element_type=F32)
                    att = jnp.where(masks[d], att, 0.0).astype(BF16)
                    outs.append(inter + jnp.dot(att, v[:, ls], preferred_element_type=F32))
                    sts[h] = sts[h] * dec[:, ls] + lax.dot_general(v[:, ls], k_end[:, ls], tn,
                                                                   preferred_element_type=F32)
                out_refs[d][rows, :] = jnp.concatenate(outs, axis=1)
            for h in heads:
                st_scr[d, h] = sts[h]

    unsafe = guard > HG_SAFE_EXP
    pl.when(unsafe)(lambda: run(True))
    pl.when(jnp.logical_not(unsafe))(lambda: run(False))

    for d, fin_ref in enumerate((sf_ref, sb_ref)):
        @pl.when(is_ctx[d])
        def _():
            for h in heads:
                fin_ref[h] = st_scr[d, h].T


def hgrn_mixer(z_hg, lb_logits, state, layer):
    last = HG_BLOCKS - 1
    blocks = (lambda i: i, lambda i: last - i)
    lat_batch = lambda b: jnp.clip(lax.div(b - HG_CTX_BLOCKS, HG_SEQ_BLOCKS), 0, DEC_BATCH - 1)
    ctx_seq = lambda b: jnp.minimum(b, BATCH - 1)

    def col(d, group):
        return pl.BlockSpec((HG_TB, HG_WIDTH), lambda i: (blocks[d](i), group))

    def s0(d):
        return pl.BlockSpec((None, None, None, HG_HEADS, HG_DK, HG_DK),
                            lambda i: (lat_batch(blocks[d](i)), layer, d, 0, 0, 0))

    def fin(d):
        return pl.BlockSpec((None, HG_HEADS, HG_DK, HG_DK), lambda i: (ctx_seq(blocks[d](i)), 0, 0, 0))

    return pl.pallas_call(
        functools.partial(_hgrn_kernel, layer=layer),
        grid=(HG_BLOCKS,),
        in_specs=[col(0, 0), col(0, 1), col(0, 3), col(1, 0), col(1, 2), col(1, 3),
                  pl.BlockSpec((DEPTH, 2, HG_WIDTH), lambda i: (0, 0, 0)), s0(0), s0(1)],
        out_specs=[col(0, 0), col(1, 0), fin(0), fin(1)],
        out_shape=[jax.ShapeDtypeStruct((N_TOK, HG_WIDTH), F32), jax.ShapeDtypeStruct((N_TOK, HG_WIDTH), F32),
                   jax.ShapeDtypeStruct((BATCH, HG_HEADS, HG_DK, HG_DK), F32),
                   jax.ShapeDtypeStruct((BATCH, HG_HEADS, HG_DK, HG_DK), F32)],
        scratch_shapes=[pltpu.VMEM((2, HG_HEADS, HG_DK, HG_DK), F32)],
        compiler_params=_params(("arbitrary",)),
        name="hgrn",
    )(z_hg, z_hg, z_hg, z_hg, z_hg, z_hg, lb_logits, state, state)


RG_ROWS = 256
RG_PAD = SUBLANES
RG_SCAN_UNROLL = 8


def _tile_scan(a, b, reverse):
    row = lax.broadcasted_iota(jnp.int32, a.shape, 0)
    for sh in (1, 2, 4):
        if reverse:
            a_s, b_s = pltpu.roll(a, SUBLANES - sh, 0), pltpu.roll(b, SUBLANES - sh, 0)
            valid = row < SUBLANES - sh
        else:
            a_s, b_s = pltpu.roll(a, sh, 0), pltpu.roll(b, sh, 0)
            valid = row >= sh
        b = jnp.where(valid, a * b_s + b, b)
        a = jnp.where(valid, a * a_s, a)
    return a, b


def _rglru_kernel(*refs, seq_len, has_state):
    if has_state:
        (x_ref, y_ref, cw_ref, cb_ref, wr_ref, br_ref, wi_ref, bi_ref, lam_ref, h0_ref,
         o_ref, xp_scr, a_scr, b_scr, edge_scr, cin_scr) = refs
    else:
        (x_ref, y_ref, cw_ref, cb_ref, wr_ref, br_ref, wi_ref, bi_ref, lam_ref,
         o_ref, hfin_ref, xp_scr, a_scr, b_scr, edge_scr, cin_scr) = refs
    t = seq_len
    zeros = jnp.zeros((RG_PAD, LANES), F32)
    xp_scr[0:RG_PAD, :] = zeros
    xp_scr[RG_PAD + t:, :] = zeros
    xp_scr[RG_PAD:RG_PAD + t, :] = x_ref[...]

    cw = cw_ref[...]
    cb = cb_ref[...]
    lam = lam_ref[...]
    sp = jnp.maximum(-lam, 0.0) + jnp.log1p(jnp.exp(-jnp.abs(lam)))

    def gates(ci, _):
        r0 = pl.multiple_of(ci * RG_ROWS, RG_ROWS)
        xc = cb
        for j in range(RG_CONV):
            xc = xc + cw[j:j + 1] * xp_scr[pl.ds(r0 + RG_PAD - RG_CONV // 2 + j, RG_ROWS), :]
        rows = pl.ds(r0, RG_ROWS)
        xcb = xc.astype(BF16)
        for d in range(2):
            r = jax.nn.sigmoid(jnp.dot(xcb, wr_ref[d], preferred_element_type=F32) + br_ref[d])
            ig = jax.nn.sigmoid(jnp.dot(xcb, wi_ref[d], preferred_element_type=F32) + bi_ref[d])
            log_a = -RG_C * r * sp[d:d + 1]
            a_scr[d, rows, :] = jnp.exp(log_a)
            b_scr[d, rows, :] = jnp.sqrt(1.0 - jnp.exp(2.0 * log_a)) * (ig * xc)
        return 0

    lax.fori_loop(0, t // RG_ROWS, gates, 0)

    n_tiles = t // SUBLANES

    def tile_rows(j):
        return pl.ds(pl.multiple_of(j * SUBLANES, SUBLANES), SUBLANES)

    def local(j, _):
        rows = tile_rows(j)
        for d in range(2):
            aa, bb = _tile_scan(a_scr[d, rows, :], b_scr[d, rows, :], d == 1)
            a_scr[d, rows, :] = aa
            b_scr[d, rows, :] = bb
            row = 0 if d == 1 else SUBLANES - 1
            edge_scr[d, 0, j] = aa[row:row + 1]
            edge_scr[d, 1, j] = bb[row:row + 1]
        return 0

    lax.fori_loop(0, n_tiles, local, 0, unroll=RG_SCAN_UNROLL)

    def carry(j, c):
        jb = n_tiles - 1 - j
        cin_scr[0, j] = c[0]
        cin_scr[1, jb] = c[1]
        return (edge_scr[0, 0, j] * c[0] + edge_scr[0, 1, j], edge_scr[1, 0, jb] * c[1] + edge_scr[1, 1, jb])

    if has_state:
        h0 = h0_ref[...]
        init = (h0[0:1], h0[1:2])
    else:
        init = (jnp.zeros((1, LANES), F32), jnp.zeros((1, LANES), F32))
    h_f, h_b = lax.fori_loop(0, n_tiles, carry, init, unroll=RG_SCAN_UNROLL)

    def apply(j, _):
        rows = tile_rows(j)
        for d in range(2):
            b_scr[d, rows, :] = a_scr[d, rows, :] * cin_scr[d, j] + b_scr[d, rows, :]
        return 0

    lax.fori_loop(0, n_tiles, apply, 0, unroll=RG_SCAN_UNROLL)
    if not has_state:
        hfin_ref[0:1, :] = h_f
        hfin_ref[1:2, :] = h_b
    o_ref[...] = (b_scr[0] + b_scr[1]) * jax.nn.gelu(y_ref[...], approximate=True)


def rglru_mixer(z_rg, conv_w, conv_b, w_r, b_r, w_i, b_i, lam, state, layer, latent):
    seq_len = DEC_SEQ if latent else SEQ
    n_seq = DEC_BATCH if latent else BATCH
    blk0 = N_CTX // seq_len if latent else 0
    h = RG_HEADS
    vec = lambda rows: pl.BlockSpec((None, rows, LANES), lambda b, hh: (layer, 0, hh))
    wspec = pl.BlockSpec((None, 2, None, RG_BLOCK, RG_BLOCK), lambda b, hh: (layer, 0, hh, 0, 0))
    bspec = pl.BlockSpec((None, 2, 1, LANES), lambda b, hh: (layer, 0, 0, hh))
    in_specs = [
        pl.BlockSpec((seq_len, LANES), lambda b, hh: (blk0 + b, hh)),
        pl.BlockSpec((seq_len, LANES), lambda b, hh: (blk0 + b, h + hh)),
        vec(RG_CONV), vec(1), wspec, bspec, wspec, bspec, vec(2),
    ]
    args = [z_rg, z_rg, conv_w, conv_b, w_r, b_r.reshape(DEPTH, 2, 1, RG_WIDTH), w_i,
            b_i.reshape(DEPTH, 2, 1, RG_WIDTH), lam]
    o_spec = pl.BlockSpec((seq_len, LANES), lambda b, hh: (b, hh))
    o_shape = jax.ShapeDtypeStruct((n_seq * seq_len, RG_WIDTH), F32)
    if latent:
        in_specs.append(pl.BlockSpec((None, None, 2, LANES), lambda b, hh: (b, layer, 0, hh)))
        args.append(state)
        out_specs, out_shape = [o_spec], [o_shape]
    else:
        out_specs = [o_spec, pl.BlockSpec((None, 2, LANES), lambda b, hh: (b, 0, hh))]
        out_shape = [o_shape, jax.ShapeDtypeStruct((BATCH, 2, RG_WIDTH), F32)]
    return pl.pallas_call(
        functools.partial(_rglru_kernel, seq_len=seq_len, has_state=latent),
        grid=(n_seq, h),
        in_specs=in_specs,
        out_specs=out_specs,
        out_shape=out_shape,
        scratch_shapes=[pltpu.VMEM((seq_len + 2 * RG_PAD, LANES), F32),
                        pltpu.VMEM((2, seq_len, LANES), F32), pltpu.VMEM((2, seq_len, LANES), F32),
                        pltpu.VMEM((2, 2, seq_len // SUBLANES, 1, LANES), F32),
                        pltpu.VMEM((2, seq_len // SUBLANES, 1, LANES), F32)],
        compiler_params=_params(("arbitrary", "arbitrary")),
        name="rglru_lat" if latent else "rglru_ctx",
    )(*args)


MLA_HEAD_PAD = 2 * LANES
MLA_TM = 512
ATT_TQ = 256


def rope_tables():
    half = MLA_ROPE_DIM // 2
    t = np.arange(DEC_SEQ)
    row = (t // GRID_W).astype(np.float32)
    col = (t % GRID_W).astype(np.float32)
    inv = (ROPE_THETA ** (-np.arange(0, half, 2, dtype=np.float32) / half)).astype(np.float32)
    ar, ac = row[:, None] * inv, col[:, None] * inv
    cos = np.concatenate([np.cos(ar), np.cos(ar), np.cos(ac), np.cos(ac)], -1)
    sin = np.concatenate([np.sin(ar), np.sin(ar), np.sin(ac), np.sin(ac)], -1)
    pad = np.zeros((DEC_SEQ, LANES - MLA_ROPE_DIM), np.float32)
    cos_lat = np.tile(np.concatenate([cos, pad], -1), (DEC_BATCH, 1))
    sin_lat = np.tile(np.concatenate([sin, pad], -1), (DEC_BATCH, 1))
    cos_ctx = np.concatenate([np.ones((N_CTX, MLA_ROPE_DIM), np.float32), np.zeros((N_CTX, LANES - MLA_ROPE_DIM), np.float32)], -1)
    sin_ctx = np.zeros((N_CTX, LANES), np.float32)
    return (jnp.asarray(np.concatenate([cos_ctx, cos_lat], 0).astype(np.float32)),
            jnp.asarray(np.concatenate([sin_ctx, sin_lat], 0).astype(np.float32)))


def _rope_group(x, cos, sin):
    return x * cos + pltpu.roll(x, MLA_ROPE_DIM, 1) * sin


def _rms(x, g):
    return x * lax.rsqrt(jnp.mean(x * x, axis=-1, keepdims=True) + RMS_EPS) * g


def _qproj_kernel(cq_ref, g_ref, w_ref, cos_ref, sin_ref, q_ref):
    xn = _rms(cq_ref[...], g_ref[...]).astype(BF16)
    qm = jnp.dot(xn, w_ref[...], preferred_element_type=F32)
    cos, sin = cos_ref[...], sin_ref[...]
    for h in range(MLA_HEADS):
        a = h * MLA_HEAD_PAD
        q_ref[:, a:a + LANES] = qm[:, a:a + LANES].astype(BF16)
        q_ref[:, a + LANES:a + 2 * LANES] = _rope_group(qm[:, a + LANES:a + 2 * LANES], cos, sin).astype(BF16)


def q_proj(z_cq, q_norm_g, w_uq_l, cos_t, sin_t, layer):
    tm = MLA_TM
    width = MLA_HEADS * MLA_HEAD_PAD
    return pl.pallas_call(
        _qproj_kernel,
        grid=(N_TOK // tm,),
        in_specs=[
            pl.BlockSpec((tm, Q_LORA), lambda i: (i, 0)),
            pl.BlockSpec((None, 1, Q_LORA), lambda i: (layer, 0, 0)),
            pl.BlockSpec((Q_LORA, width), lambda i: (0, 0)),
            pl.BlockSpec((tm, LANES), lambda i: (i, 0)),
            pl.BlockSpec((tm, LANES), lambda i: (i, 0)),
        ],
        out_specs=pl.BlockSpec((tm, width), lambda i: (i, 0)),
        out_shape=jax.ShapeDtypeStruct((N_TOK, width), BF16),
        compiler_params=_params(("arbitrary",)),
        name="q_proj",
    )(z_cq, q_norm_g, w_uq_l, cos_t, sin_t)


def prep_w_uq(w_uq_l):
    w = w_uq_l.reshape(Q_LORA, MLA_HEADS, MLA_QK_DIM)
    pe = w[..., MLA_NOPE_DIM:]
    w = jnp.concatenate([w, _rot_cols(pe)], axis=-1)
    return w.reshape(Q_LORA, MLA_HEADS * MLA_HEAD_PAD).astype(BF16)


def _kvproj_kernel(zkv_ref, g_ref, wk_ref, wv_ref, cos_ref, sin_ref, *outs, normalize):
    if normalize:
        k_ref, v_ref, ckv_ref, kpe_ref = outs
    else:
        k_ref, v_ref = outs
    ckv = zkv_ref[:, 0:KV_LORA]
    if normalize:
        ckv = _rms(ckv, g_ref[...])
        ckv_ref[...] = ckv
    pe_group = zkv_ref[:, KV_LORA:KV_LORA + LANES]
    if normalize:
        kpe_ref[...] = pe_group[:, 0:MLA_ROPE_DIM]
    pe = _rope_group(pe_group, cos_ref[...], sin_ref[...]).astype(BF16)
    cb = ckv.astype(BF16)
    kn = jnp.dot(cb, wk_ref[...], preferred_element_type=F32)
    v_ref[...] = jnp.dot(cb, wv_ref[...], preferred_element_type=F32).astype(BF16)
    for h in range(MLA_HEADS):
        a = h * MLA_HEAD_PAD
        k_ref[:, a:a + LANES] = kn[:, h * LANES:(h + 1) * LANES].astype(BF16)
        k_ref[:, a + LANES:a + 2 * LANES] = pe


def kv_proj(z_kv, kv_norm_g, w_uk_l, w_uv_l, cos_t, sin_t, layer):
    tm = MLA_TM
    kw, vw = MLA_HEADS * MLA_HEAD_PAD, MLA_WIDTH
    return pl.pallas_call(
        functools.partial(_kvproj_kernel, normalize=True),
        grid=(N_TOK // tm,),
        in_specs=[
            pl.BlockSpec((tm, W_KV), lambda i: (i, 0)),
            pl.BlockSpec((None, 1, KV_LORA), lambda i: (layer, 0, 0)),
            pl.BlockSpec((KV_LORA, MLA_HEADS * MLA_NOPE_DIM), lambda i: (0, 0)),
            pl.BlockSpec((KV_LORA, MLA_WIDTH), lambda i: (0, 0)),
            pl.BlockSpec((tm, LANES), lambda i: (i, 0)),
            pl.BlockSpec((tm, LANES), lambda i: (i, 0)),
        ],
        out_specs=[
            pl.BlockSpec((tm, kw), lambda i: (i, 0)),
            pl.BlockSpec((tm, vw), lambda i: (i, 0)),
            pl.BlockSpec((tm, KV_LORA), lambda i: (i, 0)),
            pl.BlockSpec((tm, MLA_ROPE_DIM), lambda i: (i, 0)),
        ],
        out_shape=[
            jax.ShapeDtypeStruct((N_TOK, kw), BF16),
            jax.ShapeDtypeStruct((N_TOK, vw), BF16),
            jax.ShapeDtypeStruct((N_TOK, KV_LORA), F32),
            jax.ShapeDtypeStruct((N_TOK, MLA_ROPE_DIM), F32),
        ],
        compiler_params=_params(("arbitrary",)),
        name="kv_proj",
    )(z_kv, kv_norm_g, w_uk_l, w_uv_l, cos_t, sin_t)


def kv_proj_cache(cache_kv, kv_norm_g, w_uk_l, w_uv_l, cos_c, sin_c, layer):
    tm = PAST_LEN
    kw, vw = MLA_HEADS * MLA_HEAD_PAD, MLA_WIDTH
    n = DEC_BATCH * PAST_LEN
    return pl.pallas_call(
        functools.partial(_kvproj_kernel, normalize=False),
        grid=(DEC_BATCH,),
        in_specs=[
            pl.BlockSpec((tm, W_KV), lambda b: (b, 0)),
            pl.BlockSpec((None, 1, KV_LORA), lambda b: (layer, 0, 0)),
            pl.BlockSpec((KV_LORA, MLA_HEADS * MLA_NOPE_DIM), lambda b: (0, 0)),
            pl.BlockSpec((KV_LORA, MLA_WIDTH), lambda b: (0, 0)),
            pl.BlockSpec((tm, LANES), lambda b: (0, 0)),
            pl.BlockSpec((tm, LANES), lambda b: (0, 0)),
        ],
        out_specs=[pl.BlockSpec((tm, kw), lambda b: (b, 0)), pl.BlockSpec((tm, vw), lambda b: (b, 0))],
        out_shape=[jax.ShapeDtypeStruct((n, kw), BF16), jax.ShapeDtypeStruct((n, vw), BF16)],
        compiler_params=_params(("arbitrary",)),
        name="kv_proj_cache",
    )(cache_kv, kv_norm_g, w_uk_l, w_uv_l, cos_c, sin_c)


def _attn_kernel(q_ref, *refs, n_heads):
    o_ref = refs[-1]
    segs = [(refs[i], refs[i + 1]) for i in range(0, len(refs) - 1, 2)]
    nt = (((1,), (1,)), ((), ()))
    for h in range(n_heads):
        cols = slice(h * MLA_HEAD_PAD, (h + 1) * MLA_HEAD_PAD)
        vcols = slice(h * MLA_V_DIM, (h + 1) * MLA_V_DIM)
        q = q_ref[:, cols]
        scores = [lax.dot_general(q, k_ref[:, cols], nt, preferred_element_type=F32) for k_ref, _ in segs]
        m = scores[0].max(axis=-1, keepdims=True)
        for s in scores[1:]:
            m = jnp.maximum(m, s.max(axis=-1, keepdims=True))
        l, o = 0.0, 0.0
        for s, (_, v_ref) in zip(scores, segs):
            p = jnp.exp2((s - m) * (MLA_QK_DIM ** -0.5 * math.log2(math.e)))
            l = l + jnp.sum(p, axis=-1, keepdims=True)
            o = o + jnp.dot(p.astype(BF16), v_ref[:, vcols], preferred_element_type=F32)
        o_ref[:, vcols] = o / l


def mla_attention(q, k_tok, v_tok, k_cache, v_cache, latent):
    h = MLA_HEADS
    if latent:
        tq, n_heads = ATT_TQ, 1
        n_q = DEC_SEQ // tq
        grid = (DEC_BATCH, h, n_q)
        q_map = lambda b, hh, i: (N_CTX // tq + b * n_q + i, hh)
        kv_map = lambda b, hh, i: (N_CTX // DEC_SEQ + b, hh)
        o_map = lambda b, hh, i: (b * n_q + i, hh)
        in_specs = [
            pl.BlockSpec((tq, MLA_HEAD_PAD), q_map),
            pl.BlockSpec((DEC_SEQ, MLA_HEAD_PAD), kv_map),
            pl.BlockSpec((DEC_SEQ, MLA_V_DIM), kv_map),
            pl.BlockSpec((PAST_LEN, MLA_HEAD_PAD), lambda b, hh, i: (b, hh)),
            pl.BlockSpec((PAST_LEN, MLA_V_DIM), lambda b, hh, i: (b, hh)),
        ]
        args = [q, k_tok, v_tok, k_cache, v_cache]
        n_out = N_LAT
    else:
        tq, n_heads = SEQ, h
        grid = (BATCH,)
        o_map = lambda b: (b, 0)
        in_specs = [pl.BlockSpec((SEQ, h * MLA_HEAD_PAD), o_map)] * 2 + [pl.BlockSpec((SEQ, MLA_WIDTH), o_map)]
        args = [q, k_tok, v_tok]
        n_out = N_CTX
    return pl.pallas_call(
        functools.partial(_attn_kernel, n_heads=n_heads),
        grid=grid,
        in_specs=in_specs,
        out_specs=pl.BlockSpec((tq, n_heads * MLA_V_DIM), o_map),
        out_shape=jax.ShapeDtypeStruct((n_out, MLA_WIDTH), F32),
        compiler_params=_params(("arbitrary",) * len(grid)),
        name="mla_attn_lat" if latent else "mla_attn_ctx",
    )(*args)


OUT_TM = 256


def _layer_norm(y, g, b):
    mu = jnp.mean(y, axis=-1, keepdims=True)
    yc = y - mu
    var = jnp.mean(yc * yc, axis=-1, keepdims=True)
    return yc * lax.rsqrt(var + LN_EPS) * g + b


def _outproj_kernel(of_ref, ob_ref, hgg_ref, hgn_ref, org_ref, omla_ref, w_ref, x_ref, mod_ref, g_ref, b_ref, wr_ref,
                    x1_ref, hf_ref, pt_ref):
    o = of_ref[...] + ob_ref[...]
    heads = [o[:, h * HG_DK:(h + 1) * HG_DK] for h in range(HG_HEADS)]
    o = jnp.concatenate([oh * lax.rsqrt(jnp.mean(oh * oh, axis=-1, keepdims=True) + RMS_EPS) for oh in heads], axis=1)
    o_hg = o * hgn_ref[...] * _silu(hgg_ref[...])
    m = jnp.dot(o_hg.astype(BF16), w_ref[0:HG_WIDTH, :], preferred_element_type=F32)
    m += jnp.dot(org_ref[...].astype(BF16), w_ref[HG_WIDTH:HG_WIDTH + RG_WIDTH, :], preferred_element_type=F32)
    m += jnp.dot(omla_ref[...].astype(BF16), w_ref[HG_WIDTH + RG_WIDTH:, :], preferred_element_type=F32)
    md = mod_ref[...]
    x1 = _layer_norm(DN_ALPHA * x_ref[...] + md[2:3] * m, g_ref[...], b_ref[...])
    x1_ref[...] = x1
    hf = x1 * (1.0 + md[4:5]) + md[3:4]
    hf_ref[...] = hf
    logits = jnp.dot(hf.astype(BF16), wr_ref[...], preferred_element_type=F32)
    lane = lax.broadcasted_iota(jnp.int32, logits.shape, 1)
    logits = jnp.where(lane < N_EXPERTS, logits, -jnp.inf)
    e = jnp.exp(logits - jnp.max(logits, axis=-1, keepdims=True))
    p = e / jnp.sum(e, axis=-1, keepdims=True)
    pt_ref[...] = p.T[0:N_EXPERTS, :]


def out_proj(o_hg_f, o_hg_b, z_hg, hg_norm_g, o_rg, o_mla, w_out_l, x, mod6, ln_g, ln_b, w_router_l, layer):
    tm = OUT_TM
    row = lambda w: pl.BlockSpec((tm, w), lambda i: (i, 0))
    const = lambda shape: pl.BlockSpec(shape, lambda i: (0,) * len(shape), pipeline_mode=pl.Buffered(1))
    return pl.pallas_call(
        _outproj_kernel,
        grid=(N_TOK // tm,),
        in_specs=[
            row(HG_WIDTH), row(HG_WIDTH),
            pl.BlockSpec((tm, HG_WIDTH), lambda i: (i, 4)),
            pl.BlockSpec((None, 1, HG_WIDTH), lambda i: (layer, 0, 0)),
            row(RG_WIDTH), row(MLA_WIDTH),
            const((D_MODEL, D_MODEL)),
            row(D_MODEL),
            pl.BlockSpec((None, 6, D_MODEL), lambda i: (layer * N_MODROWS + _mod_row(i, tm), 0, 0)),
            pl.BlockSpec((None, 1, D_MODEL), lambda i: (layer, 0, 0)),
            pl.BlockSpec((None, 1, D_MODEL), lambda i: (layer, 0, 0)),
            const((D_MODEL, LANES)),
        ],
        out_specs=[row(D_MODEL), row(D_MODEL), pl.BlockSpec((N_EXPERTS, tm), lambda i: (0, i))],
        out_shape=[jax.ShapeDtypeStruct((N_TOK, D_MODEL), F32),
                   jax.ShapeDtypeStruct((N_TOK, D_MODEL), F32),
                   jax.ShapeDtypeStruct((N_EXPERTS, N_TOK), F32)],
        compiler_params=_params(("arbitrary",)),
        name="out_proj",
    )(o_hg_f, o_hg_b, z_hg, hg_norm_g, o_rg, o_mla, w_out_l, x, mod6, ln_g, ln_b, w_router_l)


CAP_CTX = CAP_FACTOR * N_CTX // N_EXPERTS
CAP_LAT = CAP_FACTOR * N_LAT // N_EXPERTS
ROUTE_SETS = ((0, N_CTX, CAP_CTX), (N_CTX, N_LAT, CAP_LAT))
SLOTS = CAP_CTX + CAP_LAT


def _route_thr_kernel(pt_ref, thr_ref, need_ref):
    for si, (start, n, cap) in enumerate(ROUTE_SETS):
        p = pt_ref[:, start:start + n]

        def count(mask):
            return jnp.sum(mask.astype(F32), axis=1, keepdims=True)

        def body(i, t):
            cand = t | jnp.left_shift(jnp.int32(1), 30 - i)
            return jnp.where(count(p >= pltpu.bitcast(cand, F32)) >= cap, cand, t)

        t = pltpu.bitcast(lax.fori_loop(0, 31, body, jnp.zeros((N_EXPERTS, 1), jnp.int32)), F32)
        need = cap - count(p > t)
        thr_ref[si] = jnp.broadcast_to(t, (N_EXPERTS, LANES))
        need_ref[si] = jnp.broadcast_to(need, (N_EXPERTS, LANES))


def route_threshold(p_t):
    n_sets = len(ROUTE_SETS)
    return pl.pallas_call(
        _route_thr_kernel,
        out_shape=[jax.ShapeDtypeStruct((n_sets, N_EXPERTS, LANES), F32),
                   jax.ShapeDtypeStruct((n_sets, N_EXPERTS, LANES), F32)],
        compiler_params=_params(None),
        name="route_threshold",
    )(p_t)


def _route_lists_kernel(p_ref, thr_ref, need_ref, lists_ref, cnt_ref, first_ref, sel_ref, *, nb, cap, tok0, row0):
    nt = (((1,), (1,)), ((), ()))
    r128 = lax.broadcasted_iota(jnp.int32, (LANES, LANES), 0)
    c128 = lax.broadcasted_iota(jnp.int32, (LANES, LANES), 1)
    incl = (r128 <= c128).astype(BF16)
    eye = r128 == c128
    rb = lax.broadcasted_iota(jnp.int32, (nb, nb), 0)
    cb = lax.broadcasted_iota(jnp.int32, (nb, nb), 1)
    below = (cb < rb).astype(BF16)
    incl_b = (rb <= cb).astype(BF16)
    ones8 = jnp.ones((SUBLANES, LANES), BF16)
    s_col = lax.broadcasted_iota(jnp.int32, (cap, 1), 0).astype(F32)
    lane = lax.broadcasted_iota(jnp.int32, (cap, LANES), 1)
    lane_f = lane.astype(F32)
    j_row = lax.broadcasted_iota(jnp.int32, (1, nb), 1).astype(F32)

    def dot(a, b):
        return jnp.dot(a, b, preferred_element_type=F32)

    def block_base(totals, unit):
        hi = jnp.floor(totals * (1.0 / unit))
        lo = totals - unit * hi
        bc = lambda a: jnp.broadcast_to(a, (nb, LANES)).astype(BF16)
        return unit * dot(below, bc(hi)) + dot(below, bc(lo))

    def choose(e, acc):
        p = p_ref[e]
        t = thr_ref[pl.ds(e, 1), :]
        need = need_ref[pl.ds(e, 1), :]
        eq = p == t
        eq_f = eq.astype(F32)
        eq_lp = dot(eq_f.astype(BF16), incl)
        tie_rank = eq_lp - eq_f + block_base(eq_lp[:, LANES - 1:LANES], 16.0)
        sel_f = jnp.where((p > t) | (eq & (tie_rank < need)), 1.0, 0.0)
        sel_ref[e] = sel_f
        return acc + sel_f

    cnt = lax.fori_loop(0, N_EXPERTS, choose, jnp.zeros((nb, LANES), F32))
    cnt_lp = dot(cnt.astype(BF16), incl)
    first = cnt_lp - cnt + block_base(cnt_lp[:, LANES - 1:LANES], 64.0) + row0
    first_hi = jnp.floor(first * (1.0 / LANES))
    first_lo = first - LANES * first_hi

    def expert(e, acc):
        p = p_ref[e]
        sel_f = sel_ref[e]
        sel_b = sel_f.astype(BF16)
        lp = dot(sel_b, incl)
        c_row = lax.dot_general(ones8, sel_b, nt, preferred_element_type=F32)
        incl_row = dot(c_row.astype(BF16), incl_b)[0:1]
        excl_row = incl_row - c_row[0:1]
        oh_j = jnp.where((s_col >= excl_row) & (s_col < incl_row), 1.0, 0.0)
        base_s = jnp.sum(oh_j * excl_row, axis=1, keepdims=True)
        j_s = jnp.sum(oh_j * j_row, axis=1, keepdims=True)
        oh_jb = oh_j.astype(BF16)
        lp_rows = dot(oh_jb, lp.astype(BF16))
        pos = jnp.sum(jnp.where(lp_rows <= s_col - base_s, 1.0, 0.0), axis=1, keepdims=True)
        oh_c = lane_f == pos
        p1 = p.astype(BF16)
        r1 = p - p1.astype(F32)
        p2 = r1.astype(BF16)
        p3 = (r1 - p2.astype(F32)).astype(BF16)
        p_rows = dot(oh_jb, p1) + dot(oh_jb, p2) + dot(oh_jb, p3)
        gate = jnp.sum(jnp.where(oh_c, p_rows, 0.0), axis=1, keepdims=True)
        rank_rows = dot(oh_jb, acc.astype(BF16))
        first_rows = LANES * dot(oh_jb, first_hi.astype(BF16)) + dot(oh_jb, first_lo.astype(BF16))
        dst = jnp.sum(jnp.where(oh_c, rank_rows + first_rows, 0.0), axis=1, keepdims=True)
        idx = tok0 + LANES * j_s + pos
        lists_ref[e] = jnp.where(lane == 0, idx, jnp.where(lane == 1, dst, jnp.where(lane == 2, gate, 0.0)))
        return acc + sel_f

    lax.fori_loop(0, N_EXPERTS, expert, jnp.zeros((nb, LANES), F32))

    ones_b = jnp.ones((LANES, LANES), BF16)

    def column(a, j):
        diag = jnp.where(eye, jnp.broadcast_to(a[j:j + 1, :], (LANES, LANES)), 0.0)
        return dot(diag.astype(BF16), ones_b)

    for j in range(nb):
        rows = slice(j * LANES, (j + 1) * LANES)
        cnt_ref[rows, :] = column(cnt, j)
        first_ref[rows, :] = LANES * column(first_hi, j) + column(first_lo, j)


def route_lists(p_blk, thr, need, set_index):
    tok0, n, cap = ROUTE_SETS[set_index]
    nb = n // LANES
    row0 = float(CAP_FACTOR * tok0)
    kern = functools.partial(_route_lists_kernel, nb=nb, cap=cap, tok0=tok0, row0=row0)
    return pl.pallas_call(
        kern,
        grid=(1,),
        in_specs=[
            pl.BlockSpec((N_EXPERTS, nb, LANES), lambda i: (0, 0, 0)),
            pl.BlockSpec((None, N_EXPERTS, LANES), lambda i: (set_index, 0, 0)),
            pl.BlockSpec((None, N_EXPERTS, LANES), lambda i: (set_index, 0, 0)),
        ],
        out_specs=[pl.BlockSpec((N_EXPERTS, cap, LANES), lambda i: (0, 0, 0)),
                   pl.BlockSpec((n, LANES), lambda i: (0, 0)),
                   pl.BlockSpec((n, LANES), lambda i: (0, 0))],
        out_shape=[jax.ShapeDtypeStruct((N_EXPERTS, cap, LANES), F32),
                   jax.ShapeDtypeStruct((n, LANES), F32),
                   jax.ShapeDtypeStruct((n, LANES), F32)],
        scratch_shapes=[pltpu.VMEM((N_EXPERTS, nb, LANES), F32)],
        compiler_params=_params(("arbitrary",)),
        name="route_lists_lat" if set_index else "route_lists_ctx",
    )(p_blk, thr, need)


FFN_TM = 512
FFN_FC = 256
FFN_ISSUE_UNROLL = 8


def _ffn_kernel(tok_ref, dst_ref, dstp_ref, hf_ref, lists_ref, wg_ref, wu_ref, wd_ref, yc_ref,
                wg_b, wu_b, wd_b, x_buf, y_buf, sem_in, sem_out):
    e, f = pl.program_id(0), pl.program_id(1)
    n_e, n_f = pl.num_programs(0), pl.num_programs(1)
    tm, fc = FFN_TM, FFN_FC
    n_tiles = SLOTS // tm

    def gather(r):
        slot = r % 2

        def issue(i, _):
            pltpu.make_async_copy(hf_ref.at[pl.ds(tok_ref[0, 0, r * tm + i], 1), :],
                                  x_buf.at[slot, pl.ds(i, 1), :], sem_in.at[slot]).start()
            return 0

        lax.fori_loop(0, tm, issue, 0, unroll=FFN_ISSUE_UNROLL)

    def scatter(r, idx_ref):
        slot = r % 2

        def issue(i, _):
            pltpu.make_async_copy(y_buf.at[slot, pl.ds(i, 1), :],
                                  yc_ref.at[pl.ds(idx_ref[0, 0, r * tm + i], 1), :], sem_out.at[slot]).start()
            return 0

        lax.fori_loop(0, tm, issue, 0, unroll=FFN_ISSUE_UNROLL)

    def wait_gather(slot):
        pltpu.make_async_copy(hf_ref.at[pl.ds(0, tm), :], x_buf.at[slot], sem_in.at[slot]).wait()

    def wait_scatter(slot):
        pltpu.make_async_copy(y_buf.at[slot], yc_ref.at[pl.ds(0, tm), :], sem_out.at[slot]).wait()

    assert n_tiles == 3
    for step, tile in ((0, 2), (1, 1)):
        @pl.when(f == step)
        def _():
            gather(step)
            pl.when(e > 0)(functools.partial(scatter, tile, dstp_ref))

    cols = pl.ds(pl.multiple_of(f * fc, fc), fc)
    wg_b[:, cols] = wg_ref[...].astype(BF16)
    wu_b[:, cols] = wu_ref[...].astype(BF16)
    wd_b[cols, :] = wd_ref[...].astype(BF16)

    @pl.when(f == n_f - 1)
    def _():
        for r in range(n_tiles):
            slot = r % 2
            if r == 1:
                gather(2)
            wait_gather(slot)
            x = x_buf[slot].astype(BF16)
            gate = jnp.dot(x, wg_b[...], preferred_element_type=F32)
            up = jnp.dot(x, wu_b[...], preferred_element_type=F32)
            y = jnp.dot((_silu(gate) * up).astype(BF16), wd_b[...], preferred_element_type=F32)
            if r == 2:
                wait_scatter(slot)
            else:
                pl.when(e > 0)(functools.partial(wait_scatter, slot))
            y_buf[slot] = y * lists_ref[r * tm:(r + 1) * tm, 2:3]
            if r == 0:
                scatter(0, dst_ref)

        @pl.when(e == n_e - 1)
        def _():
            scatter(1, dst_ref)
            scatter(2, dst_ref)
            wait_scatter(1)
            wait_scatter(0)


def expert_ffn(tok, dst, hf, lists, w_gate, w_up, w_down, layer):
    fc = FFN_FC
    dst3 = dst.reshape(N_EXPERTS, 1, SLOTS)
    idx_spec = pl.BlockSpec((1, 1, SLOTS), lambda e, f: (e, 0, 0), memory_space=pltpu.SMEM)
    return pl.pallas_call(
        _ffn_kernel,
        grid=(N_EXPERTS, D_FF_EXPERT // fc),
        in_specs=[
            idx_spec, idx_spec,
            pl.BlockSpec((1, 1, SLOTS), lambda e, f: (jnp.maximum(e - 1, 0), 0, 0), memory_space=pltpu.SMEM),
            pl.BlockSpec(memory_space=pl.ANY),
            pl.BlockSpec((None, SLOTS, LANES), lambda e, f: (e, 0, 0)),
            pl.BlockSpec((None, None, D_MODEL, fc), lambda e, f: (layer, e, 0, f)),
            pl.BlockSpec((None, None, D_MODEL, fc), lambda e, f: (layer, e, 0, f)),
            pl.BlockSpec((None, None, fc, D_MODEL), lambda e, f: (layer, e, f, 0)),
        ],
        out_specs=pl.BlockSpec(memory_space=pl.ANY),
        out_shape=jax.ShapeDtypeStruct((N_CHOICES, D_MODEL), F32),
        scratch_shapes=[pltpu.VMEM((D_MODEL, D_FF_EXPERT), BF16), pltpu.VMEM((D_MODEL, D_FF_EXPERT), BF16),
                        pltpu.VMEM((D_FF_EXPERT, D_MODEL), BF16),
                        pltpu.VMEM((2, FFN_TM, D_MODEL), F32), pltpu.VMEM((2, FFN_TM, D_MODEL), F32),
                        pltpu.SemaphoreType.DMA((2,)), pltpu.SemaphoreType.DMA((2,))],
        compiler_params=_params(("arbitrary", "arbitrary")),
        name="expert_ffn",
    )(tok.reshape(N_EXPERTS, 1, SLOTS), dst3, dst3, hf, lists, w_gate, w_up, w_down)


COMB_TM = 256
COMB_ROWS = 768
N_CHOICES = N_EXPERTS * SLOTS


def _combine_kernel(rows_ref, y_ref, cnt_ref, first_ref, x1_ref, mod_ref, g_ref, b_ref, *rest, split):
    if split:
        oc_ref, ol_ref, buf, sem = rest
    else:
        o_ref, buf, sem = rest
    i, n = pl.program_id(0), pl.num_programs(0)
    ln = COMB_ROWS

    def base(t):
        return lax.div(rows_ref[t], SUBLANES) * SUBLANES

    def start(t, k):
        return pl.multiple_of(jnp.minimum(base(t) + k * ln, N_CHOICES - ln), SUBLANES)

    def copy(t, k, slot):
        return pltpu.make_async_copy(y_ref.at[pl.ds(start(t, k), ln), :], buf.at[slot], sem.at[slot])

    slot = lax.rem(i, 2)

    @pl.when(i == 0)
    def _():
        copy(0, 0, 0).start()

    @pl.when(i + 1 < n)
    def _():
        copy(i + 1, 0, 1 - slot).start()

    copy(i, 0, slot).wait()
    first = first_ref[:, 0:1]
    last = first + cnt_ref[:, 0:1]
    col = lax.broadcasted_iota(jnp.int32, (1, ln), 1)

    def contribution(k, slot_k):
        row_id = (start(i, k) + col).astype(F32)
        mine = (row_id >= first) & (row_id < last) & (row_id >= (base(i) + k * ln).astype(F32))
        onehot = jnp.where(mine, 1.0, 0.0).astype(BF16)
        y = buf[slot_k]
        hi = y.astype(BF16)
        lo = (y - hi.astype(F32)).astype(BF16)
        return jnp.dot(onehot, hi, preferred_element_type=F32) + jnp.dot(onehot, lo, preferred_element_type=F32)

    def extra(k, acc):
        c = copy(i, k, 2)
        c.start()
        c.wait()
        return acc + contribution(k, 2)

    n_groups = lax.div(rows_ref[i + 1] - base(i) + ln - 1, ln)
    acc = lax.fori_loop(1, n_groups, extra, contribution(0, slot))
    g2 = mod_ref[...][5:6]
    out = _layer_norm(DN_ALPHA * x1_ref[...] + g2 * acc, g_ref[...], b_ref[...])
    if split:
        is_ctx = i < N_CTX // COMB_TM

        @pl.when(is_ctx)
        def _():
            oc_ref[...] = out

        @pl.when(jnp.logical_not(is_ctx))
        def _():
            ol_ref[...] = out
    else:
        o_ref[...] = out


def moe_combine(tile_rows, y_choices, cnt, first, x1, mod6, ln_g, ln_b, layer, split):
    tm = COMB_TM
    n_ctx_tiles = N_CTX // tm
    row = lambda w: pl.BlockSpec((tm, w), lambda i, t: (i, 0))
    if split:
        out_specs = [pl.BlockSpec((tm, D_MODEL), lambda i, t: (jnp.minimum(i, n_ctx_tiles - 1), 0)),
                     pl.BlockSpec((tm, D_MODEL), lambda i, t: (jnp.maximum(i - n_ctx_tiles, 0), 0))]
        out_shape = [jax.ShapeDtypeStruct((N_CTX, D_MODEL), F32), jax.ShapeDtypeStruct((N_LAT, D_MODEL), F32)]
    else:
        out_specs = row(D_MODEL)
        out_shape = jax.ShapeDtypeStruct((N_TOK, D_MODEL), F32)
    grid_spec = pltpu.PrefetchScalarGridSpec(
        num_scalar_prefetch=1,
        grid=(N_TOK // tm,),
        in_specs=[
            pl.BlockSpec(memory_space=pl.ANY),
            row(LANES), row(LANES), row(D_MODEL),
            pl.BlockSpec((None, 6, D_MODEL), lambda i, t: (layer * N_MODROWS + _mod_row(i, tm), 0, 0)),
            pl.BlockSpec((None, 1, D_MODEL), lambda i, t: (layer, 0, 0)),
            pl.BlockSpec((None, 1, D_MODEL), lambda i, t: (layer, 0, 0)),
        ],
        out_specs=out_specs,
        scratch_shapes=[pltpu.VMEM((3, COMB_ROWS, D_MODEL), F32), pltpu.SemaphoreType.DMA((3,))],
    )
    return pl.pallas_call(
        functools.partial(_combine_kernel, split=split),
        grid_spec=grid_spec,
        out_shape=out_shape,
        compiler_params=_params(("arbitrary",)),
        name="moe_combine",
    )(tile_rows, y_choices, cnt, first, x1, mod6, ln_g, ln_b)


def moe_block(x1, hf, p_t, mod6, ln_g, ln_b, w_gate, w_up, w_down, layer, split):
    thr, need = route_threshold(p_t)
    parts = []
    for si, (tok0, n, cap) in enumerate(ROUTE_SETS):
        p_blk = p_t[:, tok0:tok0 + n].reshape(N_EXPERTS, n // LANES, LANES)
        parts.append(route_lists(p_blk, thr, need, si))
    lists = jnp.concatenate([p[0] for p in parts], axis=1)
    cnt = jnp.concatenate([p[1] for p in parts], axis=0)
    first = jnp.concatenate([p[2] for p in parts], axis=0)
    tok = lists[:, :, 0].astype(jnp.int32)
    dst = lists[:, :, 1].astype(jnp.int32)
    y_choices = expert_ffn(tok, dst, hf, lists, w_gate, w_up, w_down, layer)
    tile_rows = jnp.concatenate([first[::COMB_TM, 0], jnp.full((1,), N_CHOICES, F32)]).astype(jnp.int32)
    return moe_combine(tile_rows, y_choices, cnt, first, x1, mod6, ln_g, ln_b, layer, split)


def kernel(x_prompt, x_sample, cache_mla_ckv, cache_mla_kpe, state_hgrn, state_rglru, c, c_ctx,
           w_in, w_out, hg_lb_logits, hg_norm_g, rg_conv_w, rg_conv_b, rg_w_r, rg_b_r, rg_w_i, rg_b_i,
           rg_lambda, mla_q_norm_g, mla_kv_norm_g, mla_w_uq, mla_w_uk, mla_w_uv, ada_w, ada_b,
           ln1_g, ln1_b, ln2_g, ln2_b, moe_router, moe_w_gate, moe_w_up, moe_w_down):
    x = jnp.concatenate([x_prompt.reshape(N_CTX, D_MODEL), x_sample.reshape(N_LAT, D_MODEL)], axis=0)
    cvec = jnp.concatenate([c_ctx[None, :], c, jnp.zeros((SUBLANES - N_MODROWS, D_MODEL), F32)], axis=0)
    mod = ada_mod(cvec, ada_w, ada_b)
    mod6 = mod[:, :N_MODROWS].reshape(DEPTH * N_MODROWS, 6, D_MODEL)
    cos_t, sin_t = rope_tables()
    vec = lambda a: a.reshape(DEPTH, 1, a.shape[-1])
    hg_ng, cb, qg, kg = vec(hg_norm_g), vec(rg_conv_b), vec(mla_q_norm_g), vec(mla_kv_norm_g)
    g1, b1, g2, b2 = vec(ln1_g), vec(ln1_b), vec(ln2_g), vec(ln2_b)
    w_r, w_i, w_in_b = rg_w_r.astype(BF16), rg_w_i.astype(BF16), w_in.astype(BF16)
    router = jnp.pad(moe_router, ((0, 0), (0, 0), (0, LANES - N_EXPERTS))).astype(BF16)

    ckvs, kpes, hgs, rgs = [], [], [], []
    for l in range(DEPTH):
        z_hg, z_rg, z_cq, z_kv = in_proj(x, mod6, w_in_b[l], prep_w_pe(w_in[l]), l)

        o_hg_f, o_hg_b, hg_fin_f, hg_fin_b = hgrn_mixer(z_hg, hg_lb_logits, state_hgrn, l)

        rg_args = (rg_conv_w, cb, w_r, rg_b_r, w_i, rg_b_i, rg_lambda)
        o_rg_c, rg_fin = rglru_mixer(z_rg, *rg_args, None, l, False)
        (o_rg_l,) = rglru_mixer(z_rg, *rg_args, state_rglru, l, True)
        o_rg = jnp.concatenate([o_rg_c, o_rg_l], axis=0)

        w_uk, w_uv = mla_w_uk[l].astype(BF16), mla_w_uv[l].astype(BF16)
        q = q_proj(z_cq, qg, prep_w_uq(mla_w_uq[l]), cos_t, sin_t, l)
        k_tok, v_tok, ckv_n, kpe = kv_proj(z_kv, kg, w_uk, w_uv, cos_t, sin_t, l)
        cache = jnp.concatenate([cache_mla_ckv[:, l].reshape(DEC_BATCH * PAST_LEN, KV_LORA),
                                 cache_mla_kpe[:, l].reshape(DEC_BATCH * PAST_LEN, MLA_ROPE_DIM),
                                 jnp.zeros((DEC_BATCH * PAST_LEN, MLA_ROPE_DIM), F32)], axis=1)
        k_cache, v_cache = kv_proj_cache(cache, kg, w_uk, w_uv, cos_t[:PAST_LEN], sin_t[:PAST_LEN], l)
        o_mla = jnp.concatenate([mla_attention(q, k_tok, v_tok, None, None, False),
                                 mla_attention(q, k_tok, v_tok, k_cache, v_cache, True)], axis=0)

        x1, hf, p_t = out_proj(o_hg_f, o_hg_b, z_hg, hg_ng, o_rg, o_mla, w_out[l].astype(BF16), x, mod6, g1, b1,
                               router[l], l)
        x = moe_block(x1, hf, p_t, mod6, g2, b2, moe_w_gate, moe_w_up, moe_w_down, l, split=(l == DEPTH - 1))

        ckvs.append(ckv_n[:N_CTX].reshape(BATCH, SEQ, KV_LORA))
        kpes.append(kpe[:N_CTX].reshape(BATCH, SEQ, MLA_ROPE_DIM))
        hgs.append(jnp.stack([hg_fin_f, hg_fin_b], axis=1))
        rgs.append(rg_fin)

    y_prompt = x[0].reshape(BATCH, SEQ, D_MODEL)
    y_sample = x[1].reshape(DEC_BATCH, DEC_SEQ, D_MODEL)
    return (y_prompt, y_sample, jnp.stack(ckvs, axis=1), jnp.stack(kpes, axis=1),
            jnp.stack(hgs, axis=1), jnp.stack(rgs, axis=1))
```

```python
import functools
import math

import jax
import jax.numpy as jnp
import numpy as np
from jax import lax
from jax.experimental import pallas as pl
from jax.experimental.pallas import tpu as pltpu

F32 = jnp.float32
BF16 = jnp.bfloat16

D_MODEL = 2048
BATCH = 16
SEQ = 256
DEPTH = 2
DEC_BATCH = 2
DEC_SEQ = 4096
PAST_LEN = 512
GRID_W = 64
HG_WIDTH = 512
HG_HEADS = 4
HG_DK = 128
RG_WIDTH = 512
RG_HEADS = 4
RG_BLOCK = 128
RG_CONV = 4
RG_C = 8.0
MLA_WIDTH = 1024
MLA_HEADS = 8
MLA_V_DIM = 128
MLA_NOPE_DIM = 128
MLA_ROPE_DIM = 64
MLA_QK_DIM = 192
Q_LORA = 512
KV_LORA = 512
ROPE_THETA = 10000.0
Q_BLOCK = 128
N_EXPERTS = 16
CAP_FACTOR = 2
D_FF_EXPERT = 1024
DN_ALPHA = (2.0 * DEPTH) ** 0.25
LN_EPS = 1e-5
RMS_EPS = 1e-6

N_CTX = BATCH * SEQ
N_LAT = DEC_BATCH * DEC_SEQ
N_TOK = N_CTX + N_LAT
N_MODROWS = 1 + DEC_BATCH

LANES = 128
SUBLANES = 8
VMEM_LIMIT_BYTES = 56 * 1024 * 1024

W_HG = 5 * HG_WIDTH
W_RG = 2 * RG_WIDTH
W_KV = KV_LORA + 2 * MLA_ROPE_DIM
IN_GROUPS = (W_HG, W_RG, Q_LORA, W_KV)
IN_COLS_OWN = sum(IN_GROUPS)


def _params(sem, vmem=VMEM_LIMIT_BYTES):
    return pltpu.CompilerParams(dimension_semantics=sem, vmem_limit_bytes=vmem)


def _mod_row(i, tm):
    n_ctx_tiles = N_CTX // tm
    per_batch = DEC_SEQ // tm
    return jnp.where(i < n_ctx_tiles, 0, 1 + (i - n_ctx_tiles) // per_batch)


ADA_TN = 1024


def _ada_kernel(c_ref, w_ref, b_ref, o_ref):
    c = c_ref[...]
    s = (c * jax.nn.sigmoid(c)).astype(BF16)
    o_ref[...] = jnp.dot(s, w_ref[...].astype(BF16), preferred_element_type=F32) + b_ref[...]


def ada_mod(cvec, ada_w, ada_b):
    ncol = 6 * D_MODEL
    return pl.pallas_call(
        _ada_kernel,
        grid=(DEPTH, ncol // ADA_TN),
        in_specs=[
            pl.BlockSpec((SUBLANES, D_MODEL), lambda l, j: (0, 0)),
            pl.BlockSpec((None, D_MODEL, ADA_TN), lambda l, j: (l, 0, j)),
            pl.BlockSpec((None, 1, ADA_TN), lambda l, j: (l, 0, j)),
        ],
        out_specs=pl.BlockSpec((None, SUBLANES, ADA_TN), lambda l, j: (l, 0, j)),
        out_shape=jax.ShapeDtypeStruct((DEPTH, SUBLANES, ncol), F32),
        compiler_params=_params(("arbitrary", "arbitrary")),
        name="ada_mod",
    )(cvec, ada_w, ada_b.reshape(DEPTH, 1, ncol))


INPROJ_TM = 256


def _inproj_kernel(x_ref, mod_ref, w_ref, wpe_ref, ohg_ref, org_ref, ocq_ref, okv_ref):
    m = mod_ref[...]
    hm = (x_ref[...] * (1.0 + m[1:2]) + m[0:1]).astype(BF16)
    a = 0
    for o_ref, width in zip((ohg_ref, org_ref, ocq_ref), IN_GROUPS[:3]):
        o_ref[...] = jnp.dot(hm, w_ref[:, a:a + width], preferred_element_type=F32)
        a += width
    okv_ref[:, 0:KV_LORA] = jnp.dot(hm, w_ref[:, a:a + KV_LORA], preferred_element_type=F32)
    okv_ref[:, KV_LORA:] = jnp.dot(hm, wpe_ref[...], preferred_element_type=F32)


def in_proj(x, mod6, w_in_l, w_pe_l, layer):
    tm = INPROJ_TM
    const = lambda a: pl.BlockSpec(a.shape, lambda i: (0, 0), pipeline_mode=pl.Buffered(1))
    return pl.pallas_call(
        _inproj_kernel,
        grid=(N_TOK // tm,),
        in_specs=[
            pl.BlockSpec((tm, D_MODEL), lambda i: (i, 0)),
            pl.BlockSpec((None, 6, D_MODEL), lambda i: (layer * N_MODROWS + _mod_row(i, tm), 0, 0)),
            const(w_in_l), const(w_pe_l),
        ],
        out_specs=[pl.BlockSpec((tm, w), lambda i: (i, 0)) for w in IN_GROUPS],
        out_shape=[jax.ShapeDtypeStruct((N_TOK, w), F32) for w in IN_GROUPS],
        compiler_params=_params(("arbitrary",)),
        name="in_proj",
    )(x, mod6, w_in_l, w_pe_l)


def _rot_cols(w):
    quarter = MLA_ROPE_DIM // 4
    j = np.arange(MLA_ROPE_DIM)
    first = (j % (2 * quarter)) < quarter
    src = np.where(first, j + quarter, j - quarter)
    sign = np.where(first, -1.0, 1.0).astype(np.float32)
    return w[..., src] * sign


def prep_w_pe(w_in_l):
    kpe = w_in_l[:, -MLA_ROPE_DIM:]
    return jnp.concatenate([kpe, _rot_cols(kpe)], axis=1).astype(BF16)


HG_CHUNK = 64
HG_SAFE_EXP = 80.0


def _silu(x):
    return x * jax.nn.sigmoid(x)


HG_TB = SEQ
HG_BLOCKS = N_TOK // HG_TB
HG_CTX_BLOCKS = N_CTX // HG_TB
HG_SEQ_BLOCKS = DEC_SEQ // HG_TB


def _hg_exact_att(q, k, cum):
    c = q.shape[0]
    lane = lax.broadcasted_iota(jnp.int32, (c, c), 1)
    row = lax.broadcasted_iota(jnp.int32, (c, 1), 0)

    def body(s_idx, att):
        sel = row == s_idx
        cum_s = jnp.sum(jnp.where(sel, cum, 0.0), axis=0, keepdims=True)
        k_s = jnp.sum(jnp.where(sel, k, 0.0), axis=0, keepdims=True)
        dec = jnp.exp(jnp.minimum(cum - cum_s, 0.0))
        col = jnp.sum(q * k_s * dec, axis=1, keepdims=True)
        return jnp.where(lane == s_idx, col, att)

    return lax.fori_loop(0, c, body, jnp.zeros((c, c), F32))


def _hgrn_kernel(qf_ref, ff_ref, vf_ref, qb_ref, fb_ref, vb_ref, lbl_ref, s0f_ref, s0b_ref,
                 of_ref, ob_ref, sf_ref, sb_ref, st_scr, *, layer):
    c = HG_CHUNK
    n_c = HG_TB // c
    heads = range(HG_HEADS)
    i = pl.program_id(0)
    blk = (i, HG_BLOCKS - 1 - i)
    is_ctx = tuple(b < HG_CTX_BLOCKS for b in blk)
    pos = tuple(lax.rem(b - HG_CTX_BLOCKS, HG_SEQ_BLOCKS) for b in blk)
    starts = (is_ctx[0] | (pos[0] == 0), is_ctx[1] | (pos[1] == HG_SEQ_BLOCKS - 1))

    for d, s0_ref in enumerate((s0f_ref, s0b_ref)):
        @pl.when(starts[d] & is_ctx[d])
        def _():
            st_scr[d] = jnp.zeros((HG_HEADS, HG_DK, HG_DK), F32)

        @pl.when(starts[d] & jnp.logical_not(is_ctx[d]))
        def _():
            for h in heads:
                st_scr[d, h] = s0_ref[h].T

    lg = lbl_ref[...]
    e = jnp.exp(lg - jnp.max(lg, axis=0, keepdims=True))
    sm = e / jnp.sum(e, axis=0, keepdims=True)
    lb = jnp.zeros_like(sm[0])
    for j in range(1, layer + 1):
        lb = lb + sm[j]

    r = lax.broadcasted_iota(jnp.int32, (c, c), 0)
    s = lax.broadcasted_iota(jnp.int32, (c, c), 1)
    masks = (r >= s, r <= s)
    marks = ((0, c // 2 - 1, c - 1), (c - 1, c // 2, 0))
    in_refs = ((qf_ref, ff_ref, vf_ref), (qb_ref, fb_ref, vb_ref))
    out_refs = (of_ref, ob_ref)

    work = ([], [])
    guard = jnp.float32(0.0)
    for d in range(2):
        q_ref, f_ref, v_ref = in_refs[d]
        lbd = lb[d:d + 1]
        first, mid, last = marks[d]
        for ci in (range(n_c) if d == 0 else reversed(range(n_c))):
            rows = slice(ci * c, (ci + 1) * c)
            f = lbd + (1.0 - lbd) * jax.nn.sigmoid(f_ref[rows, :])
            cum = jnp.dot(masks[d].astype(F32), jnp.log(f), precision=lax.Precision.HIGHEST,
                          preferred_element_type=F32)
            c_mid, c_last = cum[mid:mid + 1], cum[last:last + 1]
            guard = jnp.maximum(guard, jnp.max(jnp.maximum(cum[first:first + 1] - c_mid, c_mid - c_last)))
            work[d].append((rows, _silu(q_ref[rows, :]), 1.0 - f, v_ref[rows, :].astype(BF16), cum))

    nt = (((1,), (1,)), ((), ()))
    tn = (((0,), (0,)), ((), ()))

    def run(exact):
        for d in range(2):
            _, mid, last = marks[d]
            sts = [st_scr[d, h] for h in heads]
            for rows, q, k, v, cum in work[d]:
                c_mid, c_last = cum[mid:mid + 1], cum[last:last + 1]
                q_in = (q * jnp.exp(cum)).astype(BF16)
                k_end = (k * jnp.exp(c_last - cum)).astype(BF16)
                dec = jnp.exp(c_last)
                if not exact:
                    qt = (q * jnp.exp(cum - c_mid)).astype(BF16)
                    kt = (k * jnp.exp(c_mid - cum)).astype(BF16)
                outs = []
                for h in heads:
                    ls = slice(h * LANES, (h + 1) * LANES)
                    inter = lax.dot_general(q_in[:, ls], sts[h].astype(BF16), nt, preferred_element_type=F32)
                    if exact:
                        att = _hg_exact_att(q[:, ls], k[:, ls], cum[:, ls])
                    else:
                        att = lax.dot_general(qt[:, ls], kt[:, ls], nt, preferred_element_type=F32)
                    att = jnp.where(masks[d], att, 0.0).astype(BF16)
                    outs.append(inter + jnp.dot(att, v[:, ls], preferred_element_type=F32))
                    sts[h] = sts[h] * dec[:, ls] + lax.dot_general(v[:, ls], k_end[:, ls], tn,
                                                                   preferred_element_type=F32)
                out_refs[d][rows, :] = jnp.concatenate(outs, axis=1)
            for h in heads:
                st_scr[d, h] = sts[h]

    unsafe = guard > HG_SAFE_EXP
    pl.when(unsafe)(lambda: run(True))
    pl.when(jnp.logical_not(unsafe))(lambda: run(False))

    for d, fin_ref in enumerate((sf_ref, sb_ref)):
        @pl.when(is_ctx[d])
        def _():
            for h in heads:
                fin_ref[h] = st_scr[d, h].T


def hgrn_mixer(z_hg, lb_logits, state, layer):
    last = HG_BLOCKS - 1
    blocks = (lambda i: i, lambda i: last - i)
    lat_batch = lambda b: jnp.clip(lax.div(b - HG_CTX_BLOCKS, HG_SEQ_BLOCKS), 0, DEC_BATCH - 1)
    ctx_seq = lambda b: jnp.minimum(b, BATCH - 1)

    def col(d, group):
        return pl.BlockSpec((HG_TB, HG_WIDTH), lambda i: (blocks[d](i), group))

    def s0(d):
        return pl.BlockSpec((None, None, None, HG_HEADS, HG_DK, HG_DK),
                            lambda i: (lat_batch(blocks[d](i)), layer, d, 0, 0, 0))

    def fin(d):
        return pl.BlockSpec((None, HG_HEADS, HG_DK, HG_DK), lambda i: (ctx_seq(blocks[d](i)), 0, 0, 0))

    return pl.pallas_call(
        functools.partial(_hgrn_kernel, layer=layer),
        grid=(HG_BLOCKS,),
        in_specs=[col(0, 0), col(0, 1), col(0, 3), col(1, 0), col(1, 2), col(1, 3),
                  pl.BlockSpec((DEPTH, 2, HG_WIDTH), lambda i: (0, 0, 0)), s0(0), s0(1)],
        out_specs=[col(0, 0), col(1, 0), fin(0), fin(1)],
        out_shape=[jax.ShapeDtypeStruct((N_TOK, HG_WIDTH), F32), jax.ShapeDtypeStruct((N_TOK, HG_WIDTH), F32),
                   jax.ShapeDtypeStruct((BATCH, HG_HEADS, HG_DK, HG_DK), F32),
                   jax.ShapeDtypeStruct((BATCH, HG_HEADS, HG_DK, HG_DK), F32)],
        scratch_shapes=[pltpu.VMEM((2, HG_HEADS, HG_DK, HG_DK), F32)],
        compiler_params=_params(("arbitrary",)),
        name="hgrn",
    )(z_hg, z_hg, z_hg, z_hg, z_hg, z_hg, lb_logits, state, state)


RG_ROWS = 256
RG_PAD = SUBLANES
RG_SCAN_UNROLL = 8


def _tile_scan(a, b, reverse):
    row = lax.broadcasted_iota(jnp.int32, a.shape, 0)
    for sh in (1, 2, 4):
        if reverse:
            a_s, b_s = pltpu.roll(a, SUBLANES - sh, 0), pltpu.roll(b, SUBLANES - sh, 0)
            valid = row < SUBLANES - sh
        else:
            a_s, b_s = pltpu.roll(a, sh, 0), pltpu.roll(b, sh, 0)
            valid = row >= sh
        b = jnp.where(valid, a * b_s + b, b)
        a = jnp.where(valid, a * a_s, a)
    return a, b


def _rglru_kernel(*refs, seq_len, has_state):
    if has_state:
        (x_ref, y_ref, cw_ref, cb_ref, wr_ref, br_ref, wi_ref, bi_ref, lam_ref, h0_ref,
         o_ref, xp_scr, a_scr, b_scr, edge_scr, cin_scr) = refs
    else:
        (x_ref, y_ref, cw_ref, cb_ref, wr_ref, br_ref, wi_ref, bi_ref, lam_ref,
         o_ref, hfin_ref, xp_scr, a_scr, b_scr, edge_scr, cin_scr) = refs
    t = seq_len
    zeros = jnp.zeros((RG_PAD, LANES), F32)
    xp_scr[0:RG_PAD, :] = zeros
    xp_scr[RG_PAD + t:, :] = zeros
    xp_scr[RG_PAD:RG_PAD + t, :] = x_ref[...]

    cw = cw_ref[...]
    cb = cb_ref[...]
    lam = lam_ref[...]
    sp = jnp.maximum(-lam, 0.0) + jnp.log1p(jnp.exp(-jnp.abs(lam)))

    def gates(ci, _):
        r0 = pl.multiple_of(ci * RG_ROWS, RG_ROWS)
        xc = cb
        for j in range(RG_CONV):
            xc = xc + cw[j:j + 1] * xp_scr[pl.ds(r0 + RG_PAD - RG_CONV // 2 + j, RG_ROWS), :]
        rows = pl.ds(r0, RG_ROWS)
        xcb = xc.astype(BF16)
        for d in range(2):
            r = jax.nn.sigmoid(jnp.dot(xcb, wr_ref[d], preferred_element_type=F32) + br_ref[d])
            ig = jax.nn.sigmoid(jnp.dot(xcb, wi_ref[d], preferred_element_type=F32) + bi_ref[d])
            log_a = -RG_C * r * sp[d:d + 1]
            a_scr[d, rows, :] = jnp.exp(log_a)
            b_scr[d, rows, :] = jnp.sqrt(1.0 - jnp.exp(2.0 * log_a)) * (ig * xc)
        return 0

    lax.fori_loop(0, t // RG_ROWS, gates, 0)

    n_tiles = t // SUBLANES

    def tile_rows(j):
        return pl.ds(pl.multiple_of(j * SUBLANES, SUBLANES), SUBLANES)

    def local(j, _):
        rows = tile_rows(j)
        for d in range(2):
            aa, bb = _tile_scan(a_scr[d, rows, :], b_scr[d, rows, :], d == 1)
            a_scr[d, rows, :] = aa
            b_scr[d, rows, :] = bb
            row = 0 if d == 1 else SUBLANES - 1
            edge_scr[d, 0, j] = aa[row:row + 1]
            edge_scr[d, 1, j] = bb[row:row + 1]
        return 0

    lax.fori_loop(0, n_tiles, local, 0, unroll=RG_SCAN_UNROLL)

    def carry(j, c):
        jb = n_tiles - 1 - j
        cin_scr[0, j] = c[0]
        cin_scr[1, jb] = c[1]
        return (edge_scr[0, 0, j] * c[0] + edge_scr[0, 1, j], edge_scr[1, 0, jb] * c[1] + edge_scr[1, 1, jb])

    if has_state:
        h0 = h0_ref[...]
        init = (h0[0:1], h0[1:2])
    else:
        init = (jnp.zeros((1, LANES), F32), jnp.zeros((1, LANES), F32))
    h_f, h_b = lax.fori_loop(0, n_tiles, carry, init, unroll=RG_SCAN_UNROLL)

    def apply(j, _):
        rows = tile_rows(j)
        for d in range(2):
            b_scr[d, rows, :] = a_scr[d, rows, :] * cin_scr[d, j] + b_scr[d, rows, :]
        return 0

    lax.fori_loop(0, n_tiles, apply, 0, unroll=RG_SCAN_UNROLL)
    if not has_state:
        hfin_ref[0:1, :] = h_f
        hfin_ref[1:2, :] = h_b
    o_ref[...] = (b_scr[0] + b_scr[1]) * jax.nn.gelu(y_ref[...], approximate=True)


def rglru_mixer(z_rg, conv_w, conv_b, w_r, b_r, w_i, b_i, lam, state, layer, latent):
    seq_len = DEC_SEQ if latent else SEQ
    n_seq = DEC_BATCH if latent else BATCH
    blk0 = N_CTX // seq_len if latent else 0
    h = RG_HEADS
    vec = lambda rows: pl.BlockSpec((None, rows, LANES), lambda b, hh: (layer, 0, hh))
    wspec = pl.BlockSpec((None, 2, None, RG_BLOCK, RG_BLOCK), lambda b, hh: (layer, 0, hh, 0, 0))
    bspec = pl.BlockSpec((None, 2, 1, LANES), lambda b, hh: (layer, 0, 0, hh))
    in_specs = [
        pl.BlockSpec((seq_len, LANES), lambda b, hh: (blk0 + b, hh)),
        pl.BlockSpec((seq_len, LANES), lambda b, hh: (blk0 + b, h + hh)),
        vec(RG_CONV), vec(1), wspec, bspec, wspec, bspec, vec(2),
    ]
    args = [z_rg, z_rg, conv_w, conv_b, w_r, b_r.reshape(DEPTH, 2, 1, RG_WIDTH), w_i,
            b_i.reshape(DEPTH, 2, 1, RG_WIDTH), lam]
    o_spec = pl.BlockSpec((seq_len, LANES), lambda b, hh: (b, hh))
    o_shape = jax.ShapeDtypeStruct((n_seq * seq_len, RG_WIDTH), F32)
    if latent:
        in_specs.append(pl.BlockSpec((None, None, 2, LANES), lambda b, hh: (b, layer, 0, hh)))
        args.append(state)
        out_specs, out_shape = [o_spec], [o_shape]
    else:
        out_specs = [o_spec, pl.BlockSpec((None, 2, LANES), lambda b, hh: (b, 0, hh))]
        out_shape = [o_shape, jax.ShapeDtypeStruct((BATCH, 2, RG_WIDTH), F32)]
    return pl.pallas_call(
        functools.partial(_rglru_kernel, seq_len=seq_len, has_state=latent),
        grid=(n_seq, h),
        in_specs=in_specs,
        out_specs=out_specs,
        out_shape=out_shape,
        scratch_shapes=[pltpu.VMEM((seq_len + 2 * RG_PAD, LANES), F32),
                        pltpu.VMEM((2, seq_len, LANES), F32), pltpu.VMEM((2, seq_len, LANES), F32),
                        pltpu.VMEM((2, 2, seq_len // SUBLANES, 1, LANES), F32),
                        pltpu.VMEM((2, seq_len // SUBLANES, 1, LANES), F32)],
        compiler_params=_params(("arbitrary", "arbitrary")),
        name="rglru_lat" if latent else "rglru_ctx",
    )(*args)


MLA_HEAD_PAD = 2 * LANES
MLA_TM = 512
ATT_TQ = 256


def rope_tables():
    half = MLA_ROPE_DIM // 2
    t = np.arange(DEC_SEQ)
    row = (t // GRID_W).astype(np.float32)
    col = (t % GRID_W).astype(np.float32)
    inv = (ROPE_THETA ** (-np.arange(0, half, 2, dtype=np.float32) / half)).astype(np.float32)
    ar, ac = row[:, None] * inv, col[:, None] * inv
    cos = np.concatenate([np.cos(ar), np.cos(ar), np.cos(ac), np.cos(ac)], -1)
    sin = np.concatenate([np.sin(ar), np.sin(ar), np.sin(ac), np.sin(ac)], -1)
    pad = np.zeros((DEC_SEQ, LANES - MLA_ROPE_DIM), np.float32)
    cos_lat = np.tile(np.concatenate([cos, pad], -1), (DEC_BATCH, 1))
    sin_lat = np.tile(np.concatenate([sin, pad], -1), (DEC_BATCH, 1))
    cos_ctx = np.concatenate([np.ones((N_CTX, MLA_ROPE_DIM), np.float32), np.zeros((N_CTX, LANES - MLA_ROPE_DIM), np.float32)], -1)
    sin_ctx = np.zeros((N_CTX, LANES), np.float32)
    return (jnp.asarray(np.concatenate([cos_ctx, cos_lat], 0).astype(np.float32)),
            jnp.asarray(np.concatenate([sin_ctx, sin_lat], 0).astype(np.float32)))


def _rope_group(x, cos, sin):
    return x * cos + pltpu.roll(x, MLA_ROPE_DIM, 1) * sin


def _rms(x, g):
    return x * lax.rsqrt(jnp.mean(x * x, axis=-1, keepdims=True) + RMS_EPS) * g


def _qproj_kernel(cq_ref, g_ref, w_ref, cos_ref, sin_ref, q_ref):
    xn = _rms(cq_ref[...], g_ref[...]).astype(BF16)
    qm = jnp.dot(xn, w_ref[...], preferred_element_type=F32)
    cos, sin = cos_ref[...], sin_ref[...]
    for h in range(MLA_HEADS):
        a = h * MLA_HEAD_PAD
        q_ref[:, a:a + LANES] = qm[:, a:a + LANES].astype(BF16)
        q_ref[:, a + LANES:a + 2 * LANES] = _rope_group(qm[:, a + LANES:a + 2 * LANES], cos, sin).astype(BF16)


def q_proj(z_cq, q_norm_g, w_uq_l, cos_t, sin_t, layer):
    tm = MLA_TM
    width = MLA_HEADS * MLA_HEAD_PAD
    return pl.pallas_call(
        _qproj_kernel,
        grid=(N_TOK // tm,),
        in_specs=[
            pl.BlockSpec((tm, Q_LORA), lambda i: (i, 0)),
            pl.BlockSpec((None, 1, Q_LORA), lambda i: (layer, 0, 0)),
            pl.BlockSpec((Q_LORA, width), lambda i: (0, 0)),
            pl.BlockSpec((tm, LANES), lambda i: (i, 0)),
            pl.BlockSpec((tm, LANES), lambda i: (i, 0)),
        ],
        out_specs=pl.BlockSpec((tm, width), lambda i: (i, 0)),
        out_shape=jax.ShapeDtypeStruct((N_TOK, width), BF16),
        compiler_params=_params(("arbitrary",)),
        name="q_proj",
    )(z_cq, q_norm_g, w_uq_l, cos_t, sin_t)


def prep_w_uq(w_uq_l):
    w = w_uq_l.reshape(Q_LORA, MLA_HEADS, MLA_QK_DIM)
    pe = w[..., MLA_NOPE_DIM:]
    w = jnp.concatenate([w, _rot_cols(pe)], axis=-1)
    return w.reshape(Q_LORA, MLA_HEADS * MLA_HEAD_PAD).astype(BF16)


def _kvproj_kernel(zkv_ref, g_ref, wk_ref, wv_ref, cos_ref, sin_ref, *outs, normalize):
    if normalize:
        k_ref, v_ref, ckv_ref, kpe_ref = outs
    else:
        k_ref, v_ref = outs
    ckv = zkv_ref[:, 0:KV_LORA]
    if normalize:
        ckv = _rms(ckv, g_ref[...])
        ckv_ref[...] = ckv
    pe_group = zkv_ref[:, KV_LORA:KV_LORA + LANES]
    if normalize:
        kpe_ref[...] = pe_group[:, 0:MLA_ROPE_DIM]
    pe = _rope_group(pe_group, cos_ref[...], sin_ref[...]).astype(BF16)
    cb = ckv.astype(BF16)
    kn = jnp.dot(cb, wk_ref[...], preferred_element_type=F32)
    v_ref[...] = jnp.dot(cb, wv_ref[...], preferred_element_type=F32).astype(BF16)
    for h in range(MLA_HEADS):
        a = h * MLA_HEAD_PAD
        k_ref[:, a:a + LANES] = kn[:, h * LANES:(h + 1) * LANES].astype(BF16)
        k_ref[:, a + LANES:a + 2 * LANES] = pe


def kv_proj(z_kv, kv_norm_g, w_uk_l, w_uv_l, cos_t, sin_t, layer):
    tm = MLA_TM
    kw, vw = MLA_HEADS * MLA_HEAD_PAD, MLA_WIDTH
    return pl.pallas_call(
        functools.partial(_kvproj_kernel, normalize=True),
        grid=(N_TOK // tm,),
        in_specs=[
            pl.BlockSpec((tm, W_KV), lambda i: (i, 0)),
            pl.BlockSpec((None, 1, KV_LORA), lambda i: (layer, 0, 0)),
            pl.BlockSpec((KV_LORA, MLA_HEADS * MLA_NOPE_DIM), lambda i: (0, 0)),
            pl.BlockSpec((KV_LORA, MLA_WIDTH), lambda i: (0, 0)),
            pl.BlockSpec((tm, LANES), lambda i: (i, 0)),
            pl.BlockSpec((tm, LANES), lambda i: (i, 0)),
        ],
        out_specs=[
            pl.BlockSpec((tm, kw), lambda i: (i, 0)),
            pl.BlockSpec((tm, vw), lambda i: (i, 0)),
            pl.BlockSpec((tm, KV_LORA), lambda i: (i, 0)),
            pl.BlockSpec((tm, MLA_ROPE_DIM), lambda i: (i, 0)),
        ],
        out_shape=[
            jax.ShapeDtypeStruct((N_TOK, kw), BF16),
            jax.ShapeDtypeStruct((N_TOK, vw), BF16),
            jax.ShapeDtypeStruct((N_TOK, KV_LORA), F32),
            jax.ShapeDtypeStruct((N_TOK, MLA_ROPE_DIM), F32),
        ],
        compiler_params=_params(("arbitrary",)),
        name="kv_proj",
    )(z_kv, kv_norm_g, w_uk_l, w_uv_l, cos_t, sin_t)


def kv_proj_cache(cache_kv, kv_norm_g, w_uk_l, w_uv_l, cos_c, sin_c, layer):
    tm = PAST_LEN
    kw, vw = MLA_HEADS * MLA_HEAD_PAD, MLA_WIDTH
    n = DEC_BATCH * PAST_LEN
    return pl.pallas_call(
        functools.partial(_kvproj_kernel, normalize=False),
        grid=(DEC_BATCH,),
        in_specs=[
            pl.BlockSpec((tm, W_KV), lambda b: (b, 0)),
            pl.BlockSpec((None, 1, KV_LORA), lambda b: (layer, 0, 0)),
            pl.BlockSpec((KV_LORA, MLA_HEADS * MLA_NOPE_DIM), lambda b: (0, 0)),
            pl.BlockSpec((KV_LORA, MLA_WIDTH), lambda b: (0, 0)),
            pl.BlockSpec((tm, LANES), lambda b: (0, 0)),
            pl.BlockSpec((tm, LANES), lambda b: (0, 0)),
        ],
        out_specs=[pl.BlockSpec((tm, kw), lambda b: (b, 0)), pl.BlockSpec((tm, vw), lambda b: (b, 0))],
        out_shape=[jax.ShapeDtypeStruct((n, kw), BF16), jax.ShapeDtypeStruct((n, vw), BF16)],
        compiler_params=_params(("arbitrary",)),
        name="kv_proj_cache",
    )(cache_kv, kv_norm_g, w_uk_l, w_uv_l, cos_c, sin_c)


def _attn_kernel(q_ref, *refs, n_heads):
    o_ref = refs[-1]
    segs = [(refs[i], refs[i + 1]) for i in range(0, len(refs) - 1, 2)]
    nt = (((1,), (1,)), ((), ()))
    for h in range(n_heads):
        cols = slice(h * MLA_HEAD_PAD, (h + 1) * MLA_HEAD_PAD)
        vcols = slice(h * MLA_V_DIM, (h + 1) * MLA_V_DIM)
        q = q_ref[:, cols]
        scores = [lax.dot_general(q, k_ref[:, cols], nt, preferred_element_type=F32) for k_ref, _ in segs]
        m = scores[0].max(axis=-1, keepdims=True)
        for s in scores[1:]:
            m = jnp.maximum(m, s.max(axis=-1, keepdims=True))
        l, o = 0.0, 0.0
        for s, (_, v_ref) in zip(scores, segs):
            p = jnp.exp2((s - m) * (MLA_QK_DIM ** -0.5 * math.log2(math.e)))
            l = l + jnp.sum(p, axis=-1, keepdims=True)
            o = o + jnp.dot(p.astype(BF16), v_ref[:, vcols], preferred_element_type=F32)
        o_ref[:, vcols] = o / l


def mla_attention(q, k_tok, v_tok, k_cache, v_cache, latent):
    h = MLA_HEADS
    if latent:
        tq, n_heads = ATT_TQ, 1
        n_q = DEC_SEQ // tq
        grid = (DEC_BATCH, h, n_q)
        q_map = lambda b, hh, i: (N_CTX // tq + b * n_q + i, hh)
        kv_map = lambda b, hh, i: (N_CTX // DEC_SEQ + b, hh)
        o_map = lambda b, hh, i: (b * n_q + i, hh)
        in_specs = [
            pl.BlockSpec((tq, MLA_HEAD_PAD), q_map),
            pl.BlockSpec((DEC_SEQ, MLA_HEAD_PAD), kv_map),
            pl.BlockSpec((DEC_SEQ, MLA_V_DIM), kv_map),
            pl.BlockSpec((PAST_LEN, MLA_HEAD_PAD), lambda b, hh, i: (b, hh)),
            pl.BlockSpec((PAST_LEN, MLA_V_DIM), lambda b, hh, i: (b, hh)),
        ]
        args = [q, k_tok, v_tok, k_cache, v_cache]
        n_out = N_LAT
    else:
        tq, n_heads = SEQ, h
        grid = (BATCH,)
        o_map = lambda b: (b, 0)
        in_specs = [pl.BlockSpec((SEQ, h * MLA_HEAD_PAD), o_map)] * 2 + [pl.BlockSpec((SEQ, MLA_WIDTH), o_map)]
        args = [q, k_tok, v_tok]
        n_out = N_CTX
    return pl.pallas_call(
        functools.partial(_attn_kernel, n_heads=n_heads),
        grid=grid,
        in_specs=in_specs,
        out_specs=pl.BlockSpec((tq, n_heads * MLA_V_DIM), o_map),
        out_shape=jax.ShapeDtypeStruct((n_out, MLA_WIDTH), F32),
        compiler_params=_params(("arbitrary",) * len(grid)),
        name="mla_attn_lat" if latent else "mla_attn_ctx",
    )(*args)


OUT_TM = 256


def _layer_norm(y, g, b):
    mu = jnp.mean(y, axis=-1, keepdims=True)
    yc = y - mu
    var = jnp.mean(yc * yc, axis=-1, keepdims=True)
    return yc * lax.rsqrt(var + LN_EPS) * g + b


def _outproj_kernel(of_ref, ob_ref, hgg_ref, hgn_ref, orgc_ref, orgl_ref, omlac_ref, omlal_ref, w_ref, x_ref, mod_ref,
                    g_ref, b_ref, wr_ref, x1_ref, hf_ref, pt_ref):
    is_ctx = pl.program_id(0) < N_CTX // OUT_TM
    o_rg = jnp.where(is_ctx, orgc_ref[...], orgl_ref[...])
    o_mla = jnp.where(is_ctx, omlac_ref[...], omlal_ref[...])
    o = of_ref[...] + ob_ref[...]
    heads = [o[:, h * HG_DK:(h + 1) * HG_DK] for h in range(HG_HEADS)]
    o = jnp.concatenate([oh * lax.rsqrt(jnp.mean(oh * oh, axis=-1, keepdims=True) + RMS_EPS) for oh in heads], axis=1)
    o_hg = o * hgn_ref[...] * _silu(hgg_ref[...])
    m = jnp.dot(o_hg.astype(BF16), w_ref[0:HG_WIDTH, :], preferred_element_type=F32)
    m += jnp.dot(o_rg.astype(BF16), w_ref[HG_WIDTH:HG_WIDTH + RG_WIDTH, :], preferred_element_type=F32)
    m += jnp.dot(o_mla.astype(BF16), w_ref[HG_WIDTH + RG_WIDTH:, :], preferred_element_type=F32)
    md = mod_ref[...]
    x1 = _layer_norm(DN_ALPHA * x_ref[...] + md[2:3] * m, g_ref[...], b_ref[...])
    x1_ref[...] = x1
    hf = x1 * (1.0 + md[4:5]) + md[3:4]
    hf_ref[...] = hf
    logits = jnp.dot(hf.astype(BF16), wr_ref[...], preferred_element_type=F32)
    lane = lax.broadcasted_iota(jnp.int32, logits.shape, 1)
    logits = jnp.where(lane < N_EXPERTS, logits, -jnp.inf)
    e = jnp.exp(logits - jnp.max(logits, axis=-1, keepdims=True))
    p = e / jnp.sum(e, axis=-1, keepdims=True)
    pt_ref[...] = p.T[0:N_EXPERTS, :]


def out_proj(o_hg_f, o_hg_b, z_hg, hg_norm_g, o_rg, o_mla, w_out_l, x, mod6, ln_g, ln_b, w_router_l, layer):
    tm = OUT_TM
    n_ctx_tiles = N_CTX // tm
    row = lambda w: pl.BlockSpec((tm, w), lambda i: (i, 0))
    ctx_row = lambda w: pl.BlockSpec((tm, w), lambda i: (jnp.minimum(i, n_ctx_tiles - 1), 0))
    lat_row = lambda w: pl.BlockSpec((tm, w), lambda i: (jnp.maximum(i - n_ctx_tiles, 0), 0))
    const = lambda shape: pl.BlockSpec(shape, lambda i: (0,) * len(shape), pipeline_mode=pl.Buffered(1))
    return pl.pallas_call(
        _outproj_kernel,
        grid=(N_TOK // tm,),
        in_specs=[
            row(HG_WIDTH), row(HG_WIDTH),
            pl.BlockSpec((tm, HG_WIDTH), lambda i: (i, 4)),
            pl.BlockSpec((None, 1, HG_WIDTH), lambda i: (layer, 0, 0)),
            ctx_row(RG_WIDTH), lat_row(RG_WIDTH), ctx_row(MLA_WIDTH), lat_row(MLA_WIDTH),
            const((D_MODEL, D_MODEL)),
            row(D_MODEL),
            pl.BlockSpec((None, 6, D_MODEL), lambda i: (layer * N_MODROWS + _mod_row(i, tm), 0, 0)),
            pl.BlockSpec((None, 1, D_MODEL), lambda i: (layer, 0, 0)),
            pl.BlockSpec((None, 1, D_MODEL), lambda i: (layer, 0, 0)),
            const((D_MODEL, LANES)),
        ],
        out_specs=[row(D_MODEL), row(D_MODEL), pl.BlockSpec((N_EXPERTS, tm), lambda i: (0, i))],
        out_shape=[jax.ShapeDtypeStruct((N_TOK, D_MODEL), F32),
                   jax.ShapeDtypeStruct((N_TOK, D_MODEL), F32),
                   jax.ShapeDtypeStruct((N_EXPERTS, N_TOK), F32)],
        compiler_params=_params(("arbitrary",)),
        name="out_proj",
    )(o_hg_f, o_hg_b, z_hg, hg_norm_g, *o_rg, *o_mla, w_out_l, x, mod6, ln_g, ln_b, w_router_l)


CAP_CTX = CAP_FACTOR * N_CTX // N_EXPERTS
CAP_LAT = CAP_FACTOR * N_LAT // N_EXPERTS
ROUTE_SETS = ((0, N_CTX, CAP_CTX), (N_CTX, N_LAT, CAP_LAT))
SLOTS = CAP_CTX + CAP_LAT


def _route_thr_kernel(pt_ref, thr_ref, need_ref):
    for si, (start, n, cap) in enumerate(ROUTE_SETS):
        p = pt_ref[:, start:start + n]

        def count(mask):
            return jnp.sum(mask.astype(F32), axis=1, keepdims=True)

        def body(i, t):
            cand = t | jnp.left_shift(jnp.int32(1), 30 - i)
            return jnp.where(count(p >= pltpu.bitcast(cand, F32)) >= cap, cand, t)

        t = pltpu.bitcast(lax.fori_loop(0, 31, body, jnp.zeros((N_EXPERTS, 1), jnp.int32)), F32)
        need = cap - count(p > t)
        thr_ref[si] = jnp.broadcast_to(t, (N_EXPERTS, LANES))
        need_ref[si] = jnp.broadcast_to(need, (N_EXPERTS, LANES))


def route_threshold(p_t):
    n_sets = len(ROUTE_SETS)
    return pl.pallas_call(
        _route_thr_kernel,
        out_shape=[jax.ShapeDtypeStruct((n_sets, N_EXPERTS, LANES), F32),
                   jax.ShapeDtypeStruct((n_sets, N_EXPERTS, LANES), F32)],
        compiler_params=_params(None),
        name="route_threshold",
    )(p_t)


def _route_lists_kernel(p_ref, thr_ref, need_ref, lists_ref, cnt_ref, first_ref, sel_ref, *, nb, cap, tok0, row0):
    nt = (((1,), (1,)), ((), ()))
    r128 = lax.broadcasted_iota(jnp.int32, (LANES, LANES), 0)
    c128 = lax.broadcasted_iota(jnp.int32, (LANES, LANES), 1)
    incl = (r128 <= c128).astype(BF16)
    eye = r128 == c128
    rb = lax.broadcasted_iota(jnp.int32, (nb, nb), 0)
    cb = lax.broadcasted_iota(jnp.int32, (nb, nb), 1)
    below = (cb < rb).astype(BF16)
    incl_b = (rb <= cb).astype(BF16)
    ones8 = jnp.ones((SUBLANES, LANES), BF16)
    s_col = lax.broadcasted_iota(jnp.int32, (cap, 1), 0).astype(F32)
    lane = lax.broadcasted_iota(jnp.int32, (cap, LANES), 1)
    lane_f = lane.astype(F32)
    j_row = lax.broadcasted_iota(jnp.int32, (1, nb), 1).astype(F32)

    def dot(a, b):
        return jnp.dot(a, b, preferred_element_type=F32)

    def block_base(totals, unit):
        hi = jnp.floor(totals * (1.0 / unit))
        lo = totals - unit * hi
        bc = lambda a: jnp.broadcast_to(a, (nb, LANES)).astype(BF16)
        return unit * dot(below, bc(hi)) + dot(below, bc(lo))

    def choose(e, acc):
        p = p_ref[e]
        t = thr_ref[pl.ds(e, 1), :]
        need = need_ref[pl.ds(e, 1), :]
        eq = p == t
        eq_f = eq.astype(F32)
        eq_lp = dot(eq_f.astype(BF16), incl)
        tie_rank = eq_lp - eq_f + block_base(eq_lp[:, LANES - 1:LANES], 16.0)
        sel_f = jnp.where((p > t) | (eq & (tie_rank < need)), 1.0, 0.0)
        sel_ref[e] = sel_f
        return acc + sel_f

    cnt = lax.fori_loop(0, N_EXPERTS, choose, jnp.zeros((nb, LANES), F32))
    cnt_lp = dot(cnt.astype(BF16), incl)
    first = cnt_lp - cnt + block_base(cnt_lp[:, LANES - 1:LANES], 64.0) + row0
    first_hi = jnp.floor(first * (1.0 / LANES))
    first_lo = first - LANES * first_hi

    def expert(e, acc):
        p = p_ref[e]
        sel_f = sel_ref[e]
        sel_b = sel_f.astype(BF16)
        lp = dot(sel_b, incl)
        c_row = lax.dot_general(ones8, sel_b, nt, preferred_element_type=F32)
        incl_row = dot(c_row.astype(BF16), incl_b)[0:1]
        excl_row = incl_row - c_row[0:1]
        oh_j = jnp.where((s_col >= excl_row) & (s_col < incl_row), 1.0, 0.0)
        base_s = jnp.sum(oh_j * excl_row, axis=1, keepdims=True)
        j_s = jnp.sum(oh_j * j_row, axis=1, keepdims=True)
        oh_jb = oh_j.astype(BF16)
        lp_rows = dot(oh_jb, lp.astype(BF16))
        pos = jnp.sum(jnp.where(lp_rows <= s_col - base_s, 1.0, 0.0), axis=1, keepdims=True)
        oh_c = lane_f == pos
        p1 = p.astype(BF16)
        r1 = p - p1.astype(F32)
        p2 = r1.astype(BF16)
        p3 = (r1 - p2.astype(F32)).astype(BF16)
        p_rows = dot(oh_jb, p1) + dot(oh_jb, p2) + dot(oh_jb, p3)
        gate = jnp.sum(jnp.where(oh_c, p_rows, 0.0), axis=1, keepdims=True)
        rank_rows = dot(oh_jb, acc.astype(BF16))
        first_rows = LANES * dot(oh_jb, first_hi.astype(BF16)) + dot(oh_jb, first_lo.astype(BF16))
        dst = jnp.sum(jnp.where(oh_c, rank_rows + first_rows, 0.0), axis=1, keepdims=True)
        idx = tok0 + LANES * j_s + pos
        lists_ref[e] = jnp.where(lane == 0, idx, jnp.where(lane == 1, dst, jnp.where(lane == 2, gate, 0.0)))
        return acc + sel_f

    lax.fori_loop(0, N_EXPERTS, expert, jnp.zeros((nb, LANES), F32))

    ones_b = jnp.ones((LANES, LANES), BF16)

    def column(a, j):
        diag = jnp.where(eye, jnp.broadcast_to(a[j:j + 1, :], (LANES, LANES)), 0.0)
        return dot(diag.astype(BF16), ones_b)

    for j in range(nb):
        rows = slice(j * LANES, (j + 1) * LANES)
        cnt_ref[rows, :] = column(cnt, j)
        first_ref[rows, :] = LANES * column(first_hi, j) + column(first_lo, j)


def route_lists(p_blk, thr, need, set_index):
    tok0, n, cap = ROUTE_SETS[set_index]
    nb = n // LANES
    row0 = float(CAP_FACTOR * tok0)
    kern = functools.partial(_route_lists_kernel, nb=nb, cap=cap, tok0=tok0, row0=row0)
    return pl.pallas_call(
        kern,
        grid=(1,),
        in_specs=[
            pl.BlockSpec((N_EXPERTS, nb, LANES), lambda i: (0, 0, 0)),
            pl.BlockSpec((None, N_EXPERTS, LANES), lambda i: (set_index, 0, 0)),
            pl.BlockSpec((None, N_EXPERTS, LANES), lambda i: (set_index, 0, 0)),
        ],
        out_specs=[pl.BlockSpec((N_EXPERTS, cap, LANES), lambda i: (0, 0, 0)),
                   pl.BlockSpec((n, LANES), lambda i: (0, 0)),
                   pl.BlockSpec((n, LANES), lambda i: (0, 0))],
        out_shape=[jax.ShapeDtypeStruct((N_EXPERTS, cap, LANES), F32),
                   jax.ShapeDtypeStruct((n, LANES), F32),
                   jax.ShapeDtypeStruct((n, LANES), F32)],
        scratch_shapes=[pltpu.VMEM((N_EXPERTS, nb, LANES), F32)],
        compiler_params=_params(("arbitrary",)),
        name="route_lists_lat" if set_index else "route_lists_ctx",
    )(p_blk, thr, need)


FFN_TM = 512
FFN_FC = 256
FFN_ISSUE_UNROLL = 8


def _ffn_kernel(tok_ref, dst_ref, dstp_ref, hf_ref, lists_ref, wg_ref, wu_ref, wd_ref, yc_ref,
                wg_b, wu_b, wd_b, x_buf, y_buf, sem_in, sem_out):
    e, f = pl.program_id(0), pl.program_id(1)
    n_e, n_f = pl.num_programs(0), pl.num_programs(1)
    tm, fc = FFN_TM, FFN_FC
    n_tiles = SLOTS // tm

    def gather(r):
        slot = r % 2

        def issue(i, _):
            pltpu.make_async_copy(hf_ref.at[pl.ds(tok_ref[0, 0, r * tm + i], 1), :],
                                  x_buf.at[slot, pl.ds(i, 1), :], sem_in.at[slot]).start()
            return 0

        lax.fori_loop(0, tm, issue, 0, unroll=FFN_ISSUE_UNROLL)

    def scatter(r, idx_ref):
        slot = r % 2

        def issue(i, _):
            pltpu.make_async_copy(y_buf.at[slot, pl.ds(i, 1), :],
                                  yc_ref.at[pl.ds(idx_ref[0, 0, r * tm + i], 1), :], sem_out.at[slot]).start()
            return 0

        lax.fori_loop(0, tm, issue, 0, unroll=FFN_ISSUE_UNROLL)

    def wait_gather(slot):
        pltpu.make_async_copy(hf_ref.at[pl.ds(0, tm), :], x_buf.at[slot], sem_in.at[slot]).wait()

    def wait_scatter(slot):
        pltpu.make_async_copy(y_buf.at[slot], yc_ref.at[pl.ds(0, tm), :], sem_out.at[slot]).wait()

    assert n_tiles == 3
    for step, tile in ((0, 2), (1, 1)):
        @pl.when(f == step)
        def _():
            gather(step)
            pl.when(e > 0)(functools.partial(scatter, tile, dstp_ref))

    cols = pl.ds(pl.multiple_of(f * fc, fc), fc)
    wg_b[:, cols] = wg_ref[...].astype(BF16)
    wu_b[:, cols] = wu_ref[...].astype(BF16)
    wd_b[cols, :] = wd_ref[...].astype(BF16)

    @pl.when(f == n_f - 1)
    def _():
        for r in range(n_tiles):
            slot = r % 2
            if r == 1:
                gather(2)
            wait_gather(slot)
            x = x_buf[slot].astype(BF16)
            gate = jnp.dot(x, wg_b[...], preferred_element_type=F32)
            up = jnp.dot(x, wu_b[...], preferred_element_type=F32)
            y = jnp.dot((_silu(gate) * up).astype(BF16), wd_b[...], preferred_element_type=F32)
            if r == 2:
                wait_scatter(slot)
            else:
                pl.when(e > 0)(functools.partial(wait_scatter, slot))
            y_buf[slot] = y * lists_ref[r * tm:(r + 1) * tm, 2:3]
            if r == 0:
                scatter(0, dst_ref)

        @pl.when(e == n_e - 1)
        def _():
            scatter(1, dst_ref)
            scatter(2, dst_ref)
            wait_scatter(1)
            wait_scatter(0)


def expert_ffn(tok, dst, hf, lists, w_gate, w_up, w_down, layer):
    fc = FFN_FC
    dst3 = dst.reshape(N_EXPERTS, 1, SLOTS)
    idx_spec = pl.BlockSpec((1, 1, SLOTS), lambda e, f: (e, 0, 0), memory_space=pltpu.SMEM)
    return pl.pallas_call(
        _ffn_kernel,
        grid=(N_EXPERTS, D_FF_EXPERT // fc),
        in_specs=[
            idx_spec, idx_spec,
            pl.BlockSpec((1, 1, SLOTS), lambda e, f: (jnp.maximum(e - 1, 0), 0, 0), memory_space=pltpu.SMEM),
            pl.BlockSpec(memory_space=pl.ANY),
            pl.BlockSpec((None, SLOTS, LANES), lambda e, f: (e, 0, 0)),
            pl.BlockSpec((None, None, D_MODEL, fc), lambda e, f: (layer, e, 0, f)),
            pl.BlockSpec((None, None, D_MODEL, fc), lambda e, f: (layer, e, 0, f)),
            pl.BlockSpec((None, None, fc, D_MODEL), lambda e, f: (layer, e, f, 0)),
        ],
        out_specs=pl.BlockSpec(memory_space=pl.ANY),
        out_shape=jax.ShapeDtypeStruct((N_CHOICES, D_MODEL), F32),
        scratch_shapes=[pltpu.VMEM((D_MODEL, D_FF_EXPERT), BF16), pltpu.VMEM((D_MODEL, D_FF_EXPERT), BF16),
                        pltpu.VMEM((D_FF_EXPERT, D_MODEL), BF16),
                        pltpu.VMEM((2, FFN_TM, D_MODEL), F32), pltpu.VMEM((2, FFN_TM, D_MODEL), F32),
                        pltpu.SemaphoreType.DMA((2,)), pltpu.SemaphoreType.DMA((2,))],
        compiler_params=_params(("arbitrary", "arbitrary")),
        name="expert_ffn",
    )(tok.reshape(N_EXPERTS, 1, SLOTS), dst3, dst3, hf, lists, w_gate, w_up, w_down)


COMB_TM = 256
COMB_ROWS = 640
N_CHOICES = N_EXPERTS * SLOTS


def _combine_kernel(rows_ref, y_ref, cnt_ref, first_ref, x1_ref, mod_ref, g_ref, b_ref, *rest, split):
    if split:
        oc_ref, ol_ref, buf, sem = rest
    else:
        o_ref, buf, sem = rest
    i, n = pl.program_id(0), pl.num_programs(0)
    ln = COMB_ROWS

    def base(t):
        return lax.div(rows_ref[t], SUBLANES) * SUBLANES

    def start(t, k):
        return pl.multiple_of(jnp.minimum(base(t) + k * ln, N_CHOICES - ln), SUBLANES)

    def copy(t, k, slot):
        return pltpu.make_async_copy(y_ref.at[pl.ds(start(t, k), ln), :], buf.at[slot], sem.at[slot])

    slot = lax.rem(i, 2)

    @pl.when(i == 0)
    def _():
        copy(0, 0, 0).start()

    @pl.when(i + 1 < n)
    def _():
        copy(i + 1, 0, 1 - slot).start()

    copy(i, 0, slot).wait()
    first = first_ref[:, 0:1]
    last = first + cnt_ref[:, 0:1]
    col = lax.broadcasted_iota(jnp.int32, (1, ln), 1)

    def contribution(k, slot_k):
        row_id = (start(i, k) + col).astype(F32)
        mine = (row_id >= first) & (row_id < last) & (row_id >= (base(i) + k * ln).astype(F32))
        onehot = jnp.where(mine, 1.0, 0.0).astype(BF16)
        y = buf[slot_k]
        hi = y.astype(BF16)
        lo = (y - hi.astype(F32)).astype(BF16)
        return jnp.dot(onehot, hi, preferred_element_type=F32) + jnp.dot(onehot, lo, preferred_element_type=F32)

    def extra(k, acc):
        c = copy(i, k, 2)
        c.start()
        c.wait()
        return acc + contribution(k, 2)

    n_groups = lax.div(rows_ref[i + 1] - base(i) + ln - 1, ln)
    acc = lax.fori_loop(1, n_groups, extra, contribution(0, slot))
    g2 = mod_ref[...][5:6]
    out = _layer_norm(DN_ALPHA * x1_ref[...] + g2 * acc, g_ref[...], b_ref[...])
    if split:
        is_ctx = i < N_CTX // COMB_TM

        @pl.when(is_ctx)
        def _():
            oc_ref[...] = out

        @pl.when(jnp.logical_not(is_ctx))
        def _():
            ol_ref[...] = out
    else:
        o_ref[...] = out


def moe_combine(tile_rows, y_choices, cnt, first, x1, mod6, ln_g, ln_b, layer, split):
    tm = COMB_TM
    n_ctx_tiles = N_CTX // tm
    row = lambda w: pl.BlockSpec((tm, w), lambda i, t: (i, 0))
    if split:
        out_specs = [pl.BlockSpec((tm, D_MODEL), lambda i, t: (jnp.minimum(i, n_ctx_tiles - 1), 0)),
                     pl.BlockSpec((tm, D_MODEL), lambda i, t: (jnp.maximum(i - n_ctx_tiles, 0), 0))]
        out_shape = [jax.ShapeDtypeStruct((N_CTX, D_MODEL), F32), jax.ShapeDtypeStruct((N_LAT, D_MODEL), F32)]
    else:
        out_specs = row(D_MODEL)
        out_shape = jax.ShapeDtypeStruct((N_TOK, D_MODEL), F32)
    grid_spec = pltpu.PrefetchScalarGridSpec(
        num_scalar_prefetch=1,
        grid=(N_TOK // tm,),
        in_specs=[
            pl.BlockSpec(memory_space=pl.ANY),
            row(LANES), row(LANES), row(D_MODEL),
            pl.BlockSpec((None, 6, D_MODEL), lambda i, t: (layer * N_MODROWS + _mod_row(i, tm), 0, 0)),
            pl.BlockSpec((None, 1, D_MODEL), lambda i, t: (layer, 0, 0)),
            pl.BlockSpec((None, 1, D_MODEL), lambda i, t: (layer, 0, 0)),
        ],
        out_specs=out_specs,
        scratch_shapes=[pltpu.VMEM((3, COMB_ROWS, D_MODEL), F32), pltpu.SemaphoreType.DMA((3,))],
    )
    return pl.pallas_call(
        functools.partial(_combine_kernel, split=split),
        grid_spec=grid_spec,
        out_shape=out_shape,
        compiler_params=_params(("arbitrary",)),
        name="moe_combine",
    )(tile_rows, y_choices, cnt, first, x1, mod6, ln_g, ln_b)


def moe_block(x1, hf, p_t, mod6, ln_g, ln_b, w_gate, w_up, w_down, layer, split):
    thr, need = route_threshold(p_t)
    parts = []
    for si, (tok0, n, cap) in enumerate(ROUTE_SETS):
        p_blk = p_t[:, tok0:tok0 + n].reshape(N_EXPERTS, n // LANES, LANES)
        parts.append(route_lists(p_blk, thr, need, si))
    lists = jnp.concatenate([p[0] for p in parts], axis=1)
    cnt = jnp.concatenate([p[1] for p in parts], axis=0)
    first = jnp.concatenate([p[2] for p in parts], axis=0)
    tok = lists[:, :, 0].astype(jnp.int32)
    dst = lists[:, :, 1].astype(jnp.int32)
    y_choices = expert_ffn(tok, dst, hf, lists, w_gate, w_up, w_down, layer)
    tile_rows = jnp.concatenate([first[::COMB_TM, 0], jnp.full((1,), N_CHOICES, F32)]).astype(jnp.int32)
    return moe_combine(tile_rows, y_choices, cnt, first, x1, mod6, ln_g, ln_b, layer, split)


def kernel(x_prompt, x_sample, cache_mla_ckv, cache_mla_kpe, state_hgrn, state_rglru, c, c_ctx,
           w_in, w_out, hg_lb_logits, hg_norm_g, rg_conv_w, rg_conv_b, rg_w_r, rg_b_r, rg_w_i, rg_b_i,
           rg_lambda, mla_q_norm_g, mla_kv_norm_g, mla_w_uq, mla_w_uk, mla_w_uv, ada_w, ada_b,
           ln1_g, ln1_b, ln2_g, ln2_b, moe_router, moe_w_gate, moe_w_up, moe_w_down):
    x = jnp.concatenate([x_prompt.reshape(N_CTX, D_MODEL), x_sample.reshape(N_LAT, D_MODEL)], axis=0)
    cvec = jnp.concatenate([c_ctx[None, :], c, jnp.zeros((SUBLANES - N_MODROWS, D_MODEL), F32)], axis=0)
    mod = ada_mod(cvec, ada_w, ada_b)
    mod6 = mod[:, :N_MODROWS].reshape(DEPTH * N_MODROWS, 6, D_MODEL)
    cos_t, sin_t = rope_tables()
    vec = lambda a: a.reshape(DEPTH, 1, a.shape[-1])
    hg_ng, cb, qg, kg = vec(hg_norm_g), vec(rg_conv_b), vec(mla_q_norm_g), vec(mla_kv_norm_g)
    g1, b1, g2, b2 = vec(ln1_g), vec(ln1_b), vec(ln2_g), vec(ln2_b)
    w_r, w_i, w_in_b = rg_w_r.astype(BF16), rg_w_i.astype(BF16), w_in.astype(BF16)
    router = jnp.pad(moe_router, ((0, 0), (0, 0), (0, LANES - N_EXPERTS))).astype(BF16)

    ckvs, kpes, hgs, rgs = [], [], [], []
    for l in range(DEPTH):
        z_hg, z_rg, z_cq, z_kv = in_proj(x, mod6, w_in_b[l], prep_w_pe(w_in[l]), l)

        o_hg_f, o_hg_b, hg_fin_f, hg_fin_b = hgrn_mixer(z_hg, hg_lb_logits, state_hgrn, l)

        rg_args = (rg_conv_w, cb, w_r, rg_b_r, w_i, rg_b_i, rg_lambda)
        o_rg_c, rg_fin = rglru_mixer(z_rg, *rg_args, None, l, False)
        (o_rg_l,) = rglru_mixer(z_rg, *rg_args, state_rglru, l, True)
        o_rg = (o_rg_c, o_rg_l)

        w_uk, w_uv = mla_w_uk[l].astype(BF16), mla_w_uv[l].astype(BF16)
        q = q_proj(z_cq, qg, prep_w_uq(mla_w_uq[l]), cos_t, sin_t, l)
        k_tok, v_tok, ckv_n, kpe = kv_proj(z_kv, kg, w_uk, w_uv, cos_t, sin_t, l)
        cache = jnp.concatenate([cache_mla_ckv[:, l].reshape(DEC_BATCH * PAST_LEN, KV_LORA),
                                 cache_mla_kpe[:, l].reshape(DEC_BATCH * PAST_LEN, MLA_ROPE_DIM),
                                 jnp.zeros((DEC_BATCH * PAST_LEN, MLA_ROPE_DIM), F32)], axis=1)
        k_cache, v_cache = kv_proj_cache(cache, kg, w_uk, w_uv, cos_t[:PAST_LEN], sin_t[:PAST_LEN], l)
        o_mla = (mla_attention(q, k_tok, v_tok, None, None, False),
                 mla_attention(q, k_tok, v_tok, k_cache, v_cache, True))

        x1, hf, p_t = out_proj(o_hg_f, o_hg_b, z_hg, hg_ng, o_rg, o_mla, w_out[l].astype(BF16), x, mod6, g1, b1,
                               router[l], l)
        x = moe_block(x1, hf, p_t, mod6, g2, b2, moe_w_gate, moe_w_up, moe_w_down, l, split=(l == DEPTH - 1))

        ckvs.append(ckv_n[:N_CTX].reshape(BATCH, SEQ, KV_LORA))
        kpes.append(kpe[:N_CTX].reshape(BATCH, SEQ, MLA_ROPE_DIM))
        hgs.append(jnp.stack([hg_fin_f, hg_fin_b], axis=1))
        rgs.append(rg_fin)

    y_prompt = x[0].reshape(BATCH, SEQ, D_MODEL)
    y_sample = x[1].reshape(DEC_BATCH, DEC_SEQ, D_MODEL)
    return (y_prompt, y_sample, jnp.stack(ckvs, axis=1), jnp.stack(kpes, axis=1),
            jnp.stack(hgs, axis=1), jnp.stack(rgs, axis=1))
```

```python
import functools
import math

import jax
import jax.numpy as jnp
import numpy as np
from jax import lax
from jax.experimental import pallas as pl
from jax.experimental.pallas import tpu as pltpu

F32 = jnp.float32
BF16 = jnp.bfloat16

D_MODEL = 2048
BATCH = 16
SEQ = 256
DEPTH = 2
DEC_BATCH = 2
DEC_SEQ = 4096
PAST_LEN = 512
GRID_W = 64
HG_WIDTH = 512
HG_HEADS = 4
HG_DK = 128
RG_WIDTH = 512
RG_HEADS = 4
RG_BLOCK = 128
RG_CONV = 4
RG_C = 8.0
MLA_WIDTH = 1024
MLA_HEADS = 8
MLA_V_DIM = 128
MLA_NOPE_DIM = 128
MLA_ROPE_DIM = 64
MLA_QK_DIM = 192
Q_LORA = 512
KV_LORA = 512
ROPE_THETA = 10000.0
Q_BLOCK = 128
N_EXPERTS = 16
CAP_FACTOR = 2
D_FF_EXPERT = 1024
DN_ALPHA = (2.0 * DEPTH) ** 0.25
LN_EPS = 1e-5
RMS_EPS = 1e-6

N_CTX = BATCH * SEQ
N_LAT = DEC_BATCH * DEC_SEQ
N_TOK = N_CTX + N_LAT
N_MODROWS = 1 + DEC_BATCH

LANES = 128
SUBLANES = 8
VMEM_LIMIT_BYTES = 56 * 1024 * 1024

W_HG = 5 * HG_WIDTH
W_RG = 2 * RG_WIDTH
W_KV = KV_LORA + 2 * MLA_ROPE_DIM
IN_GROUPS = (W_HG, W_RG, Q_LORA, W_KV)
IN_COLS_OWN = sum(IN_GROUPS)


def _params(sem, vmem=VMEM_LIMIT_BYTES):
    return pltpu.CompilerParams(dimension_semantics=sem, vmem_limit_bytes=vmem)


def _mod_row(i, tm):
    n_ctx_tiles = N_CTX // tm
    per_batch = DEC_SEQ // tm
    return jnp.where(i < n_ctx_tiles, 0, 1 + (i - n_ctx_tiles) // per_batch)


ADA_TN = 1024


def _ada_kernel(c_ref, w_ref, b_ref, o_ref):
    c = c_ref[...]
    s = (c * jax.nn.sigmoid(c)).astype(BF16)
    o_ref[...] = jnp.dot(s, w_ref[...].astype(BF16), preferred_element_type=F32) + b_ref[...]


def ada_mod(cvec, ada_w, ada_b):
    ncol = 6 * D_MODEL
    return pl.pallas_call(
        _ada_kernel,
        grid=(DEPTH, ncol // ADA_TN),
        in_specs=[
            pl.BlockSpec((SUBLANES, D_MODEL), lambda l, j: (0, 0)),
            pl.BlockSpec((None, D_MODEL, ADA_TN), lambda l, j: (l, 0, j)),
            pl.BlockSpec((None, 1, ADA_TN), lambda l, j: (l, 0, j)),
        ],
        out_specs=pl.BlockSpec((None, SUBLANES, ADA_TN), lambda l, j: (l, 0, j)),
        out_shape=jax.ShapeDtypeStruct((DEPTH, SUBLANES, ncol), F32),
        compiler_params=_params(("arbitrary", "arbitrary")),
        name="ada_mod",
    )(cvec, ada_w, ada_b.reshape(DEPTH, 1, ncol))


INPROJ_TM = 512


def _inproj_kernel(x_ref, mod_ref, w_ref, wpe_ref, ohg_ref, org_ref, ocq_ref, okv_ref):
    m = mod_ref[...]
    hm = (x_ref[...] * (1.0 + m[1:2]) + m[0:1]).astype(BF16)
    a = 0
    for o_ref, width in zip((ohg_ref, org_ref, ocq_ref), IN_GROUPS[:3]):
        o_ref[...] = jnp.dot(hm, w_ref[:, a:a + width], preferred_element_type=F32)
        a += width
    okv_ref[:, 0:KV_LORA] = jnp.dot(hm, w_ref[:, a:a + KV_LORA], preferred_element_type=F32)
    okv_ref[:, KV_LORA:] = jnp.dot(hm, wpe_ref[...], preferred_element_type=F32)


def in_proj(x, mod6, w_in_l, w_pe_l, layer):
    tm = INPROJ_TM
    const = lambda a: pl.BlockSpec(a.shape, lambda i: (0, 0), pipeline_mode=pl.Buffered(1))
    return pl.pallas_call(
        _inproj_kernel,
        grid=(N_TOK // tm,),
        in_specs=[
            pl.BlockSpec((tm, D_MODEL), lambda i: (i, 0)),
            pl.BlockSpec((None, 6, D_MODEL), lambda i: (layer * N_MODROWS + _mod_row(i, tm), 0, 0)),
            const(w_in_l), const(w_pe_l),
        ],
        out_specs=[pl.BlockSpec((tm, w), lambda i: (i, 0)) for w in IN_GROUPS],
        out_shape=[jax.ShapeDtypeStruct((N_TOK, w), F32) for w in IN_GROUPS],
        compiler_params=_params(("arbitrary",)),
        name="in_proj",
    )(x, mod6, w_in_l, w_pe_l)


def _rot_cols(w):
    quarter = MLA_ROPE_DIM // 4
    j = np.arange(MLA_ROPE_DIM)
    first = (j % (2 * quarter)) < quarter
    src = np.where(first, j + quarter, j - quarter)
    sign = np.where(first, -1.0, 1.0).astype(np.float32)
    return w[..., src] * sign


def prep_w_pe(w_in_l):
    kpe = w_in_l[:, -MLA_ROPE_DIM:]
    return jnp.concatenate([kpe, _rot_cols(kpe)], axis=1).astype(BF16)


HG_CHUNK = 64
HG_SAFE_EXP = 80.0


def _silu(x):
    return x * jax.nn.sigmoid(x)


HG_TB = SEQ
HG_BLOCKS = N_TOK // HG_TB
HG_CTX_BLOCKS = N_CTX // HG_TB
HG_SEQ_BLOCKS = DEC_SEQ // HG_TB


def _hg_exact_att(q, k, cum):
    c = q.shape[0]
    lane = lax.broadcasted_iota(jnp.int32, (c, c), 1)
    row = lax.broadcasted_iota(jnp.int32, (c, 1), 0)

    def body(s_idx, att):
        sel = row == s_idx
        cum_s = jnp.sum(jnp.where(sel, cum, 0.0), axis=0, keepdims=True)
        k_s = jnp.sum(jnp.where(sel, k, 0.0), axis=0, keepdims=True)
        dec = jnp.exp(jnp.minimum(cum - cum_s, 0.0))
        col = jnp.sum(q * k_s * dec, axis=1, keepdims=True)
        return jnp.where(lane == s_idx, col, att)

    return lax.fori_loop(0, c, body, jnp.zeros((c, c), F32))


def _hgrn_kernel(qf_ref, ff_ref, vf_ref, qb_ref, fb_ref, vb_ref, lbl_ref, s0f_ref, s0b_ref,
                 of_ref, ob_ref, sf_ref, sb_ref, st_scr, *, layer):
    c = HG_CHUNK
    n_c = HG_TB // c
    heads = range(HG_HEADS)
    i = pl.program_id(0)
    blk = (i, HG_BLOCKS - 1 - i)
    is_ctx = tuple(b < HG_CTX_BLOCKS for b in blk)
    pos = tuple(lax.rem(b - HG_CTX_BLOCKS, HG_SEQ_BLOCKS) for b in blk)
    starts = (is_ctx[0] | (pos[0] == 0), is_ctx[1] | (pos[1] == HG_SEQ_BLOCKS - 1))

    for d, s0_ref in enumerate((s0f_ref, s0b_ref)):
        @pl.when(starts[d] & is_ctx[d])
        def _():
            st_scr[d] = jnp.zeros((HG_HEADS, HG_DK, HG_DK), F32)

        @pl.when(starts[d] & jnp.logical_not(is_ctx[d]))
        def _():
            for h in heads:
                st_scr[d, h] = s0_ref[h].T

    lg = lbl_ref[...]
    e = jnp.exp(lg - jnp.max(lg, axis=0, keepdims=True))
    sm = e / jnp.sum(e, axis=0, keepdims=True)
    lb = jnp.zeros_like(sm[0])
    for j in range(1, layer + 1):
        lb = lb + sm[j]

    r = lax.broadcasted_iota(jnp.int32, (c, c), 0)
    s = lax.broadcasted_iota(jnp.int32, (c, c), 1)
    masks = (r >= s, r <= s)
    marks = ((0, c // 2 - 1, c - 1), (c - 1, c // 2, 0))
    in_refs = ((qf_ref, ff_ref, vf_ref), (qb_ref, fb_ref, vb_ref))
    out_refs = (of_ref, ob_ref)

    work = ([], [])
    guard = jnp.float32(0.0)
    for d in range(2):
        q_ref, f_ref, v_ref = in_refs[d]
        lbd = lb[d:d + 1]
        first, mid, last = marks[d]
        for ci in (range(n_c) if d == 0 else reversed(range(n_c))):
            rows = slice(ci * c, (ci + 1) * c)
            f = lbd + (1.0 - lbd) * jax.nn.sigmoid(f_ref[rows, :])
            cum = jnp.dot(masks[d].astype(F32), jnp.log(f), precision=lax.Precision.HIGHEST,
                          preferred_element_type=F32)
            c_mid, c_last = cum[mid:mid + 1], cum[last:last + 1]
            guard = jnp.maximum(guard, jnp.max(jnp.maximum(cum[first:first + 1] - c_mid, c_mid - c_last)))
            work[d].append((rows, _silu(q_ref[rows, :]), 1.0 - f, v_ref[rows, :].astype(BF16), cum))

    nt = (((1,), (1,)), ((), ()))
    tn = (((0,), (0,)), ((), ()))

    def run(exact):
        for d in range(2):
            _, mid, last = marks[d]
            sts = [st_scr[d, h] for h in heads]
            for rows, q, k, v, cum in work[d]:
                c_mid, c_last = cum[mid:mid + 1], cum[last:last + 1]
                q_in = (q * jnp.exp(cum)).astype(BF16)
                k_end = (k * jnp.exp(c_last - cum)).astype(BF16)
                dec = jnp.exp(c_last)
                if not exact:
                    qt = (q * jnp.exp(cum - c_mid)).astype(BF16)
                    kt = (k * jnp.exp(c_mid - cum)).astype(BF16)
                outs = []
                for h in heads:
                    ls = slice(h * LANES, (h + 1) * LANES)
                    inter = lax.dot_general(q_in[:, ls], sts[h].astype(BF16), nt, preferred_element_type=F32)
                    if exact:
                        att = _hg_exact_att(q[:, ls], k[:, ls], cum[:, ls])
                    else:
                        att = lax.dot_general(qt[:, ls], kt[:, ls], nt, preferred_element_type=F32)
                    att = jnp.where(masks[d], att, 0.0).astype(BF16)
                    outs.append(inter + jnp.dot(att, v[:, ls], preferred_element_type=F32))
                    sts[h] = sts[h] * dec[:, ls] + lax.dot_general(v[:, ls], k_end[:, ls], tn,
                                                                   preferred_element_type=F32)
                out_refs[d][rows, :] = jnp.concatenate(outs, axis=1)
            for h in heads:
                st_scr[d, h] = sts[h]

    unsafe = guard > HG_SAFE_EXP
    pl.when(unsafe)(lambda: run(True))
    pl.when(jnp.logical_not(unsafe))(lambda: run(False))

    for d, fin_ref in enumerate((sf_ref, sb_ref)):
        @pl.when(is_ctx[d])
        def _():
            for h in heads:
                fin_ref[h] = st_scr[d, h].T


def hgrn_mixer(z_hg, lb_logits, state, layer):
    last = HG_BLOCKS - 1
    blocks = (lambda i: i, lambda i: last - i)
    lat_batch = lambda b: jnp.clip(lax.div(b - HG_CTX_BLOCKS, HG_SEQ_BLOCKS), 0, DEC_BATCH - 1)
    ctx_seq = lambda b: jnp.minimum(b, BATCH - 1)

    def col(d, group):
        return pl.BlockSpec((HG_TB, HG_WIDTH), lambda i: (blocks[d](i), group))

    def s0(d):
        return pl.BlockSpec((None, None, None, HG_HEADS, HG_DK, HG_DK),
                            lambda i: (lat_batch(blocks[d](i)), layer, d, 0, 0, 0))

    def fin(d):
        return pl.BlockSpec((None, HG_HEADS, HG_DK, HG_DK), lambda i: (ctx_seq(blocks[d](i)), 0, 0, 0))

    return pl.pallas_call(
        functools.partial(_hgrn_kernel, layer=layer),
        grid=(HG_BLOCKS,),
        in_specs=[col(0, 0), col(0, 1), col(0, 3), col(1, 0), col(1, 2), col(1, 3),
                  pl.BlockSpec((DEPTH, 2, HG_WIDTH), lambda i: (0, 0, 0)), s0(0), s0(1)],
        out_specs=[col(0, 0), col(1, 0), fin(0), fin(1)],
        out_shape=[jax.ShapeDtypeStruct((N_TOK, HG_WIDTH), F32), jax.ShapeDtypeStruct((N_TOK, HG_WIDTH), F32),
                   jax.ShapeDtypeStruct((BATCH, HG_HEADS, HG_DK, HG_DK), F32),
                   jax.ShapeDtypeStruct((BATCH, HG_HEADS, HG_DK, HG_DK), F32)],
        scratch_shapes=[pltpu.VMEM((2, HG_HEADS, HG_DK, HG_DK), F32)],
        compiler_params=_params(("arbitrary",)),
        name="hgrn",
    )(z_hg, z_hg, z_hg, z_hg, z_hg, z_hg, lb_logits, state, state)


RG_ROWS = 256
RG_PAD = SUBLANES
RG_SCAN_UNROLL = 8


def _tile_scan(a, b, reverse):
    row = lax.broadcasted_iota(jnp.int32, a.shape, 0)
    for sh in (1, 2, 4):
        if reverse:
            a_s, b_s = pltpu.roll(a, SUBLANES - sh, 0), pltpu.roll(b, SUBLANES - sh, 0)
            valid = row < SUBLANES - sh
        else:
            a_s, b_s = pltpu.roll(a, sh, 0), pltpu.roll(b, sh, 0)
            valid = row >= sh
        b = jnp.where(valid, a * b_s + b, b)
        a = jnp.where(valid, a * a_s, a)
    return a, b


def _rglru_kernel(*refs, seq_len, has_state):
    if has_state:
        (x_ref, y_ref, cw_ref, cb_ref, wr_ref, br_ref, wi_ref, bi_ref, lam_ref, h0_ref,
         o_ref, xp_scr, a_scr, b_scr, edge_scr, cin_scr) = refs
    else:
        (x_ref, y_ref, cw_ref, cb_ref, wr_ref, br_ref, wi_ref, bi_ref, lam_ref,
         o_ref, hfin_ref, xp_scr, a_scr, b_scr, edge_scr, cin_scr) = refs
    t = seq_len
    zeros = jnp.zeros((RG_PAD, LANES), F32)
    xp_scr[0:RG_PAD, :] = zeros
    xp_scr[RG_PAD + t:, :] = zeros
    xp_scr[RG_PAD:RG_PAD + t, :] = x_ref[...]

    cw = cw_ref[...]
    cb = cb_ref[...]
    lam = lam_ref[...]
    sp = jnp.maximum(-lam, 0.0) + jnp.log1p(jnp.exp(-jnp.abs(lam)))

    def gates(ci, _):
        r0 = pl.multiple_of(ci * RG_ROWS, RG_ROWS)
        xc = cb
        for j in range(RG_CONV):
            xc = xc + cw[j:j + 1] * xp_scr[pl.ds(r0 + RG_PAD - RG_CONV // 2 + j, RG_ROWS), :]
        rows = pl.ds(r0, RG_ROWS)
        xcb = xc.astype(BF16)
        for d in range(2):
            r = jax.nn.sigmoid(jnp.dot(xcb, wr_ref[d], preferred_element_type=F32) + br_ref[d])
            ig = jax.nn.sigmoid(jnp.dot(xcb, wi_ref[d], preferred_element_type=F32) + bi_ref[d])
            log_a = -RG_C * r * sp[d:d + 1]
            a_scr[d, rows, :] = jnp.exp(log_a)
            b_scr[d, rows, :] = jnp.sqrt(1.0 - jnp.exp(2.0 * log_a)) * (ig * xc)
        return 0

    lax.fori_loop(0, t // RG_ROWS, gates, 0)

    n_tiles = t // SUBLANES

    def tile_rows(j):
        return pl.ds(pl.multiple_of(j * SUBLANES, SUBLANES), SUBLANES)

    def local(j, _):
        rows = tile_rows(j)
        for d in range(2):
            aa, bb = _tile_scan(a_scr[d, rows, :], b_scr[d, rows, :], d == 1)
            a_scr[d, rows, :] = aa
            b_scr[d, rows, :] = bb
            row = 0 if d == 1 else SUBLANES - 1
            edge_scr[d, 0, j] = aa[row:row + 1]
            edge_scr[d, 1, j] = bb[row:row + 1]
        return 0

    lax.fori_loop(0, n_tiles, local, 0, unroll=RG_SCAN_UNROLL)

    def carry(j, c):
        jb = n_tiles - 1 - j
        cin_scr[0, j] = c[0]
        cin_scr[1, jb] = c[1]
        return (edge_scr[0, 0, j] * c[0] + edge_scr[0, 1, j], edge_scr[1, 0, jb] * c[1] + edge_scr[1, 1, jb])

    if has_state:
        h0 = h0_ref[...]
        init = (h0[0:1], h0[1:2])
    else:
        init = (jnp.zeros((1, LANES), F32), jnp.zeros((1, LANES), F32))
    h_f, h_b = lax.fori_loop(0, n_tiles, carry, init, unroll=RG_SCAN_UNROLL)

    def apply(j, _):
        rows = tile_rows(j)
        for d in range(2):
            b_scr[d, rows, :] = a_scr[d, rows, :] * cin_scr[d, j] + b_scr[d, rows, :]
        return 0

    lax.fori_loop(0, n_tiles, apply, 0, unroll=RG_SCAN_UNROLL)
    if not has_state:
        hfin_ref[0:1, :] = h_f
        hfin_ref[1:2, :] = h_b
    o_ref[...] = (b_scr[0] + b_scr[1]) * jax.nn.gelu(y_ref[...], approximate=True)


def rglru_mixer(z_rg, conv_w, conv_b, w_r, b_r, w_i, b_i, lam, state, layer, latent):
    seq_len = DEC_SEQ if latent else SEQ
    n_seq = DEC_BATCH if latent else BATCH
    blk0 = N_CTX // seq_len if latent else 0
    h = RG_HEADS
    vec = lambda rows: pl.BlockSpec((None, rows, LANES), lambda b, hh: (layer, 0, hh))
    wspec = pl.BlockSpec((None, 2, None, RG_BLOCK, RG_BLOCK), lambda b, hh: (layer, 0, hh, 0, 0))
    bspec = pl.BlockSpec((None, 2, 1, LANES), lambda b, hh: (layer, 0, 0, hh))
    in_specs = [
        pl.BlockSpec((seq_len, LANES), lambda b, hh: (blk0 + b, hh)),
        pl.BlockSpec((seq_len, LANES), lambda b, hh: (blk0 + b, h + hh)),
        vec(RG_CONV), vec(1), wspec, bspec, wspec, bspec, vec(2),
    ]
    args = [z_rg, z_rg, conv_w, conv_b, w_r, b_r.reshape(DEPTH, 2, 1, RG_WIDTH), w_i,
            b_i.reshape(DEPTH, 2, 1, RG_WIDTH), lam]
    o_spec = pl.BlockSpec((seq_len, LANES), lambda b, hh: (b, hh))
    o_shape = jax.ShapeDtypeStruct((n_seq * seq_len, RG_WIDTH), F32)
    if latent:
        in_specs.append(pl.BlockSpec((None, None, 2, LANES), lambda b, hh: (b, layer, 0, hh)))
        args.append(state)
        out_specs, out_shape = [o_spec], [o_shape]
    else:
        out_specs = [o_spec, pl.BlockSpec((None, 2, LANES), lambda b, hh: (b, 0, hh))]
        out_shape = [o_shape, jax.ShapeDtypeStruct((BATCH, 2, RG_WIDTH), F32)]
    return pl.pallas_call(
        functools.partial(_rglru_kernel, seq_len=seq_len, has_state=latent),
        grid=(n_seq, h),
        in_specs=in_specs,
        out_specs=out_specs,
        out_shape=out_shape,
        scratch_shapes=[pltpu.VMEM((seq_len + 2 * RG_PAD, LANES), F32),
                        pltpu.VMEM((2, seq_len, LANES), F32), pltpu.VMEM((2, seq_len, LANES), F32),
                        pltpu.VMEM((2, 2, seq_len // SUBLANES, 1, LANES), F32),
                        pltpu.VMEM((2, seq_len // SUBLANES, 1, LANES), F32)],
        compiler_params=_params(("arbitrary", "arbitrary")),
        name="rglru_lat" if latent else "rglru_ctx",
    )(*args)


MLA_HEAD_PAD = 2 * LANES
MLA_TM = 512
ATT_TQ = 256


def rope_tables():
    half = MLA_ROPE_DIM // 2
    t = np.arange(DEC_SEQ)
    row = (t // GRID_W).astype(np.float32)
    col = (t % GRID_W).astype(np.float32)
    inv = (ROPE_THETA ** (-np.arange(0, half, 2, dtype=np.float32) / half)).astype(np.float32)
    ar, ac = row[:, None] * inv, col[:, None] * inv
    cos = np.concatenate([np.cos(ar), np.cos(ar), np.cos(ac), np.cos(ac)], -1)
    sin = np.concatenate([np.sin(ar), np.sin(ar), np.sin(ac), np.sin(ac)], -1)
    pad = np.zeros((DEC_SEQ, LANES - MLA_ROPE_DIM), np.float32)
    cos_lat = np.tile(np.concatenate([cos, pad], -1), (DEC_BATCH, 1))
    sin_lat = np.tile(np.concatenate([sin, pad], -1), (DEC_BATCH, 1))
    cos_ctx = np.concatenate([np.ones((N_CTX, MLA_ROPE_DIM), np.float32), np.zeros((N_CTX, LANES - MLA_ROPE_DIM), np.float32)], -1)
    sin_ctx = np.zeros((N_CTX, LANES), np.float32)
    return (jnp.asarray(np.concatenate([cos_ctx, cos_lat], 0).astype(np.float32)),
            jnp.asarray(np.concatenate([sin_ctx, sin_lat], 0).astype(np.float32)))


def _rope_group(x, cos, sin):
    return x * cos + pltpu.roll(x, MLA_ROPE_DIM, 1) * sin


def _rms(x, g):
    return x * lax.rsqrt(jnp.mean(x * x, axis=-1, keepdims=True) + RMS_EPS) * g


def _qproj_kernel(cq_ref, g_ref, w_ref, cos_ref, sin_ref, q_ref):
    xn = _rms(cq_ref[...], g_ref[...]).astype(BF16)
    qm = jnp.dot(xn, w_ref[...], preferred_element_type=F32)
    cos, sin = cos_ref[...], sin_ref[...]
    for h in range(MLA_HEADS):
        a = h * MLA_HEAD_PAD
        q_ref[:, a:a + LANES] = qm[:, a:a + LANES].astype(BF16)
        q_ref[:, a + LANES:a + 2 * LANES] = _rope_group(qm[:, a + LANES:a + 2 * LANES], cos, sin).astype(BF16)


def q_proj(z_cq, q_norm_g, w_uq_l, cos_t, sin_t, layer):
    tm = MLA_TM
    width = MLA_HEADS * MLA_HEAD_PAD
    return pl.pallas_call(
        _qproj_kernel,
        grid=(N_TOK // tm,),
        in_specs=[
            pl.BlockSpec((tm, Q_LORA), lambda i: (i, 0)),
            pl.BlockSpec((None, 1, Q_LORA), lambda i: (layer, 0, 0)),
            pl.BlockSpec((Q_LORA, width), lambda i: (0, 0)),
            pl.BlockSpec((tm, LANES), lambda i: (i, 0)),
            pl.BlockSpec((tm, LANES), lambda i: (i, 0)),
        ],
        out_specs=pl.BlockSpec((tm, width), lambda i: (i, 0)),
        out_shape=jax.ShapeDtypeStruct((N_TOK, width), BF16),
        compiler_params=_params(("arbitrary",)),
        name="q_proj",
    )(z_cq, q_norm_g, w_uq_l, cos_t, sin_t)


def prep_w_uq(w_uq_l):
    w = w_uq_l.reshape(Q_LORA, MLA_HEADS, MLA_QK_DIM)
    pe = w[..., MLA_NOPE_DIM:]
    w = jnp.concatenate([w, _rot_cols(pe)], axis=-1)
    return w.reshape(Q_LORA, MLA_HEADS * MLA_HEAD_PAD).astype(BF16)


def _kvproj_kernel(zkv_ref, g_ref, wk_ref, wv_ref, cos_ref, sin_ref, *outs, normalize):
    if normalize:
        k_ref, v_ref, ckv_ref, kpe_ref = outs
    else:
        k_ref, v_ref = outs
    ckv = zkv_ref[:, 0:KV_LORA]
    if normalize:
        ckv = _rms(ckv, g_ref[...])
        ckv_ref[...] = ckv
    pe_group = zkv_ref[:, KV_LORA:KV_LORA + LANES]
    if normalize:
        kpe_ref[...] = pe_group[:, 0:MLA_ROPE_DIM]
    pe = _rope_group(pe_group, cos_ref[...], sin_ref[...]).astype(BF16)
    cb = ckv.astype(BF16)
    kn = jnp.dot(cb, wk_ref[...], preferred_element_type=F32)
    v_ref[...] = jnp.dot(cb, wv_ref[...], preferred_element_type=F32).astype(BF16)
    for h in range(MLA_HEADS):
        a = h * MLA_HEAD_PAD
        k_ref[:, a:a + LANES] = kn[:, h * LANES:(h + 1) * LANES].astype(BF16)
        k_ref[:, a + LANES:a + 2 * LANES] = pe


def kv_proj(z_kv, kv_norm_g, w_uk_l, w_uv_l, cos_t, sin_t, layer):
    tm = MLA_TM
    kw, vw = MLA_HEADS * MLA_HEAD_PAD, MLA_WIDTH
    return pl.pallas_call(
        functools.partial(_kvproj_kernel, normalize=True),
        grid=(N_TOK // tm,),
        in_specs=[
            pl.BlockSpec((tm, W_KV), lambda i: (i, 0)),
            pl.BlockSpec((None, 1, KV_LORA), lambda i: (layer, 0, 0)),
            pl.BlockSpec((KV_LORA, MLA_HEADS * MLA_NOPE_DIM), lambda i: (0, 0)),
            pl.BlockSpec((KV_LORA, MLA_WIDTH), lambda i: (0, 0)),
            pl.BlockSpec((tm, LANES), lambda i: (i, 0)),
            pl.BlockSpec((tm, LANES), lambda i: (i, 0)),
        ],
        out_specs=[
            pl.BlockSpec((tm, kw), lambda i: (i, 0)),
            pl.BlockSpec((tm, vw), lambda i: (i, 0)),
            pl.BlockSpec((tm, KV_LORA), lambda i: (i, 0)),
            pl.BlockSpec((tm, MLA_ROPE_DIM), lambda i: (i, 0)),
        ],
        out_shape=[
            jax.ShapeDtypeStruct((N_TOK, kw), BF16),
            jax.ShapeDtypeStruct((N_TOK, vw), BF16),
            jax.ShapeDtypeStruct((N_TOK, KV_LORA), F32),
            jax.ShapeDtypeStruct((N_TOK, MLA_ROPE_DIM), F32),
        ],
        compiler_params=_params(("arbitrary",)),
        name="kv_proj",
    )(z_kv, kv_norm_g, w_uk_l, w_uv_l, cos_t, sin_t)


def kv_proj_cache(cache_kv, kv_norm_g, w_uk_l, w_uv_l, cos_c, sin_c, layer):
    tm = PAST_LEN
    kw, vw = MLA_HEADS * MLA_HEAD_PAD, MLA_WIDTH
    n = DEC_BATCH * PAST_LEN
    return pl.pallas_call(
        functools.partial(_kvproj_kernel, normalize=False),
        grid=(DEC_BATCH,),
        in_specs=[
            pl.BlockSpec((tm, W_KV), lambda b: (b, 0)),
            pl.BlockSpec((None, 1, KV_LORA), lambda b: (layer, 0, 0)),
            pl.BlockSpec((KV_LORA, MLA_HEADS * MLA_NOPE_DIM), lambda b: (0, 0)),
            pl.BlockSpec((KV_LORA, MLA_WIDTH), lambda b: (0, 0)),
            pl.BlockSpec((tm, LANES), lambda b: (0, 0)),
            pl.BlockSpec((tm, LANES), lambda b: (0, 0)),
        ],
        out_specs=[pl.BlockSpec((tm, kw), lambda b: (b, 0)), pl.BlockSpec((tm, vw), lambda b: (b, 0))],
        out_shape=[jax.ShapeDtypeStruct((n, kw), BF16), jax.ShapeDtypeStruct((n, vw), BF16)],
        compiler_params=_params(("arbitrary",)),
        name="kv_proj_cache",
    )(cache_kv, kv_norm_g, w_uk_l, w_uv_l, cos_c, sin_c)


def _attn_kernel(q_ref, *refs, n_heads):
    o_ref = refs[-1]
    segs = [(refs[i], refs[i + 1]) for i in range(0, len(refs) - 1, 2)]
    nt = (((1,), (1,)), ((), ()))
    for h in range(n_heads):
        cols = slice(h * MLA_HEAD_PAD, (h + 1) * MLA_HEAD_PAD)
        vcols = slice(h * MLA_V_DIM, (h + 1) * MLA_V_DIM)
        q = q_ref[:, cols]
        scores = [lax.dot_general(q, k_ref[:, cols], nt, preferred_element_type=F32) for k_ref, _ in segs]
        m = scores[0].max(axis=-1, keepdims=True)
        for s in scores[1:]:
            m = jnp.maximum(m, s.max(axis=-1, keepdims=True))
        l, o = 0.0, 0.0
        for s, (_, v_ref) in zip(scores, segs):
            p = jnp.exp2((s - m) * (MLA_QK_DIM ** -0.5 * math.log2(math.e)))
            l = l + jnp.sum(p, axis=-1, keepdims=True)
            o = o + jnp.dot(p.astype(BF16), v_ref[:, vcols], preferred_element_type=F32)
        o_ref[:, vcols] = o / l


def mla_attention(q, k_tok, v_tok, k_cache, v_cache, latent):
    h = MLA_HEADS
    if latent:
        tq, n_heads = ATT_TQ, 1
        n_q = DEC_SEQ // tq
        grid = (DEC_BATCH, h, n_q)
        q_map = lambda b, hh, i: (N_CTX // tq + b * n_q + i, hh)
        kv_map = lambda b, hh, i: (N_CTX // DEC_SEQ + b, hh)
        o_map = lambda b, hh, i: (b * n_q + i, hh)
        in_specs = [
            pl.BlockSpec((tq, MLA_HEAD_PAD), q_map),
            pl.BlockSpec((DEC_SEQ, MLA_HEAD_PAD), kv_map),
            pl.BlockSpec((DEC_SEQ, MLA_V_DIM), kv_map),
            pl.BlockSpec((PAST_LEN, MLA_HEAD_PAD), lambda b, hh, i: (b, hh)),
            pl.BlockSpec((PAST_LEN, MLA_V_DIM), lambda b, hh, i: (b, hh)),
        ]
        args = [q, k_tok, v_tok, k_cache, v_cache]
        n_out = N_LAT
    else:
        tq, n_heads = SEQ, h
        grid = (BATCH,)
        o_map = lambda b: (b, 0)
        in_specs = [pl.BlockSpec((SEQ, h * MLA_HEAD_PAD), o_map)] * 2 + [pl.BlockSpec((SEQ, MLA_WIDTH), o_map)]
        args = [q, k_tok, v_tok]
        n_out = N_CTX
    return pl.pallas_call(
        functools.partial(_attn_kernel, n_heads=n_heads),
        grid=grid,
        in_specs=in_specs,
        out_specs=pl.BlockSpec((tq, n_heads * MLA_V_DIM), o_map),
        out_shape=jax.ShapeDtypeStruct((n_out, MLA_WIDTH), F32),
        compiler_params=_params(("arbitrary",) * len(grid)),
        name="mla_attn_lat" if latent else "mla_attn_ctx",
    )(*args)


OUT_TM = 256


def _layer_norm(y, g, b):
    mu = jnp.mean(y, axis=-1, keepdims=True)
    yc = y - mu
    var = jnp.mean(yc * yc, axis=-1, keepdims=True)
    return yc * lax.rsqrt(var + LN_EPS) * g + b


def _outproj_kernel(of_ref, ob_ref, hgg_ref, hgn_ref, orgc_ref, orgl_ref, omlac_ref, omlal_ref, w_ref, x_ref, mod_ref,
                    g_ref, b_ref, wr_ref, x1_ref, hf_ref, pt_ref):
    is_ctx = pl.program_id(0) < N_CTX // OUT_TM
    o_rg = jnp.where(is_ctx, orgc_ref[...], orgl_ref[...])
    o_mla = jnp.where(is_ctx, omlac_ref[...], omlal_ref[...])
    o = of_ref[...] + ob_ref[...]
    heads = [o[:, h * HG_DK:(h + 1) * HG_DK] for h in range(HG_HEADS)]
    o = jnp.concatenate([oh * lax.rsqrt(jnp.mean(oh * oh, axis=-1, keepdims=True) + RMS_EPS) for oh in heads], axis=1)
    o_hg = o * hgn_ref[...] * _silu(hgg_ref[...])
    m = jnp.dot(o_hg.astype(BF16), w_ref[0:HG_WIDTH, :], preferred_element_type=F32)
    m += jnp.dot(o_rg.astype(BF16), w_ref[HG_WIDTH:HG_WIDTH + RG_WIDTH, :], preferred_element_type=F32)
    m += jnp.dot(o_mla.astype(BF16), w_ref[HG_WIDTH + RG_WIDTH:, :], preferred_element_type=F32)
    md = mod_ref[...]
    x1 = _layer_norm(DN_ALPHA * x_ref[...] + md[2:3] * m, g_ref[...], b_ref[...])
    x1_ref[...] = x1
    hf = x1 * (1.0 + md[4:5]) + md[3:4]
    hf_ref[...] = hf
    logits = jnp.dot(hf.astype(BF16), wr_ref[...], preferred_element_type=F32)
    lane = lax.broadcasted_iota(jnp.int32, logits.shape, 1)
    logits = jnp.where(lane < N_EXPERTS, logits, -jnp.inf)
    e = jnp.exp(logits - jnp.max(logits, axis=-1, keepdims=True))
    p = e / jnp.sum(e, axis=-1, keepdims=True)
    pt_ref[...] = p.T[0:N_EXPERTS, :]


def out_proj(o_hg_f, o_hg_b, z_hg, hg_norm_g, o_rg, o_mla, w_out_l, x, mod6, ln_g, ln_b, w_router_l, layer):
    tm = OUT_TM
    n_ctx_tiles = N_CTX // tm
    row = lambda w: pl.BlockSpec((tm, w), lambda i: (i, 0))
    ctx_row = lambda w: pl.BlockSpec((tm, w), lambda i: (jnp.minimum(i, n_ctx_tiles - 1), 0))
    lat_row = lambda w: pl.BlockSpec((tm, w), lambda i: (jnp.maximum(i - n_ctx_tiles, 0), 0))
    const = lambda shape: pl.BlockSpec(shape, lambda i: (0,) * len(shape), pipeline_mode=pl.Buffered(1))
    return pl.pallas_call(
        _outproj_kernel,
        grid=(N_TOK // tm,),
        in_specs=[
            row(HG_WIDTH), row(HG_WIDTH),
            pl.BlockSpec((tm, HG_WIDTH), lambda i: (i, 4)),
            pl.BlockSpec((None, 1, HG_WIDTH), lambda i: (layer, 0, 0)),
            ctx_row(RG_WIDTH), lat_row(RG_WIDTH), ctx_row(MLA_WIDTH), lat_row(MLA_WIDTH),
            const((D_MODEL, D_MODEL)),
            row(D_MODEL),
            pl.BlockSpec((None, 6, D_MODEL), lambda i: (layer * N_MODROWS + _mod_row(i, tm), 0, 0)),
            pl.BlockSpec((None, 1, D_MODEL), lambda i: (layer, 0, 0)),
            pl.BlockSpec((None, 1, D_MODEL), lambda i: (layer, 0, 0)),
            const((D_MODEL, LANES)),
        ],
        out_specs=[row(D_MODEL), row(D_MODEL), pl.BlockSpec((N_EXPERTS, tm), lambda i: (0, i))],
        out_shape=[jax.ShapeDtypeStruct((N_TOK, D_MODEL), F32),
                   jax.ShapeDtypeStruct((N_TOK, D_MODEL), F32),
                   jax.ShapeDtypeStruct((N_EXPERTS, N_TOK), F32)],
        compiler_params=_params(("arbitrary",)),
        name="out_proj",
    )(o_hg_f, o_hg_b, z_hg, hg_norm_g, *o_rg, *o_mla, w_out_l, x, mod6, ln_g, ln_b, w_router_l)


CAP_CTX = CAP_FACTOR * N_CTX // N_EXPERTS
CAP_LAT = CAP_FACTOR * N_LAT // N_EXPERTS
ROUTE_SETS = ((0, N_CTX, CAP_CTX), (N_CTX, N_LAT, CAP_LAT))
SLOTS = CAP_CTX + CAP_LAT


def _route_thr_kernel(pt_ref, thr_ref, need_ref):
    for si, (start, n, cap) in enumerate(ROUTE_SETS):
        p = pt_ref[:, start:start + n]

        def count(mask):
            return jnp.sum(mask.astype(F32), axis=1, keepdims=True)

        def body(i, t):
            cand = t | jnp.left_shift(jnp.int32(1), 30 - i)
            return jnp.where(count(p >= pltpu.bitcast(cand, F32)) >= cap, cand, t)

        t = pltpu.bitcast(lax.fori_loop(0, 31, body, jnp.zeros((N_EXPERTS, 1), jnp.int32)), F32)
        need = cap - count(p > t)
        thr_ref[si] = jnp.broadcast_to(t, (N_EXPERTS, LANES))
        need_ref[si] = jnp.broadcast_to(need, (N_EXPERTS, LANES))


def route_threshold(p_t):
    n_sets = len(ROUTE_SETS)
    return pl.pallas_call(
        _route_thr_kernel,
        out_shape=[jax.ShapeDtypeStruct((n_sets, N_EXPERTS, LANES), F32),
                   jax.ShapeDtypeStruct((n_sets, N_EXPERTS, LANES), F32)],
        compiler_params=_params(None),
        name="route_threshold",
    )(p_t)


def _route_lists_kernel(p_ref, thr_ref, need_ref, lists_ref, cnt_ref, first_ref, sel_ref, *, nb, cap, tok0, row0):
    nt = (((1,), (1,)), ((), ()))
    r128 = lax.broadcasted_iota(jnp.int32, (LANES, LANES), 0)
    c128 = lax.broadcasted_iota(jnp.int32, (LANES, LANES), 1)
    incl = (r128 <= c128).astype(BF16)
    eye = r128 == c128
    rb = lax.broadcasted_iota(jnp.int32, (nb, nb), 0)
    cb = lax.broadcasted_iota(jnp.int32, (nb, nb), 1)
    below = (cb < rb).astype(BF16)
    incl_b = (rb <= cb).astype(BF16)
    ones8 = jnp.ones((SUBLANES, LANES), BF16)
    s_col = lax.broadcasted_iota(jnp.int32, (cap, 1), 0).astype(F32)
    lane = lax.broadcasted_iota(jnp.int32, (cap, LANES), 1)
    lane_f = lane.astype(F32)
    j_row = lax.broadcasted_iota(jnp.int32, (1, nb), 1).astype(F32)

    def dot(a, b):
        return jnp.dot(a, b, preferred_element_type=F32)

    def block_base(totals, unit):
        hi = jnp.floor(totals * (1.0 / unit))
        lo = totals - unit * hi
        bc = lambda a: jnp.broadcast_to(a, (nb, LANES)).astype(BF16)
        return unit * dot(below, bc(hi)) + dot(below, bc(lo))

    def choose(e, acc):
        p = p_ref[e]
        t = thr_ref[pl.ds(e, 1), :]
        need = need_ref[pl.ds(e, 1), :]
        eq = p == t
        eq_f = eq.astype(F32)
        eq_lp = dot(eq_f.astype(BF16), incl)
        tie_rank = eq_lp - eq_f + block_base(eq_lp[:, LANES - 1:LANES], 16.0)
        sel_f = jnp.where((p > t) | (eq & (tie_rank < need)), 1.0, 0.0)
        sel_ref[e] = sel_f
        return acc + sel_f

    cnt = lax.fori_loop(0, N_EXPERTS, choose, jnp.zeros((nb, LANES), F32))
    cnt_lp = dot(cnt.astype(BF16), incl)
    first = cnt_lp - cnt + block_base(cnt_lp[:, LANES - 1:LANES], 64.0) + row0
    first_hi = jnp.floor(first * (1.0 / LANES))
    first_lo = first - LANES * first_hi

    def expert(e, acc):
        p = p_ref[e]
        sel_f = sel_ref[e]
        sel_b = sel_f.astype(BF16)
        lp = dot(sel_b, incl)
        c_row = lax.dot_general(ones8, sel_b, nt, preferred_element_type=F32)
        incl_row = dot(c_row.astype(BF16), incl_b)[0:1]
        excl_row = incl_row - c_row[0:1]
        oh_j = jnp.where((s_col >= excl_row) & (s_col < incl_row), 1.0, 0.0)
        base_s = jnp.sum(oh_j * excl_row, axis=1, keepdims=True)
        j_s = jnp.sum(oh_j * j_row, axis=1, keepdims=True)
        oh_jb = oh_j.astype(BF16)
        lp_rows = dot(oh_jb, lp.astype(BF16))
        pos = jnp.sum(jnp.where(lp_rows <= s_col - base_s, 1.0, 0.0), axis=1, keepdims=True)
        oh_c = lane_f == pos
        p1 = p.astype(BF16)
        r1 = p - p1.astype(F32)
        p2 = r1.astype(BF16)
        p3 = (r1 - p2.astype(F32)).astype(BF16)
        p_rows = dot(oh_jb, p1) + dot(oh_jb, p2) + dot(oh_jb, p3)
        gate = jnp.sum(jnp.where(oh_c, p_rows, 0.0), axis=1, keepdims=True)
        rank_rows = dot(oh_jb, acc.astype(BF16))
        first_rows = LANES * dot(oh_jb, first_hi.astype(BF16)) + dot(oh_jb, first_lo.astype(BF16))
        dst = jnp.sum(jnp.where(oh_c, rank_rows + first_rows, 0.0), axis=1, keepdims=True)
        idx = tok0 + LANES * j_s + pos
        lists_ref[e] = jnp.where(lane == 0, idx, jnp.where(lane == 1, dst, jnp.where(lane == 2, gate, 0.0)))
        return acc + sel_f

    lax.fori_loop(0, N_EXPERTS, expert, jnp.zeros((nb, LANES), F32))

    ones_b = jnp.ones((LANES, LANES), BF16)

    def column(a, j):
        diag = jnp.where(eye, jnp.broadcast_to(a[j:j + 1, :], (LANES, LANES)), 0.0)
        return dot(diag.astype(BF16), ones_b)

    for j in range(nb):
        rows = slice(j * LANES, (j + 1) * LANES)
        cnt_ref[rows, :] = column(cnt, j)
        first_ref[rows, :] = LANES * column(first_hi, j) + column(first_lo, j)


def route_lists(p_blk, thr, need, set_index):
    tok0, n, cap = ROUTE_SETS[set_index]
    nb = n // LANES
    row0 = float(CAP_FACTOR * tok0)
    kern = functools.partial(_route_lists_kernel, nb=nb, cap=cap, tok0=tok0, row0=row0)
    return pl.pallas_call(
        kern,
        grid=(1,),
        in_specs=[
            pl.BlockSpec((N_EXPERTS, nb, LANES), lambda i: (0, 0, 0)),
            pl.BlockSpec((None, N_EXPERTS, LANES), lambda i: (set_index, 0, 0)),
            pl.BlockSpec((None, N_EXPERTS, LANES), lambda i: (set_index, 0, 0)),
        ],
        out_specs=[pl.BlockSpec((N_EXPERTS, cap, LANES), lambda i: (0, 0, 0)),
                   pl.BlockSpec((n, LANES), lambda i: (0, 0)),
                   pl.BlockSpec((n, LANES), lambda i: (0, 0))],
        out_shape=[jax.ShapeDtypeStruct((N_EXPERTS, cap, LANES), F32),
                   jax.ShapeDtypeStruct((n, LANES), F32),
                   jax.ShapeDtypeStruct((n, LANES), F32)],
        scratch_shapes=[pltpu.VMEM((N_EXPERTS, nb, LANES), F32)],
        compiler_params=_params(("arbitrary",)),
        name="route_lists_lat" if set_index else "route_lists_ctx",
    )(p_blk, thr, need)


FFN_TM = 512
FFN_FC = 256
FFN_ISSUE_UNROLL = 8


def _ffn_kernel(tok_ref, dst_ref, dstp_ref, hf_ref, lists_ref, wg_ref, wu_ref, wd_ref, yc_ref,
                wg_b, wu_b, wd_b, x_buf, y_buf, sem_in, sem_out):
    e, f = pl.program_id(0), pl.program_id(1)
    n_e, n_f = pl.num_programs(0), pl.num_programs(1)
    tm, fc = FFN_TM, FFN_FC
    n_tiles = SLOTS // tm

    def gather(r):
        slot = r % 2

        def issue(i, _):
            pltpu.make_async_copy(hf_ref.at[pl.ds(tok_ref[0, 0, r * tm + i], 1), :],
                                  x_buf.at[slot, pl.ds(i, 1), :], sem_in.at[slot]).start()
            return 0

        lax.fori_loop(0, tm, issue, 0, unroll=FFN_ISSUE_UNROLL)

    def scatter(r, idx_ref):
        slot = r % 2

        def issue(i, _):
            pltpu.make_async_copy(y_buf.at[slot, pl.ds(i, 1), :],
                                  yc_ref.at[pl.ds(idx_ref[0, 0, r * tm + i], 1), :], sem_out.at[slot]).start()
            return 0

        lax.fori_loop(0, tm, issue, 0, unroll=FFN_ISSUE_UNROLL)

    def wait_gather(slot):
        pltpu.make_async_copy(hf_ref.at[pl.ds(0, tm), :], x_buf.at[slot], sem_in.at[slot]).wait()

    def wait_scatter(slot):
        pltpu.make_async_copy(y_buf.at[slot], yc_ref.at[pl.ds(0, tm), :], sem_out.at[slot]).wait()

    assert n_tiles == 3
    for step, tile in ((0, 2), (1, 1)):
        @pl.when(f == step)
        def _():
            gather(step)
            pl.when(e > 0)(functools.partial(scatter, tile, dstp_ref))

    cols = pl.ds(pl.multiple_of(f * fc, fc), fc)
    wg_b[:, cols] = wg_ref[...].astype(BF16)
    wu_b[:, cols] = wu_ref[...].astype(BF16)
    wd_b[cols, :] = wd_ref[...].astype(BF16)

    @pl.when(f == n_f - 1)
    def _():
        for r in range(n_tiles):
            slot = r % 2
            if r == 1:
                gather(2)
            wait_gather(slot)
            x = x_buf[slot].astype(BF16)
            gate = jnp.dot(x, wg_b[...], preferred_element_type=F32)
            up = jnp.dot(x, wu_b[...], preferred_element_type=F32)
            y = jnp.dot((_silu(gate) * up).astype(BF16), wd_b[...], preferred_element_type=F32)
            if r == 2:
                wait_scatter(slot)
            else:
                pl.when(e > 0)(functools.partial(wait_scatter, slot))
            y_buf[slot] = y * lists_ref[r * tm:(r + 1) * tm, 2:3]
            if r == 0:
                scatter(0, dst_ref)

        @pl.when(e == n_e - 1)
        def _():
            scatter(1, dst_ref)
            scatter(2, dst_ref)
            wait_scatter(1)
            wait_scatter(0)


def expert_ffn(tok, dst, hf, lists, w_gate, w_up, w_down, layer):
    fc = FFN_FC
    dst3 = dst.reshape(N_EXPERTS, 1, SLOTS)
    idx_spec = pl.BlockSpec((1, 1, SLOTS), lambda e, f: (e, 0, 0), memory_space=pltpu.SMEM)
    return pl.pallas_call(
        _ffn_kernel,
        grid=(N_EXPERTS, D_FF_EXPERT // fc),
        in_specs=[
            idx_spec, idx_spec,
            pl.BlockSpec((1, 1, SLOTS), lambda e, f: (jnp.maximum(e - 1, 0), 0, 0), memory_space=pltpu.SMEM),
            pl.BlockSpec(memory_space=pl.ANY),
            pl.BlockSpec((None, SLOTS, LANES), lambda e, f: (e, 0, 0)),
            pl.BlockSpec((None, None, D_MODEL, fc), lambda e, f: (layer, e, 0, f)),
            pl.BlockSpec((None, None, D_MODEL, fc), lambda e, f: (layer, e, 0, f)),
            pl.BlockSpec((None, None, fc, D_MODEL), lambda e, f: (layer, e, f, 0)),
        ],
        out_specs=pl.BlockSpec(memory_space=pl.ANY),
        out_shape=jax.ShapeDtypeStruct((N_CHOICES, D_MODEL), F32),
        scratch_shapes=[pltpu.VMEM((D_MODEL, D_FF_EXPERT), BF16), pltpu.VMEM((D_MODEL, D_FF_EXPERT), BF16),
                        pltpu.VMEM((D_FF_EXPERT, D_MODEL), BF16),
                        pltpu.VMEM((2, FFN_TM, D_MODEL), F32), pltpu.VMEM((2, FFN_TM, D_MODEL), F32),
                        pltpu.SemaphoreType.DMA((2,)), pltpu.SemaphoreType.DMA((2,))],
        compiler_params=_params(("arbitrary", "arbitrary")),
        name="expert_ffn",
    )(tok.reshape(N_EXPERTS, 1, SLOTS), dst3, dst3, hf, lists, w_gate, w_up, w_down)


COMB_TM = 256
COMB_ROWS = 640
N_CHOICES = N_EXPERTS * SLOTS


def _combine_kernel(rows_ref, y_ref, cnt_ref, first_ref, x1_ref, mod_ref, g_ref, b_ref, *rest, split):
    if split:
        oc_ref, ol_ref, buf, sem = rest
    else:
        o_ref, buf, sem = rest
    i, n = pl.program_id(0), pl.num_programs(0)
    ln = COMB_ROWS

    def base(t):
        return lax.div(rows_ref[t], SUBLANES) * SUBLANES

    def start(t, k):
        return pl.multiple_of(jnp.minimum(base(t) + k * ln, N_CHOICES - ln), SUBLANES)

    def copy(t, k, slot):
        return pltpu.make_async_copy(y_ref.at[pl.ds(start(t, k), ln), :], buf.at[slot], sem.at[slot])

    slot = lax.rem(i, 2)

    @pl.when(i == 0)
    def _():
        copy(0, 0, 0).start()

    @pl.when(i + 1 < n)
    def _():
        copy(i + 1, 0, 1 - slot).start()

    copy(i, 0, slot).wait()
    first = first_ref[:, 0:1]
    last = first + cnt_ref[:, 0:1]
    col = lax.broadcasted_iota(jnp.int32, (1, ln), 1)

    def contribution(k, slot_k):
        row_id = (start(i, k) + col).astype(F32)
        mine = (row_id >= first) & (row_id < last) & (row_id >= (base(i) + k * ln).astype(F32))
        onehot = jnp.where(mine, 1.0, 0.0).astype(BF16)
        y = buf[slot_k]
        hi = y.astype(BF16)
        lo = (y - hi.astype(F32)).astype(BF16)
        return jnp.dot(onehot, hi, preferred_element_type=F32) + jnp.dot(onehot, lo, preferred_element_type=F32)

    def extra(k, acc):
        c = copy(i, k, 2)
        c.start()
        c.wait()
        return acc + contribution(k, 2)

    n_groups = lax.div(rows_ref[i + 1] - base(i) + ln - 1, ln)
    acc = lax.fori_loop(1, n_groups, extra, contribution(0, slot))
    g2 = mod_ref[...][5:6]
    out = _layer_norm(DN_ALPHA * x1_ref[...] + g2 * acc, g_ref[...], b_ref[...])
    if split:
        is_ctx = i < N_CTX // COMB_TM

        @pl.when(is_ctx)
        def _():
            oc_ref[...] = out

        @pl.when(jnp.logical_not(is_ctx))
        def _():
            ol_ref[...] = out
    else:
        o_ref[...] = out


def moe_combine(tile_rows, y_choices, cnt, first, x1, mod6, ln_g, ln_b, layer, split):
    tm = COMB_TM
    n_ctx_tiles = N_CTX // tm
    row = lambda w: pl.BlockSpec((tm, w), lambda i, t: (i, 0))
    if split:
        out_specs = [pl.BlockSpec((tm, D_MODEL), lambda i, t: (jnp.minimum(i, n_ctx_tiles - 1), 0)),
                     pl.BlockSpec((tm, D_MODEL), lambda i, t: (jnp.maximum(i - n_ctx_tiles, 0), 0))]
        out_shape = [jax.ShapeDtypeStruct((N_CTX, D_MODEL), F32), jax.ShapeDtypeStruct((N_LAT, D_MODEL), F32)]
    else:
        out_specs = row(D_MODEL)
        out_shape = jax.ShapeDtypeStruct((N_TOK, D_MODEL), F32)
    grid_spec = pltpu.PrefetchScalarGridSpec(
        num_scalar_prefetch=1,
        grid=(N_TOK // tm,),
        in_specs=[
            pl.BlockSpec(memory_space=pl.ANY),
            row(LANES), row(LANES), row(D_MODEL),
            pl.BlockSpec((None, 6, D_MODEL), lambda i, t: (layer * N_MODROWS + _mod_row(i, tm), 0, 0)),
            pl.BlockSpec((None, 1, D_MODEL), lambda i, t: (layer, 0, 0)),
            pl.BlockSpec((None, 1, D_MODEL), lambda i, t: (layer, 0, 0)),
        ],
        out_specs=out_specs,
        scratch_shapes=[pltpu.VMEM((3, COMB_ROWS, D_MODEL), F32), pltpu.SemaphoreType.DMA((3,))],
    )
    return pl.pallas_call(
        functools.partial(_combine_kernel, split=split),
        grid_spec=grid_spec,
        out_shape=out_shape,
        compiler_params=_params(("arbitrary",)),
        name="moe_combine",
    )(tile_rows, y_choices, cnt, first, x1, mod6, ln_g, ln_b)


def moe_block(x1, hf, p_t, mod6, ln_g, ln_b, w_gate, w_up, w_down, layer, split):
    thr, need = route_threshold(p_t)
    parts = []
    for si, (tok0, n, cap) in enumerate(ROUTE_SETS):
        p_blk = p_t[:, tok0:tok0 + n].reshape(N_EXPERTS, n // LANES, LANES)
        parts.append(route_lists(p_blk, thr, need, si))
    lists = jnp.concatenate([p[0] for p in parts], axis=1)
    cnt = jnp.concatenate([p[1] for p in parts], axis=0)
    first = jnp.concatenate([p[2] for p in parts], axis=0)
    tok = lists[:, :, 0].astype(jnp.int32)
    dst = lists[:, :, 1].astype(jnp.int32)
    y_choices = expert_ffn(tok, dst, hf, lists, w_gate, w_up, w_down, layer)
    tile_rows = jnp.concatenate([first[::COMB_TM, 0], jnp.full((1,), N_CHOICES, F32)]).astype(jnp.int32)
    return moe_combine(tile_rows, y_choices, cnt, first, x1, mod6, ln_g, ln_b, layer, split)


def kernel(x_prompt, x_sample, cache_mla_ckv, cache_mla_kpe, state_hgrn, state_rglru, c, c_ctx,
           w_in, w_out, hg_lb_logits, hg_norm_g, rg_conv_w, rg_conv_b, rg_w_r, rg_b_r, rg_w_i, rg_b_i,
           rg_lambda, mla_q_norm_g, mla_kv_norm_g, mla_w_uq, mla_w_uk, mla_w_uv, ada_w, ada_b,
           ln1_g, ln1_b, ln2_g, ln2_b, moe_router, moe_w_gate, moe_w_up, moe_w_down):
    x = jnp.concatenate([x_prompt.reshape(N_CTX, D_MODEL), x_sample.reshape(N_LAT, D_MODEL)], axis=0)
    cvec = jnp.concatenate([c_ctx[None, :], c, jnp.zeros((SUBLANES - N_MODROWS, D_MODEL), F32)], axis=0)
    mod = ada_mod(cvec, ada_w, ada_b)
    mod6 = mod[:, :N_MODROWS].reshape(DEPTH * N_MODROWS, 6, D_MODEL)
    cos_t, sin_t = rope_tables()
    vec = lambda a: a.reshape(DEPTH, 1, a.shape[-1])
    hg_ng, cb, qg, kg = vec(hg_norm_g), vec(rg_conv_b), vec(mla_q_norm_g), vec(mla_kv_norm_g)
    g1, b1, g2, b2 = vec(ln1_g), vec(ln1_b), vec(ln2_g), vec(ln2_b)
    w_r, w_i, w_in_b = rg_w_r.astype(BF16), rg_w_i.astype(BF16), w_in.astype(BF16)
    router = jnp.pad(moe_router, ((0, 0), (0, 0), (0, LANES - N_EXPERTS))).astype(BF16)

    ckvs, kpes, hgs, rgs = [], [], [], []
    for l in range(DEPTH):
        z_hg, z_rg, z_cq, z_kv = in_proj(x, mod6, w_in_b[l], prep_w_pe(w_in[l]), l)

        o_hg_f, o_hg_b, hg_fin_f, hg_fin_b = hgrn_mixer(z_hg, hg_lb_logits, state_hgrn, l)

        rg_args = (rg_conv_w, cb, w_r, rg_b_r, w_i, rg_b_i, rg_lambda)
        o_rg_c, rg_fin = rglru_mixer(z_rg, *rg_args, None, l, False)
        (o_rg_l,) = rglru_mixer(z_rg, *rg_args, state_rglru, l, True)
        o_rg = (o_rg_c, o_rg_l)

        w_uk, w_uv = mla_w_uk[l].astype(BF16), mla_w_uv[l].astype(BF16)
        q = q_proj(z_cq, qg, prep_w_uq(mla_w_uq[l]), cos_t, sin_t, l)
        k_tok, v_tok, ckv_n, kpe = kv_proj(z_kv, kg, w_uk, w_uv, cos_t, sin_t, l)
        cache = jnp.concatenate([cache_mla_ckv[:, l].reshape(DEC_BATCH * PAST_LEN, KV_LORA),
                                 cache_mla_kpe[:, l].reshape(DEC_BATCH * PAST_LEN, MLA_ROPE_DIM),
                                 jnp.zeros((DEC_BATCH * PAST_LEN, MLA_ROPE_DIM), F32)], axis=1)
        k_cache, v_cache = kv_proj_cache(cache, kg, w_uk, w_uv, cos_t[:PAST_LEN], sin_t[:PAST_LEN], l)
        o_mla = (mla_attention(q, k_tok, v_tok, None, None, False),
                 mla_attention(q, k_tok, v_tok, k_cache, v_cache, True))

        x1, hf, p_t = out_proj(o_hg_f, o_hg_b, z_hg, hg_ng, o_rg, o_mla, w_out[l].astype(BF16), x, mod6, g1, b1,
                               router[l], l)
        x = moe_block(x1, hf, p_t, mod6, g2, b2, moe_w_gate, moe_w_up, moe_w_down, l, split=(l == DEPTH - 1))

        ckvs.append(ckv_n[:N_CTX].reshape(BATCH, SEQ, KV_LORA))
        kpes.append(kpe[:N_CTX].reshape(BATCH, SEQ, MLA_ROPE_DIM))
        hgs.append(jnp.stack([hg_fin_f, hg_fin_b], axis=1))
        rgs.append(rg_fin)

    y_prompt = x[0].reshape(BATCH, SEQ, D_MODEL)
    y_sample = x[1].reshape(DEC_BATCH, DEC_SEQ, D_MODEL)
    return (y_prompt, y_sample, jnp.stack(ckvs, axis=1), jnp.stack(kpes, axis=1),
            jnp.stack(hgs, axis=1), jnp.stack(rgs, axis=1))
```
